```python
import jax, jax.numpy as jnp
from jax import lax
import numpy as np

D_MODEL = 1024
BATCH = 16
SEQ = 2048
DEPTH = 1
DEC_BATCH = 32
DEC_SEQ = 8
PAST_LEN = 16384
PAGE_SIZE = 128

HEAD_DIM = 64
N_RET_HEADS = 8
N_ATT_HEADS = 8
RET_WIDTH = N_RET_HEADS * HEAD_DIM
ATT_WIDTH = N_ATT_HEADS * HEAD_DIM
MIX_WIDTH = RET_WIDTH + ATT_WIDTH
IN_COLS = 4 * RET_WIDTH + 3 * ATT_WIDTH
RET_CHUNK = 128
DIL_PATTERNS = ((128, 1), (512, 4), (2048, 16))
WIN_MAX = 2048
ATT_BLOCK = 128
N_MEM = 256
N_MEM_HEADS = 4
MEM_HEAD_DIM = D_MODEL // N_MEM_HEADS
N_GROUPS = 4
N_EXP_PER_GROUP = 8
N_EXPERTS = N_GROUPS * N_EXP_PER_GROUP
TOP_K_IN_GROUP = 2
D_EXPERT = 256
MOE_TOKEN_BLOCK = 2048
EPS = 1e-6
F32 = jnp.float32

kernel_name = 'hymba_retention_dilated_hmoe_step'


def rmsnorm(x, g):
    x32 = x.astype(F32)
    y = x32 * lax.rsqrt(jnp.mean(x32 * x32, axis=-1, keepdims=True) + EPS)
    return (y * g.astype(F32)).astype(x.dtype)


def retention_log_decay():
    gamma = 1.0 - 2.0 ** (-5.0 - jnp.arange(N_RET_HEADS, dtype=F32))
    return jnp.log(gamma)


def alibi_slopes():
    return 2.0 ** (-8.0 * jnp.arange(1, N_ATT_HEADS + 1, dtype=F32) / N_ATT_HEADS)


def project_mixers(h, w_in, g_q_att, g_k_att):
    B, L, _ = h.shape
    z = h @ w_in
    cuts = [RET_WIDTH, 2 * RET_WIDTH, 3 * RET_WIDTH, 4 * RET_WIDTH,
            4 * RET_WIDTH + ATT_WIDTH, 4 * RET_WIDTH + 2 * ATT_WIDTH]
    q_r, k_r, v_r, g_r, q_a, k_a, v_a = jnp.split(z, cuts, axis=-1)
    heads = lambda t: t.reshape(B, L, -1, HEAD_DIM)
    return (heads(q_r), heads(k_r), heads(v_r), g_r,
            rmsnorm(heads(q_a), g_q_att), rmsnorm(heads(k_a), g_k_att), heads(v_a))


def retention(q, k, v, s0):
    B, L, H, Dh = q.shape
    C = RET_CHUNK if L % RET_CHUNK == 0 else L
    n = L // C
    lg = retention_log_decay()
    qc = q.astype(F32).reshape(B, n, C, H, Dh)
    kc = (k.astype(F32) * (Dh ** -0.5)).reshape(B, n, C, H, Dh)
    vc = v.astype(F32).reshape(B, n, C, H, Dh)
    i = jnp.arange(C, dtype=F32)
    rel = i[:, None] - i[None, :]
    dmat = jnp.where(rel >= 0, jnp.exp(lg[:, None, None] * jnp.maximum(rel, 0.0)), 0.0)
    s = jnp.einsum('bnihd,bnjhd->bnhij', qc, kc) * dmat
    o_in = jnp.einsum('bnhij,bnjhe->bnihe', s, vc)
    w_k = jnp.exp(lg[:, None] * (C - 1.0 - i)[None, :])
    kv = jnp.einsum('bnjhd,bnjhe,hj->nbhde', kc, vc, w_k)
    g_chunk = jnp.exp(lg * C)[None, :, None, None]

    def step(state, kv_c):
        return g_chunk * state + kv_c, state

    s_fin, s_prev = lax.scan(step, s0.astype(F32), kv)
    w_q = jnp.exp(lg[None, :] * (i + 1.0)[:, None])
    o_x = jnp.einsum('bnihd,nbhde->bnihe', qc, s_prev) * w_q[None, None, :, :, None]
    return (o_in + o_x).reshape(B, L, H, Dh), s_fin


def head_groupnorm(o, g):
    B, L, H, Dh = o.shape
    c = o - jnp.mean(o, axis=-1, keepdims=True)
    y = c * lax.rsqrt(jnp.mean(c * c, axis=-1, keepdims=True) + EPS)
    return y.reshape(B, L, H * Dh) * g.astype(F32)


def dilated_attn_strided(q, k, v, window, dil):
    B, S, H, Dh = q.shape
    n_off = window // dil
    L = S // dil
    nb = -(-L // ATT_BLOCK)
    Lp = nb * ATT_BLOCK

    def split(t):
        t = jnp.moveaxis(t.reshape(B, L, dil, H, Dh), 2, 1)
        t = jnp.pad(t, ((0, 0), (0, 0), (0, Lp - L), (0, 0), (0, 0)))
        return t.reshape(B, dil, nb, ATT_BLOCK, H, Dh)

    def with_prev(t):
        prev = jnp.concatenate([jnp.zeros_like(t[:, :, :1]), t[:, :, :-1]], axis=2)
        return jnp.concatenate([prev, t], axis=3)

    qb = split(q)
    kk = with_prev(split(k))
    vv = with_prev(split(v))
    i = jnp.arange(ATT_BLOCK)
    j = jnp.arange(2 * ATT_BLOCK)
    rel = ATT_BLOCK + i[:, None] - j[None, :]
    kpos = jnp.arange(nb)[:, None] * ATT_BLOCK - ATT_BLOCK + j[None, :]
    valid = (rel >= 0) & (rel <= n_off) & (kpos[:, None, :] >= 0)
    s = jnp.einsum('bgnihd,bgnjhd->bgnhij', qb, kk).astype(F32) * (Dh ** -0.5)
    s = s - alibi_slopes()[:, None, None] * (rel * dil).astype(F32)
    s = jnp.where(valid[:, None], s, -jnp.inf)
    m = jnp.max(s, axis=-1)
    p = jnp.exp(s - m[..., None])
    l = jnp.sum(p, axis=-1)
    o = jnp.einsum('bgnhij,bgnjhd->bgnihd', p, vv.astype(F32)) / jnp.swapaxes(l, 3, 4)[..., None]

    def unsplit(t):
        t = t.reshape((B, dil, Lp) + t.shape[4:])[:, :, :L]
        return jnp.moveaxis(t, 1, 2).reshape((B, S) + t.shape[3:])

    return unsplit(o), unsplit(jnp.swapaxes(m, 3, 4)), unsplit(jnp.swapaxes(l, 3, 4))


def dilated_attn_gather(q, k_ext, v_ext, window, dil):
    B, L, H, Dh = q.shape
    P = k_ext.shape[1] - L
    n = jnp.arange(window // dil + 1)
    idx = P + jnp.arange(L)[:, None] - dil * n[None, :]
    valid = idx >= 0
    idx = jnp.maximum(idx, 0)
    kg = k_ext[:, idx]
    vg = v_ext[:, idx]
    s = jnp.einsum('bihd,binhd->bhin', q, kg).astype(F32) * (Dh ** -0.5)
    s = s - alibi_slopes()[:, None, None] * (dil * n).astype(F32)
    s = jnp.where(valid, s, -jnp.inf)
    m = jnp.max(s, axis=-1)
    p = jnp.exp(s - m[..., None])
    l = jnp.sum(p, axis=-1)
    o = jnp.einsum('bhin,binhd->bihd', p, vg.astype(F32)) / jnp.swapaxes(l, 1, 2)[..., None]
    return o, jnp.swapaxes(m, 1, 2), jnp.swapaxes(l, 1, 2)


def combine_dilation_groups(outs):
    o_all = jnp.stack([o for o, _, _ in outs])
    m_all = jnp.stack([m for _, m, _ in outs])
    l_all = jnp.stack([l for _, _, l in outs])
    w = l_all * jnp.exp(m_all - jnp.max(m_all, axis=0, keepdims=True))
    return jnp.sum(w[..., None] * o_all, axis=0) / jnp.sum(w, axis=0)[..., None]


def mixer_output(o_r, g_r, o_a, g_ret_out, w_o, dtype):
    B, L = o_r.shape[:2]
    r = head_groupnorm(o_r, g_ret_out) * jax.nn.silu(g_r.astype(F32))
    a = o_a.reshape(B, L, ATT_WIDTH)
    return jnp.concatenate([r, a], axis=-1).astype(dtype) @ w_o


def memory_kv(mem, g_mem, w_mk, w_mv, g_k_mem):
    B, M, _ = mem.shape
    h = rmsnorm(mem, g_mem)
    k = rmsnorm((h @ w_mk).reshape(B, M, N_MEM_HEADS, MEM_HEAD_DIM), g_k_mem)
    v = (h @ w_mv).reshape(B, M, N_MEM_HEADS, MEM_HEAD_DIM)
    return k, v


def memory_attend(h, mk, mv, w_mq, g_q_mem, w_mo):
    B, L, _ = h.shape
    q = rmsnorm((h @ w_mq).reshape(B, L, N_MEM_HEADS, MEM_HEAD_DIM), g_q_mem)
    s = jnp.einsum('blhd,bmhd->bhlm', q, mk).astype(F32) * (MEM_HEAD_DIM ** -0.5)
    p = jax.nn.softmax(s, axis=-1)
    o = jnp.einsum('bhlm,bmhd->blhd', p, mv.astype(F32)).reshape(B, L, D_MODEL)
    return o.astype(h.dtype) @ w_mo


def hier_moe(h, w_rg, b_rg, w_re, b_re, w_eg, w_eu, w_ed):
    B, L, D = h.shape
    T = B * L
    blk = min(MOE_TOKEN_BLOCK, T)
    nblk = -(-T // blk)
    t_all = jnp.pad(h.reshape(T, D), ((0, nblk * blk - T), (0, 0))).reshape(nblk, blk, D)

    def one_block(t):
        lg = (t @ w_rg).astype(F32) + b_rg.astype(F32)
        pg = jax.nn.softmax(lg, axis=-1)
        oh_g = jax.nn.one_hot(jnp.argmax(lg, axis=-1), N_GROUPS, dtype=F32)
        p_top = jnp.sum(pg * oh_g, axis=-1)
        le = jnp.einsum('td,gde->tge', t, w_re).astype(F32) + b_re.astype(F32)
        le_sel = jnp.einsum('tge,tg->te', le, oh_g)
        top_v, top_i = lax.top_k(le_sel, TOP_K_IN_GROUP)
        w2 = jax.nn.softmax(top_v, axis=-1) * p_top[:, None]
        in_group = jnp.sum(jax.nn.one_hot(top_i, N_EXP_PER_GROUP, dtype=F32) * w2[..., None], axis=1)
        gate = (oh_g[:, :, None] * in_group[:, None, :]).reshape(-1, N_EXPERTS)
        hid = jax.nn.silu(jnp.einsum('td,edf->tef', t, w_eg)) * jnp.einsum('td,edf->tef', t, w_eu)
        return jnp.einsum('tef,efd->td', hid * gate[..., None].astype(hid.dtype), w_ed)

    y = lax.map(one_block, t_all).reshape(nblk * blk, D)[:T]
    return y.reshape(B, L, D).astype(h.dtype)


def setup_inputs(seed: int = 0) -> dict:
    key = jax.random.key(seed)
    k = jax.random.split(key, 32)
    nrm = lambda i, shape, scale: jax.random.normal(k[i], shape, F32) * scale
    gain = lambda i, shape: 1.0 + 0.05 * jax.random.normal(k[i], shape, F32)
    w_buf = min(WIN_MAX, PAST_LEN)
    Dm = D_MODEL
    return {
        'x_prompt': nrm(0, (BATCH, SEQ, Dm), 1.0),
        'x_sample': nrm(1, (DEC_BATCH, DEC_SEQ, Dm), 1.0),
        'mem_prompt': nrm(2, (BATCH, N_MEM, Dm), 1.0),
        'state_ret': nrm(3, (DEPTH, DEC_BATCH, N_RET_HEADS, HEAD_DIM, HEAD_DIM), 0.5),
        'cache_win_k': nrm(4, (DEPTH, DEC_BATCH, w_buf, N_ATT_HEADS, HEAD_DIM), 1.0),
        'cache_win_v': nrm(5, (DEPTH, DEC_BATCH, w_buf, N_ATT_HEADS, HEAD_DIM), 1.0),
        'cache_mem_k': nrm(6, (DEPTH, DEC_BATCH, N_MEM, N_MEM_HEADS, MEM_HEAD_DIM), 1.0),
        'cache_mem_v': nrm(7, (DEPTH, DEC_BATCH, N_MEM, N_MEM_HEADS, MEM_HEAD_DIM), 1.0),
        'g_mix': gain(8, (DEPTH, Dm)),
        'w_in': nrm(9, (DEPTH, Dm, IN_COLS), Dm ** -0.5),
        'g_ret_out': gain(10, (DEPTH, RET_WIDTH)),
        'g_q_att': gain(11, (DEPTH, HEAD_DIM)),
        'g_k_att': gain(12, (DEPTH, HEAD_DIM)),
        'w_o': nrm(13, (DEPTH, MIX_WIDTH, Dm), MIX_WIDTH ** -0.5),
        'g_cross': gain(14, (DEPTH, Dm)),
        'g_mem': gain(15, (DEPTH, Dm)),
        'w_mq': nrm(16, (DEPTH, Dm, Dm), Dm ** -0.5),
        'w_mk': nrm(17, (DEPTH, Dm, Dm), Dm ** -0.5),
        'w_mv': nrm(18, (DEPTH, Dm, Dm), Dm ** -0.5),
        'g_q_mem': gain(19, (DEPTH, MEM_HEAD_DIM)),
        'g_k_mem': gain(20, (DEPTH, MEM_HEAD_DIM)),
        'w_mo': nrm(21, (DEPTH, Dm, Dm), Dm ** -0.5),
        'g_moe': gain(22, (DEPTH, Dm)),
        'w_router_group': nrm(23, (DEPTH, Dm, N_GROUPS), Dm ** -0.5),
        'b_router_group': nrm(24, (DEPTH, N_GROUPS), 0.01),
        'w_router_expert': nrm(25, (DEPTH, N_GROUPS, Dm, N_EXP_PER_GROUP), Dm ** -0.5),
        'b_router_expert': nrm(26, (DEPTH, N_GROUPS, N_EXP_PER_GROUP), 0.01),
        'w_exp_gate': nrm(27, (DEPTH, N_EXPERTS, Dm, D_EXPERT), Dm ** -0.5),
        'w_exp_up': nrm(28, (DEPTH, N_EXPERTS, Dm, D_EXPERT), Dm ** -0.5),
        'w_exp_down': nrm(29, (DEPTH, N_EXPERTS, D_EXPERT, Dm), D_EXPERT ** -0.5),
    }


def reference(x_prompt, x_sample, mem_prompt, state_ret, cache_win_k, cache_win_v,
              cache_mem_k, cache_mem_v, g_mix, w_in, g_ret_out, g_q_att, g_k_att, w_o,
              g_cross, g_mem, w_mq, w_mk, w_mv, g_q_mem, g_k_mem, w_mo, g_moe,
              w_router_group, b_router_group, w_router_expert, b_router_expert,
              w_exp_gate, w_exp_up, w_exp_down):
    yp, ys = x_prompt, x_sample
    S = x_prompt.shape[1]
    w_keep = min(WIN_MAX, S)
    p_ret, p_wk, p_wv, p_mk, p_mv, s_ret, s_wk, s_wv = [], [], [], [], [], [], [], []
    for l in range(DEPTH):
        q_r, k_r, v_r, g_r, q_a, k_a, v_a = project_mixers(rmsnorm(yp, g_mix[l]), w_in[l], g_q_att[l], g_k_att[l])
        s0 = jnp.zeros((yp.shape[0], N_RET_HEADS, HEAD_DIM, HEAD_DIM), F32)
        o_r, r_fin = retention(q_r, k_r, v_r, s0)
        o_a = combine_dilation_groups([dilated_attn_strided(q_a, k_a, v_a, w, d) for (w, d) in DIL_PATTERNS])
        yp = yp + mixer_output(o_r, g_r, o_a, g_ret_out[l], w_o[l], yp.dtype)
        mk, mv = memory_kv(mem_prompt, g_mem[l], w_mk[l], w_mv[l], g_k_mem[l])
        yp = yp + memory_attend(rmsnorm(yp, g_cross[l]), mk, mv, w_mq[l], g_q_mem[l], w_mo[l])
        yp = yp + hier_moe(rmsnorm(yp, g_moe[l]), w_router_group[l], b_router_group[l],
                           w_router_expert[l], b_router_expert[l], w_exp_gate[l], w_exp_up[l], w_exp_down[l])
        p_ret.append(r_fin)
        p_wk.append(k_a[:, S - w_keep:])
        p_wv.append(v_a[:, S - w_keep:])
        p_mk.append(mk)
        p_mv.append(mv)
        q_r, k_r, v_r, g_r, q_a, k_a, v_a = project_mixers(rmsnorm(ys, g_mix[l]), w_in[l], g_q_att[l], g_k_att[l])
        o_r, r_fin = retention(q_r, k_r, v_r, state_ret[l])
        k_ext = jnp.concatenate([cache_win_k[l].astype(k_a.dtype), k_a], axis=1)
        v_ext = jnp.concatenate([cache_win_v[l].astype(v_a.dtype), v_a], axis=1)
        o_a = combine_dilation_groups([dilated_attn_gather(q_a, k_ext, v_ext, w, d) for (w, d) in DIL_PATTERNS])
        ys = ys + mixer_output(o_r, g_r, o_a, g_ret_out[l], w_o[l], ys.dtype)
        ys = ys + memory_attend(rmsnorm(ys, g_cross[l]), cache_mem_k[l], cache_mem_v[l], w_mq[l], g_q_mem[l], w_mo[l])
        ys = ys + hier_moe(rmsnorm(ys, g_moe[l]), w_router_group[l], b_router_group[l],
                           w_router_expert[l], b_router_expert[l], w_exp_gate[l], w_exp_up[l], w_exp_down[l])
        s_ret.append(r_fin.astype(state_ret.dtype))
        s_wk.append(k_a)
        s_wv.append(v_a)
    return (yp, ys, jnp.stack(p_ret), jnp.stack(p_wk), jnp.stack(p_wv), jnp.stack(p_mk), jnp.stack(p_mv),
            jnp.stack(s_ret), jnp.stack(s_wk), jnp.stack(s_wv))
```

```python
import functools

import jax
import jax.numpy as jnp
from jax import lax
from jax.experimental import pallas as pl
from jax.experimental.pallas import tpu as pltpu

F32 = jnp.float32
BF16 = jnp.bfloat16
I32 = jnp.int32

D_MODEL = 1024
HEAD_DIM = 64
N_HEADS = 8
WIDTH = N_HEADS * HEAD_DIM
N_PAIRS = N_HEADS // 2
IN_COLS = 7 * WIDTH
RET_CHUNK = 128
ATT_BLOCK = 128
WIN_STEPS = 128
DILATIONS = (1, 4, 16)
N_MEM = 256
N_MEM_HEADS = 4
MEM_HEAD_DIM = 256
N_GROUPS = 4
N_EXP_PER_GROUP = 8
N_EXPERTS = 32
D_EXPERT = 256
EPS = 1e-6
LANES = 128
EXPERT_LANE0 = 32
MOE_ROW_TILE = 256
VMEM_LIMIT = 56 * 1024 * 1024

NEG_INF = float("-inf")


def _cparams(n_axes, vmem=VMEM_LIMIT):
    return pltpu.CompilerParams(dimension_semantics=("arbitrary",) * n_axes,
                                vmem_limit_bytes=vmem)


def _dot(a, b):
    return jnp.dot(a, b, preferred_element_type=F32)


def _dot_nt(a, b):
    return lax.dot_general(a, b, (((1,), (1,)), ((), ())), preferred_element_type=F32)


def _dot_tn(a, b):
    return lax.dot_general(a, b, (((0,), (0,)), ((), ())), preferred_element_type=F32)


def _rms(x, g):
    ms = jnp.mean(x * x, axis=-1, keepdims=True)
    return x * lax.rsqrt(ms + EPS) * g


def _mix_proj_kernel(x_ref, g_ref, w_ref, gq_ref, gk_ref, bd_ref,
                     qr_ref, kr_ref, vr_ref, gr_ref, qa_ref, ka_ref, va_ref):
    act = qr_ref.dtype
    h = _rms(x_ref[...], g_ref[...]).astype(BF16)

    def proj(j):
        return _dot(h, w_ref[:, j * WIDTH:(j + 1) * WIDTH])

    def head_norm(z, g):
        sq = z * z
        hi = sq.astype(BF16)
        lo = (sq - hi.astype(F32)).astype(BF16)
        ss = _dot(hi, bd_ref[...]) + _dot(lo, bd_ref[...])
        return z * lax.rsqrt(ss * (1.0 / HEAD_DIM) + EPS) * g

    qr_ref[...] = proj(0).astype(act)
    kr_ref[...] = proj(1).astype(act)
    vr_ref[...] = proj(2).astype(act)
    gr_ref[...] = proj(3).astype(act)
    qa_ref[...] = head_norm(proj(4), gq_ref[...]).astype(act)
    ka_ref[...] = head_norm(proj(5), gk_ref[...])
    va_ref[...] = proj(6)


def _mix_proj(x2d, g_mix, w_in_bf, gq_t, gk_t, bd, tm, act_dtype):
    t = x2d.shape[0]
    const = lambda i: (0, 0)
    row = lambda i: (i, 0)
    out_bf = jax.ShapeDtypeStruct((t, WIDTH), act_dtype)
    out_f = jax.ShapeDtypeStruct((t, WIDTH), F32)
    return pl.pallas_call(
        _mix_proj_kernel,
        grid=(t // tm,),
        in_specs=[
            pl.BlockSpec((tm, D_MODEL), row),
            pl.BlockSpec((1, D_MODEL), const),
            pl.BlockSpec((D_MODEL, IN_COLS), const),
            pl.BlockSpec((1, WIDTH), const),
            pl.BlockSpec((1, WIDTH), const),
            pl.BlockSpec((WIDTH, WIDTH), const),
        ],
        out_specs=[pl.BlockSpec((tm, WIDTH), row)] * 7,
        out_shape=[out_bf, out_bf, out_bf, out_bf, out_bf, out_f, out_f],
        compiler_params=_cparams(1),
        name="mix_proj",
    )(x2d, g_mix, w_in_bf, gq_t, gk_t, bd)


def _pair_masks(shape):
    lane = lax.broadcasted_iota(I32, shape, len(shape) - 1)
    return lane < HEAD_DIM


def _segment_mean(x, is_lo):
    zero = jnp.zeros_like(x)
    lo = jnp.sum(jnp.where(is_lo, x, zero), axis=-1, keepdims=True)
    hi = jnp.sum(jnp.where(is_lo, zero, x), axis=-1, keepdims=True)
    return jnp.where(is_lo, lo, hi) * (1.0 / HEAD_DIM)


def _group_norm_gate(o, g_r, g_out, is_lo):
    c = o - _segment_mean(o, is_lo)
    y = c * lax.rsqrt(_segment_mean(c * c, is_lo) + EPS) * g_out
    g = g_r.astype(F32)
    return y * (g * (1.0 / (1.0 + jnp.exp(-g))))


def _retention_kernel(lg_ref, gout_ref, q_ref, k_ref, v_ref, g_ref, o_ref, st_ref, *, n_chunks):
    c_len = RET_CHUNK
    shape = (c_len, LANES)
    is_lo = _pair_masks(shape)
    row = lax.broadcasted_iota(I32, shape, 0)
    col = lax.broadcasted_iota(I32, shape, 1)
    rel = (row - col).astype(F32)
    lg_lane = lg_ref[0]
    lg0 = lg_lane[:, 0:1]
    lg1 = lg_lane[:, HEAD_DIM:HEAD_DIM + 1]
    scale = HEAD_DIM ** -0.5
    causal = rel >= 0.0
    relp = jnp.maximum(rel, 0.0)
    d0 = jnp.where(causal, jnp.exp(lg0 * relp), 0.0) * scale
    d1 = jnp.where(causal, jnp.exp(lg1 * relp), 0.0) * scale
    rowf = row.astype(F32)
    w_k = jnp.exp(lg_lane * (c_len - 1.0 - rowf)) * scale
    w_q = jnp.exp(lg_lane * (rowf + 1.0))
    lg_row = jnp.where(row < HEAD_DIM, lg0, lg1)
    g_chunk = jnp.exp(lg_row * float(c_len))
    same_head = (row < HEAD_DIM) == (col < HEAD_DIM)
    g_out = gout_ref[0]

    def body(c, state):
        sl = pl.ds(pl.multiple_of(c * c_len, c_len), c_len)
        qc = q_ref[0, sl, :]
        kc = k_ref[0, sl, :]
        vc = v_ref[0, sl, :]
        zero = jnp.zeros_like(qc)
        q0 = jnp.where(is_lo, qc, zero)
        q1 = jnp.where(is_lo, zero, qc)
        v0 = jnp.where(is_lo, vc, zero)
        v1 = jnp.where(is_lo, zero, vc)
        s0 = (_dot_nt(q0, kc) * d0).astype(BF16)
        s1 = (_dot_nt(q1, kc) * d1).astype(BF16)
        o_in = _dot(s0, v0) + _dot(s1, v1)
        o_x = _dot(qc, state.astype(BF16)) * w_q
        kw = (kc.astype(F32) * w_k).astype(BF16)
        kv = jnp.where(same_head, _dot_tn(kw, vc), 0.0)
        o = o_in + o_x
        o_ref[0, sl, :] = _group_norm_gate(o, g_ref[0, sl, :], g_out, is_lo).astype(BF16)
        return g_chunk * state + kv

    state = lax.fori_loop(0, n_chunks, body, jnp.zeros(shape, F32))
    st_ref[0, 0] = state[:HEAD_DIM, :HEAD_DIM]
    st_ref[0, 1] = state[HEAD_DIM:, HEAD_DIM:]


def _retention(q, k, v, g, lg_pairs, gout_pairs):
    b, s, _ = q.shape
    blk = pl.BlockSpec((1, s, LANES), lambda i, p: (i, 0, p))
    tab = pl.BlockSpec((1, 1, LANES), lambda i, p: (p, 0, 0))
    return pl.pallas_call(
        functools.partial(_retention_kernel, n_chunks=s // RET_CHUNK),
        grid=(b, N_PAIRS),
        in_specs=[tab, tab, blk, blk, blk, blk],
        out_specs=[blk, pl.BlockSpec((1, 2, HEAD_DIM, HEAD_DIM), lambda i, p: (i, p, 0, 0))],
        out_shape=[jax.ShapeDtypeStruct((b, s, WIDTH), BF16),
                   jax.ShapeDtypeStruct((b, N_HEADS, HEAD_DIM, HEAD_DIM), F32)],
        compiler_params=_cparams(2),
        name="retention",
    )(lg_pairs, gout_pairs, q, k, v, g)


def _attn_block(qb, kb, vb, bias0, bias1, is_lo):
    qb = qb.astype(BF16)
    kb = kb.astype(BF16)
    zq = jnp.zeros_like(qb)
    scale = HEAD_DIM ** -0.5
    s0 = _dot_nt(jnp.where(is_lo, qb, zq), kb) * scale + bias0
    s1 = _dot_nt(jnp.where(is_lo, zq, qb), kb) * scale + bias1
    m0 = jnp.max(s0, axis=-1, keepdims=True)
    m1 = jnp.max(s1, axis=-1, keepdims=True)
    p0 = jnp.exp(s0 - m0).astype(BF16)
    p1 = jnp.exp(s1 - m1).astype(BF16)
    is_lo_k = _pair_masks(vb.shape)
    one = jnp.ones_like(vb)
    r0 = _dot(p0, jnp.where(is_lo_k, vb, one).astype(BF16))
    r1 = _dot(p1, jnp.where(is_lo_k, one, vb).astype(BF16))
    acc = jnp.where(is_lo, r0, r1)
    l = pltpu.roll(jnp.where(is_lo, r1, r0), HEAD_DIM, 1)
    m = jnp.where(is_lo, m0, m1)
    return acc, m, l


def _dil_attn_kernel(sl_ref, q_ref, k_ref, v_ref, o_ref,
                     qf_ref, acc_ref, m_ref, l_ref, bias_ref, bias1_ref, *, seq):
    blk = ATT_BLOCK
    is_lo = _pair_masks((blk, LANES))
    slope = sl_ref[0]
    slope0 = slope[:, 0:1]
    slope1 = slope[:, HEAD_DIM:HEAD_DIM + 1]

    qf_ref[...] = q_ref[0].astype(F32)

    i2 = lax.broadcasted_iota(I32, (blk, 2 * blk), 0)
    j2 = lax.broadcasted_iota(I32, (blk, 2 * blk), 1)
    rel2 = blk + i2 - j2
    ok2 = (rel2 >= 0) & (rel2 <= WIN_STEPS)
    i1 = lax.broadcasted_iota(I32, (blk, blk), 0)
    j1 = lax.broadcasted_iota(I32, (blk, blk), 1)
    rel1 = i1 - j1
    ok1 = rel1 >= 0
    for pi, d in enumerate(DILATIONS):
        dist2 = (rel2 * d).astype(F32)
        dist1 = (rel1 * d).astype(F32)
        bias_ref[2 * pi] = jnp.where(ok2, -slope0 * dist2, NEG_INF)
        bias_ref[2 * pi + 1] = jnp.where(ok2, -slope1 * dist2, NEG_INF)
        bias1_ref[2 * pi] = jnp.where(ok1, -slope0 * dist1, NEG_INF)
        bias1_ref[2 * pi + 1] = jnp.where(ok1, -slope1 * dist1, NEG_INF)

    def rows(start, n, d):
        return pl.ds(start, n) if d == 1 else pl.ds(start, n, stride=d)

    def store(pi, sl, res):
        acc, m, l = res
        acc_ref[pi, sl, :] = acc
        m_ref[pi, sl, :] = m
        l_ref[pi, sl, :] = l

    for pi, d in enumerate(DILATIONS):
        sub_len = seq // d
        n_blocks = sub_len // blk

        def residue(r, carry, pi=pi, d=d, n_blocks=n_blocks):
            sl = rows(r, blk, d)
            store(pi, sl, _attn_block(qf_ref[sl, :], k_ref[0, sl, :], v_ref[0, sl, :],
                                      bias1_ref[2 * pi], bias1_ref[2 * pi + 1], is_lo))

            def block(j, carry2):
                sq = rows(r + j * blk * d, blk, d)
                sk = rows(r + (j - 1) * blk * d, 2 * blk, d)
                store(pi, sq, _attn_block(qf_ref[sq, :], k_ref[0, sk, :], v_ref[0, sk, :],
                                          bias_ref[2 * pi], bias_ref[2 * pi + 1], is_lo))
                return carry2

            if n_blocks > 1:
                lax.fori_loop(1, n_blocks, block, 0)
            return carry

        if d == 1:
            residue(0, 0)
        else:
            lax.fori_loop(0, d, residue, 0)

    def merge(c, carry):
        sl = pl.ds(pl.multiple_of(c * 256, 256), 256)
        m_all = [m_ref[pi, sl, :] for pi in range(3)]
        m_max = jnp.maximum(jnp.maximum(m_all[0], m_all[1]), m_all[2])
        num = jnp.zeros((256, LANES), F32)
        den = jnp.zeros((256, LANES), F32)
        for pi in range(3):
            w = jnp.exp(m_all[pi] - m_max)
            num = num + w * acc_ref[pi, sl, :]
            den = den + w * l_ref[pi, sl, :]
        o_ref[0, sl, :] = (num / den).astype(BF16)
        return carry

    lax.fori_loop(0, seq // 256, merge, 0)


def _dilated_attention(qa, ka, va, slope_pairs):
    b, s, _ = qa.shape
    blk = pl.BlockSpec((1, s, LANES), lambda i, p: (i, 0, p))
    tab = pl.BlockSpec((1, 1, LANES), lambda i, p: (p, 0, 0))
    return pl.pallas_call(
        functools.partial(_dil_attn_kernel, seq=s),
        grid=(b, N_PAIRS),
        in_specs=[tab, blk, blk, blk],
        out_specs=blk,
        out_shape=jax.ShapeDtypeStruct((b, s, WIDTH), BF16),
        scratch_shapes=[
            pltpu.VMEM((s, LANES), F32),
            pltpu.VMEM((3, s, LANES), F32),
            pltpu.VMEM((3, s, LANES), F32),
            pltpu.VMEM((3, s, LANES), F32),
            pltpu.VMEM((6, ATT_BLOCK, 2 * ATT_BLOCK), F32),
            pltpu.VMEM((6, ATT_BLOCK, ATT_BLOCK), F32),
        ],
        compiler_params=_cparams(2),
        name="dilated_attention",
    )(slope_pairs, qa, ka, va)


def _sample_mixer_kernel(lg_ref, gout_ref, slope_ref, qr_ref, kr_ref, vr_ref, gr_ref,
                         qa_ref, ka_ref, va_ref, st_ref, ck_ref, cv_ref,
                         r_ref, a_ref, sto_ref, bias_ref, cnt_ref, *, n_new, w_buf):
    n = n_new
    scale = HEAD_DIM ** -0.5
    lg = lg_ref[...]
    qr = qr_ref[...]
    kr = kr_ref[...]
    vr = vr_ref[...]
    ri = lax.broadcasted_iota(I32, (n, n), 0)
    rj = lax.broadcasted_iota(I32, (n, n), 1)
    rel = (ri - rj).astype(F32)
    rowf = lax.broadcasted_iota(I32, (n, HEAD_DIM), 0).astype(F32)
    outs = []
    for h in range(N_HEADS):
        hs = slice(h * HEAD_DIM, (h + 1) * HEAD_DIM)
        lg_h = lg[:, h * HEAD_DIM:h * HEAD_DIM + 1]
        qh, kh, vh = qr[:, hs], kr[:, hs], vr[:, hs]
        dm = jnp.where(rel >= 0.0, jnp.exp(lg_h * jnp.maximum(rel, 0.0)), 0.0) * scale
        s = _dot_nt(qh, kh) * dm
        s_prev = st_ref[0, h]
        o = _dot(s, vh) + _dot(qh, s_prev) * jnp.exp(lg_h * (rowf + 1.0))
        kw = kh * (jnp.exp(lg_h * (n - 1.0 - rowf)) * scale)
        sto_ref[0, h] = jnp.exp(lg_h * float(n)) * s_prev + _dot_tn(kw, vh)
        outs.append(o)
    o_r = jnp.concatenate(outs, axis=1)
    normed = []
    for p in range(N_PAIRS):
        ps = slice(p * LANES, (p + 1) * LANES)
        is_lo = _pair_masks((n, LANES))
        normed.append(_group_norm_gate(o_r[:, ps], gr_ref[:, ps], gout_ref[:, ps], is_lo))
    r_ref[...] = jnp.concatenate(normed, axis=1)

    n_rows = N_HEADS * n
    n_keys = w_buf + LANES

    @pl.when(pl.program_id(0) == 0)
    def _():
        rr = lax.broadcasted_iota(I32, (n_rows, n_keys), 0)
        cc = lax.broadcasted_iota(I32, (n_rows, n_keys), 1)
        qi = rr % n
        dist = w_buf + qi - cc
        valid = (dist >= 0) & (cc < w_buf + n)
        cnt = jnp.zeros((n_rows, n_keys), F32)
        for d in DILATIONS:
            hit = valid & (dist % d == 0) & (dist <= WIN_STEPS * d)
            cnt = cnt + jnp.where(hit, 1.0, 0.0)
        cnt_ref[...] = cnt
        slope = slope_ref[...]
        srow = jnp.zeros((n_rows, 1), F32)
        r1 = lax.broadcasted_iota(I32, (n_rows, 1), 0)
        for h in range(N_HEADS):
            srow = jnp.where(r1 // n == h, slope[:, h * HEAD_DIM:h * HEAD_DIM + 1], srow)
        bias_ref[...] = jnp.where(cnt > 0.0, -srow * dist.astype(F32), NEG_INF)

    qa = qa_ref[...]
    q_rows = jnp.concatenate([qa] * N_HEADS, axis=0)
    r2 = lax.broadcasted_iota(I32, (n_rows, WIDTH), 0)
    c2 = lax.broadcasted_iota(I32, (n_rows, WIDTH), 1)
    q_blk = jnp.where(r2 // n == c2 // HEAD_DIM, q_rows, 0.0).astype(BF16)
    pad = jnp.zeros((LANES - n, WIDTH), F32)
    k_new = jnp.concatenate([ka_ref[...], pad], axis=0).astype(BF16)
    v_new = jnp.concatenate([va_ref[...], pad], axis=0).astype(BF16)
    k_all = jnp.concatenate([ck_ref[0].astype(BF16), k_new], axis=0)
    v_all = jnp.concatenate([cv_ref[0].astype(BF16), v_new], axis=0)
    s = _dot_nt(q_blk, k_all) * scale + bias_ref[...]
    m = jnp.max(s, axis=-1, keepdims=True)
    p = (cnt_ref[...] * jnp.exp(s - m)).astype(BF16)
    v_ext = jnp.concatenate([v_all, jnp.ones((n_keys, LANES), BF16)], axis=1)
    o_all = _dot(p, v_ext)
    heads = []
    for h in range(N_HEADS):
        rs = slice(h * n, (h + 1) * n)
        heads.append(o_all[rs, h * HEAD_DIM:(h + 1) * HEAD_DIM] / o_all[rs, WIDTH:WIDTH + HEAD_DIM])
    a_ref[...] = jnp.concatenate(heads, axis=1)


def _sample_mixer(lg_lane, gout, slope_lane, qr, kr, vr, gr, qa, ka, va, state, ck, cv, n_new):
    b = state.shape[0]
    w_buf = ck.shape[1]
    tok = pl.BlockSpec((n_new, WIDTH), lambda i: (i, 0))
    tab = pl.BlockSpec((1, WIDTH), lambda i: (0, 0))
    st = pl.BlockSpec((1, N_HEADS, HEAD_DIM, HEAD_DIM), lambda i: (i, 0, 0, 0))
    cache = pl.BlockSpec((1, w_buf, WIDTH), lambda i: (i, 0, 0))
    n_rows = N_HEADS * n_new
    return pl.pallas_call(
        functools.partial(_sample_mixer_kernel, n_new=n_new, w_buf=w_buf),
        grid=(b,),
        in_specs=[tab, tab, tab, tok, tok, tok, tok, tok, tok, tok, st, cache, cache],
        out_specs=[tok, tok, st],
        out_shape=[jax.ShapeDtypeStruct((b * n_new, WIDTH), F32),
                   jax.ShapeDtypeStruct((b * n_new, WIDTH), F32),
                   jax.ShapeDtypeStruct(state.shape, F32)],
        scratch_shapes=[pltpu.VMEM((n_rows, w_buf + LANES), F32),
                        pltpu.VMEM((n_rows, w_buf + LANES), F32)],
        compiler_params=_cparams(1),
        name="sample_mixer",
    )(lg_lane, gout, slope_lane, qr, kr, vr, gr, qa, ka, va, state, ck, cv)


def _mem_kv_kernel(x_ref, g_ref, wk_ref, wv_ref, gk_ref, k_ref, v_ref):
    h = _rms(x_ref[...], g_ref[...]).astype(BF16)
    k = _dot(h, wk_ref[...])
    gk = gk_ref[...]
    for hd in range(N_MEM_HEADS):
        hs = slice(hd * MEM_HEAD_DIM, (hd + 1) * MEM_HEAD_DIM)
        k_ref[:, hs] = _rms(k[:, hs], gk)
    v_ref[...] = _dot(h, wv_ref[...])


def _mem_kv(mem2d, g_mem, w_mk_bf, w_mv_bf, g_k_mem, tm=256):
    t = mem2d.shape[0]
    const = lambda i: (0, 0)
    row = lambda i: (i, 0)
    out = jax.ShapeDtypeStruct((t, D_MODEL), F32)
    return pl.pallas_call(
        _mem_kv_kernel,
        grid=(t // tm,),
        in_specs=[pl.BlockSpec((tm, D_MODEL), row), pl.BlockSpec((1, D_MODEL), const),
                  pl.BlockSpec((D_MODEL, D_MODEL), const), pl.BlockSpec((D_MODEL, D_MODEL), const),
                  pl.BlockSpec((1, MEM_HEAD_DIM), const)],
        out_specs=[pl.BlockSpec((tm, D_MODEL), row)] * 2,
        out_shape=[out, out],
        compiler_params=_cparams(1),
        name="mem_kv",
    )(mem2d, g_mem, w_mk_bf, w_mv_bf, g_k_mem)


def _cross_router_kernel(x_ref, r_ref, a_ref, wo_ref, gc_ref, wmq_ref, gqm_ref, mk_ref, mv_ref,
                         wmo_ref, gmoe_ref, wr_ref, br_ref, cin_ref,
                         y_ref, hp_ref, route_ref, cnt_ref, carry_ref, *, tq):
    first = (pl.program_id(0) == 0) & (pl.program_id(1) == 0)

    @pl.when(first)
    def _():
        carry_ref[...] = cin_ref[...]

    ra = jnp.concatenate([r_ref[...], a_ref[...]], axis=1).astype(BF16)
    y1 = x_ref[...] + _dot(ra, wo_ref[...])

    q = _dot(_rms(y1, gc_ref[...]).astype(BF16), wmq_ref[...])
    gqm = gqm_ref[...]
    heads = []
    for hd in range(N_MEM_HEADS):
        hs = slice(hd * MEM_HEAD_DIM, (hd + 1) * MEM_HEAD_DIM)
        qh = _rms(q[:, hs], gqm).astype(BF16)
        s = _dot_nt(qh, mk_ref[0, :, hs].astype(BF16)) * (MEM_HEAD_DIM ** -0.5)
        m = jnp.max(s, axis=-1, keepdims=True)
        p = jnp.exp(s - m)
        p = p / jnp.sum(p, axis=-1, keepdims=True)
        heads.append(_dot(p.astype(BF16), mv_ref[0, :, hs].astype(BF16)))
    o = jnp.concatenate(heads, axis=1).astype(BF16)
    y2 = y1 + _dot(o, wmo_ref[...])
    y_ref[...] = y2

    hm = _rms(y2, gmoe_ref[...])
    half = D_MODEL // 2
    hp_ref[...] = pltpu.pack_elementwise([hm[:, :half], hm[:, half:]], packed_dtype=BF16)

    logits = jnp.dot(hm, wr_ref[...], preferred_element_type=F32,
                     precision=lax.Precision.HIGHEST) + br_ref[...]
    lane = lax.broadcasted_iota(I32, (tq, LANES), 1)
    lanef = lane.astype(F32)
    big = float(LANES)
    is_grp = lane < N_GROUPS
    gl = jnp.where(is_grp, logits, NEG_INF)
    gmax = jnp.max(gl, axis=-1, keepdims=True)
    gidx = jnp.min(jnp.where(gl == gmax, lanef, big), axis=-1, keepdims=True)
    p_top = 1.0 / jnp.sum(jnp.where(is_grp, jnp.exp(logits - gmax), 0.0), axis=-1, keepdims=True)
    lo = EXPERT_LANE0 + N_EXP_PER_GROUP * gidx
    in_grp = (lanef >= lo) & (lanef < lo + N_EXP_PER_GROUP)
    el = jnp.where(in_grp, logits, NEG_INF)
    v1 = jnp.max(el, axis=-1, keepdims=True)
    i1 = jnp.min(jnp.where(el == v1, lanef, big), axis=-1, keepdims=True)
    el2 = jnp.where(lanef == i1, NEG_INF, el)
    v2 = jnp.max(el2, axis=-1, keepdims=True)
    i2 = jnp.min(jnp.where(el2 == v2, lanef, big), axis=-1, keepdims=True)
    e21 = jnp.exp(v2 - v1)
    w1 = p_top / (1.0 + e21)
    w2 = p_top * e21 / (1.0 + e21)

    hit1 = lanef == i1
    hit2 = lanef == i2
    onehot = jnp.where(hit1 | hit2, 1.0, 0.0)
    ti = lax.broadcasted_iota(I32, (tq, tq), 0)
    tj = lax.broadcasted_iota(I32, (tq, tq), 1)
    lower = jnp.where(tj < ti, 1.0, 0.0)
    before = _dot(lower, onehot) + carry_ref[...]
    rank1 = jnp.sum(jnp.where(hit1, before, 0.0), axis=-1, keepdims=True)
    rank2 = jnp.sum(jnp.where(hit2, before, 0.0), axis=-1, keepdims=True)
    carry_ref[...] = carry_ref[...] + jnp.sum(onehot, axis=0, keepdims=True)
    cnt_ref[...] = carry_ref[...]

    route = jnp.zeros((tq, LANES), F32)
    for idx, val in enumerate((i1 - EXPERT_LANE0, i2 - EXPERT_LANE0, w1, w2, rank1, rank2)):
        route = jnp.where(lane == idx, val, route)
    route_ref[...] = route[:, :8]


def _cross_router(x2d, r, a, w_o, g_cross, w_mq, g_q_mem, mk, mv, w_mo, g_moe, w_r, b_r, cnt_in,
                  n_batch, tq):
    t = x2d.shape[0]
    per_b = t // n_batch // tq
    row = lambda i, j: (i * per_b + j, 0)
    const = lambda i, j: (0, 0)
    mem = pl.BlockSpec((1, N_MEM, D_MODEL), lambda i, j: (i, 0, 0))
    wspec = pl.BlockSpec((D_MODEL, D_MODEL), const)
    vec = pl.BlockSpec((1, D_MODEL), const)
    return pl.pallas_call(
        functools.partial(_cross_router_kernel, tq=tq),
        grid=(n_batch, per_b),
        in_specs=[pl.BlockSpec((tq, D_MODEL), row), pl.BlockSpec((tq, WIDTH), row),
                  pl.BlockSpec((tq, WIDTH), row), wspec, vec, wspec,
                  pl.BlockSpec((1, MEM_HEAD_DIM), const), mem, mem, wspec, vec,
                  pl.BlockSpec((D_MODEL, LANES), const), pl.BlockSpec((1, LANES), const),
                  pl.BlockSpec((1, LANES), const)],
        out_specs=[pl.BlockSpec((tq, D_MODEL), row), pl.BlockSpec((tq, D_MODEL // 2), row),
                   pl.BlockSpec((tq, 8), row), pl.BlockSpec((1, LANES), const)],
        out_shape=[jax.ShapeDtypeStruct((t, D_MODEL), F32),
                   jax.ShapeDtypeStruct((t, D_MODEL // 2), jnp.uint32),
                   jax.ShapeDtypeStruct((t, 8), F32),
                   jax.ShapeDtypeStruct((1, LANES), F32)],
        scratch_shapes=[pltpu.VMEM((1, LANES), F32)],
        compiler_params=_cparams(2),
        name="cross_router",
    )(x2d, r, a, w_o, g_cross, w_mq, g_q_mem, mk, mv, w_mo, g_moe, w_r, b_r, cnt_in)


def _row_copy(src, src_row, dst, dst_row, sem):
    return pltpu.make_async_copy(src.at[pl.ds(src_row, 1)], dst.at[pl.ds(dst_row, 1)], sem)


def _dispatch_kernel(slots_ref, hp_ref, xs_in_ref, xs_ref, sem, *, tq):
    del xs_in_ref
    base = pl.program_id(0) * tq

    def start(j, carry):
        for kk in range(2):
            _row_copy(hp_ref, base + j, xs_ref, slots_ref[0, 0, 2 * j + kk], sem).start()
        return carry

    def wait(j, carry):
        for kk in range(2):
            _row_copy(hp_ref, base + j, xs_ref, slots_ref[0, 0, 2 * j + kk], sem).wait()
        return carry

    lax.fori_loop(0, tq, start, 0)
    lax.fori_loop(0, tq, wait, 0)


def _dispatch(slots, hp, xs, tq):
    t = hp.shape[0]
    return pl.pallas_call(
        functools.partial(_dispatch_kernel, tq=tq),
        grid=(t // tq,),
        in_specs=[pl.BlockSpec((1, 1, 2 * tq), lambda i: (i, 0, 0), memory_space=pltpu.SMEM),
                  pl.BlockSpec(memory_space=pl.ANY), pl.BlockSpec(memory_space=pl.ANY)],
        out_specs=pl.BlockSpec(memory_space=pl.ANY),
        out_shape=jax.ShapeDtypeStruct(xs.shape, xs.dtype),
        scratch_shapes=[pltpu.SemaphoreType.DMA(())],
        input_output_aliases={2: 0},
        compiler_params=pltpu.CompilerParams(dimension_semantics=("arbitrary",),
                                             has_side_effects=True),
        name="moe_dispatch",
    )(slots.reshape(t // tq, 1, 2 * tq), hp, xs)


def _unpack_pair(words):
    lo = pltpu.unpack_elementwise(words, index=0, packed_dtype=BF16, unpacked_dtype=F32)
    hi = pltpu.unpack_elementwise(words, index=1, packed_dtype=BF16, unpacked_dtype=F32)
    return lo, hi


def _expert_kernel(te_ref, nt_ref, xs_ref, wg_ref, wu_ref, wd_ref, ys_ref):
    del te_ref
    half = D_MODEL // 2

    @pl.when(pl.program_id(0) < nt_ref[0])
    def _():
        lo, hi = _unpack_pair(xs_ref[...])
        lo = lo.astype(BF16)
        hi = hi.astype(BF16)
        wg = wg_ref[0].astype(BF16)
        wu = wu_ref[0].astype(BF16)
        g = _dot(lo, wg[:half]) + _dot(hi, wg[half:])
        u = _dot(lo, wu[:half]) + _dot(hi, wu[half:])
        hid = (g * (1.0 / (1.0 + jnp.exp(-g))) * u).astype(BF16)
        y = _dot(hid, wd_ref[0].astype(BF16))
        ys_ref[...] = pltpu.pack_elementwise([y[:, :half], y[:, half:]], packed_dtype=BF16)

    @pl.when(pl.program_id(0) >= nt_ref[0])
    def _():
        ys_ref[...] = jnp.zeros_like(ys_ref)


def _experts(tile_expert, n_tiles, xs, w_gate, w_up, w_down):
    rows = xs.shape[0]
    n_max = rows // MOE_ROW_TILE

    def xmap(i, te, nt):
        return (jnp.minimum(i, nt[0] - 1), 0)

    def wmap(i, te, nt):
        return (te[i], 0, 0)

    return pl.pallas_call(
        _expert_kernel,
        grid_spec=pltpu.PrefetchScalarGridSpec(
            num_scalar_prefetch=2,
            grid=(n_max,),
            in_specs=[pl.BlockSpec((MOE_ROW_TILE, D_MODEL // 2), xmap),
                      pl.BlockSpec((1, D_MODEL, D_EXPERT), wmap),
                      pl.BlockSpec((1, D_MODEL, D_EXPERT), wmap),
                      pl.BlockSpec((1, D_EXPERT, D_MODEL), wmap)],
            out_specs=pl.BlockSpec((MOE_ROW_TILE, D_MODEL // 2), lambda i, te, nt: (i, 0)),
        ),
        out_shape=jax.ShapeDtypeStruct(xs.shape, xs.dtype),
        compiler_params=_cparams(1),
        name="moe_experts",
    )(tile_expert, n_tiles, xs, w_gate, w_up, w_down)


def _combine_kernel(slots_ref, ys_ref, y_ref, route_ref, o_ref, buf_ref, sem, *, tq):
    def start(j, carry):
        for kk in range(2):
            pltpu.make_async_copy(ys_ref.at[pl.ds(slots_ref[0, 0, 2 * j + kk], 1)],
                                  buf_ref.at[kk, pl.ds(j, 1)], sem).start()
        return carry

    def wait(j, carry):
        for kk in range(2):
            pltpu.make_async_copy(ys_ref.at[pl.ds(slots_ref[0, 0, 2 * j + kk], 1)],
                                  buf_ref.at[kk, pl.ds(j, 1)], sem).wait()
        return carry

    lax.fori_loop(0, tq, start, 0)
    lax.fori_loop(0, tq, wait, 0)
    half = D_MODEL // 2
    route = route_ref[...]
    w1 = route[:, 2:3]
    w2 = route[:, 3:4]
    lo1, hi1 = _unpack_pair(buf_ref[0])
    lo2, hi2 = _unpack_pair(buf_ref[1])
    y = y_ref[...]
    o_ref[:, :half] = y[:, :half] + (w1 * lo1 + w2 * lo2)
    o_ref[:, half:] = y[:, half:] + (w1 * hi1 + w2 * hi2)


def _combine(slots, ys, y2, route, tq):
    t = y2.shape[0]
    row = lambda i: (i, 0)
    return pl.pallas_call(
        functools.partial(_combine_kernel, tq=tq),
        grid=(t // tq,),
        in_specs=[pl.BlockSpec((1, 1, 2 * tq), lambda i: (i, 0, 0), memory_space=pltpu.SMEM),
                  pl.BlockSpec(memory_space=pl.ANY),
                  pl.BlockSpec((tq, D_MODEL), row), pl.BlockSpec((tq, 8), row)],
        out_specs=pl.BlockSpec((tq, D_MODEL), row),
        out_shape=jax.ShapeDtypeStruct((t, D_MODEL), F32),
        scratch_shapes=[pltpu.VMEM((2, tq, D_MODEL // 2), jnp.uint32),
                        pltpu.SemaphoreType.DMA(())],
        compiler_params=_cparams(1),
        name="moe_combine",
    )(slots.reshape(t // tq, 1, 2 * tq), ys, y2, route)


def _lane_table(per_head):
    lanes = jnp.repeat(per_head.astype(F32), HEAD_DIM)
    return lanes.reshape(N_PAIRS, 1, LANES), lanes.reshape(1, WIDTH)


def kernel(x_prompt, x_sample, mem_prompt, state_ret, cache_win_k, cache_win_v, cache_mem_k,
           cache_mem_v, g_mix, w_in, g_ret_out, g_q_att, g_k_att, w_o, g_cross, g_mem, w_mq, w_mk,
           w_mv, g_q_mem, g_k_mem, w_mo, g_moe, w_router_group, b_router_group, w_router_expert,
           b_router_expert, w_exp_gate, w_exp_up, w_exp_down):
    depth = g_mix.shape[0]
    assert depth == 1
    b, s, _ = x_prompt.shape
    bs, ns, _ = x_sample.shape
    t_p, t_s = b * s, bs * ns
    l = 0

    heads = jnp.arange(N_HEADS, dtype=F32)
    lg_pairs, lg_lane = _lane_table(jnp.log(1.0 - 2.0 ** (-5.0 - heads)))
    slope_pairs, slope_lane = _lane_table(2.0 ** (-8.0 * (heads + 1.0) / N_HEADS))
    gout_lane = g_ret_out[l].reshape(1, WIDTH)
    gout_pairs = g_ret_out[l].reshape(N_PAIRS, 1, LANES)
    head_of = jnp.arange(WIDTH) // HEAD_DIM
    bd = (head_of[:, None] == head_of[None, :]).astype(BF16)
    gq_t = jnp.tile(g_q_att[l], N_HEADS).reshape(1, WIDTH)
    gk_t = jnp.tile(g_k_att[l], N_HEADS).reshape(1, WIDTH)
    vec = lambda v: v.reshape(1, -1)
    w_in_bf = w_in[l].astype(BF16)
    w_o_bf, w_mq_bf, w_mk_bf = w_o[l].astype(BF16), w_mq[l].astype(BF16), w_mk[l].astype(BF16)
    w_mv_bf, w_mo_bf = w_mv[l].astype(BF16), w_mo[l].astype(BF16)
    w_r = jnp.zeros((D_MODEL, LANES), F32)
    w_r = w_r.at[:, :N_GROUPS].set(w_router_group[l])
    w_r = w_r.at[:, EXPERT_LANE0:EXPERT_LANE0 + N_EXPERTS].set(
        jnp.moveaxis(w_router_expert[l], 0, 1).reshape(D_MODEL, N_EXPERTS))
    b_r = jnp.zeros((1, LANES), F32)
    b_r = b_r.at[0, :N_GROUPS].set(b_router_group[l])
    b_r = b_r.at[0, EXPERT_LANE0:EXPERT_LANE0 + N_EXPERTS].set(b_router_expert[l].reshape(-1))

    xp = x_prompt.reshape(t_p, D_MODEL)
    qr, kr, vr, gr, qa, ka, va = _mix_proj(xp, vec(g_mix[l]), w_in_bf, gq_t, gk_t, bd, tm=512,
                                           act_dtype=BF16)
    shp = lambda z: z.reshape(b, s, WIDTH)
    r_p, state_p = _retention(shp(qr), shp(kr), shp(vr), shp(gr), lg_pairs, gout_pairs)
    a_p = _dilated_attention(shp(qa), shp(ka), shp(va), slope_pairs)
    mk_p, mv_p = _mem_kv(mem_prompt.reshape(b * N_MEM, D_MODEL), vec(g_mem[l]), w_mk_bf, w_mv_bf,
                          vec(g_k_mem[l]))
    zero_cnt = jnp.zeros((1, LANES), F32)
    y2_p, hp_p, route_p, cnt_p = _cross_router(
        xp, r_p.reshape(t_p, WIDTH), a_p.reshape(t_p, WIDTH), w_o_bf, vec(g_cross[l]), w_mq_bf,
        vec(g_q_mem[l]), mk_p.reshape(b, N_MEM, D_MODEL), mv_p.reshape(b, N_MEM, D_MODEL), w_mo_bf,
        vec(g_moe[l]), w_r, b_r, zero_cnt, n_batch=b, tq=512)

    xs_ = x_sample.reshape(t_s, D_MODEL)
    qr, kr, vr, gr, qa, ka_s, va_s = _mix_proj(xs_, vec(g_mix[l]), w_in_bf, gq_t, gk_t, bd, tm=t_s,
                                               act_dtype=F32)
    r_s, a_s, state_s = _sample_mixer(lg_lane, gout_lane, slope_lane, qr, kr, vr, gr, qa, ka_s, va_s,
                                      state_ret[l], cache_win_k[l].reshape(bs, -1, WIDTH),
                                      cache_win_v[l].reshape(bs, -1, WIDTH), n_new=ns)
    y2_s, hp_s, route_s, cnt_all = _cross_router(
        xs_, r_s, a_s, w_o_bf, vec(g_cross[l]), w_mq_bf, vec(g_q_mem[l]),
        cache_mem_k[l].reshape(bs, N_MEM, D_MODEL), cache_mem_v[l].reshape(bs, N_MEM, D_MODEL),
        w_mo_bf, vec(g_moe[l]), w_r, b_r, cnt_p, n_batch=bs, tq=ns)

    tile = MOE_ROW_TILE
    counts = cnt_all[0, EXPERT_LANE0:EXPERT_LANE0 + N_EXPERTS].astype(I32)
    tiles_per = (counts + tile - 1) // tile
    tile_end = jnp.cumsum(tiles_per)
    offsets = (tile_end - tiles_per) * tile
    n_tiles = tile_end[-1:]
    n_max = (2 * (t_p + t_s)) // tile + N_EXPERTS
    tile_ids = jnp.minimum(jnp.arange(n_max, dtype=I32), n_tiles[0] - 1)
    tile_expert = jnp.searchsorted(tile_end, tile_ids, side="right").astype(I32)

    def slots_of(route):
        e = route[:, 0:2].astype(I32)
        return offsets[e] + route[:, 4:6].astype(I32)

    slots_p, slots_s = slots_of(route_p), slots_of(route_s)
    xs_sorted = jnp.zeros((n_max * tile, D_MODEL // 2), jnp.uint32)
    xs_sorted = _dispatch(slots_p, hp_p, xs_sorted, tq=512)
    xs_sorted = _dispatch(slots_s, hp_s, xs_sorted, tq=t_s)
    ys_sorted = _experts(tile_expert, n_tiles, xs_sorted, w_exp_gate[l], w_exp_up[l], w_exp_down[l])
    y_p = _combine(slots_p, ys_sorted, y2_p, route_p, tq=256)
    y_s = _combine(slots_s, ys_sorted, y2_s, route_s, tq=t_s)

    return (y_p.reshape(b, s, D_MODEL), y_s.reshape(bs, ns, D_MODEL),
            state_p[None],
            ka.reshape(1, b, s, N_HEADS, HEAD_DIM), va.reshape(1, b, s, N_HEADS, HEAD_DIM),
            mk_p.reshape(1, b, N_MEM, N_MEM_HEADS, MEM_HEAD_DIM),
            mv_p.reshape(1, b, N_MEM, N_MEM_HEADS, MEM_HEAD_DIM),
            state_s[None],
            ka_s.reshape(1, bs, ns, N_HEADS, HEAD_DIM), va_s.reshape(1, bs, ns, N_HEADS, HEAD_DIM))
```

```python
import functools

import jax
import jax.numpy as jnp
from jax import lax
from jax.experimental import pallas as pl
from jax.experimental.pallas import tpu as pltpu

F32 = jnp.float32
BF16 = jnp.bfloat16
I32 = jnp.int32

D_MODEL = 1024
HEAD_DIM = 64
N_HEADS = 8
WIDTH = N_HEADS * HEAD_DIM
N_PAIRS = N_HEADS // 2
IN_COLS = 7 * WIDTH
RET_CHUNK = 128
RET_UNROLL = 4
ATT_UNROLL = 4
ATT_BLOCK = 128
WIN_STEPS = 128
DILATIONS = (1, 4, 16)
N_MEM = 256
N_MEM_HEADS = 4
MEM_HEAD_DIM = 256
N_GROUPS = 4
N_EXP_PER_GROUP = 8
N_EXPERTS = 32
D_EXPERT = 256
EPS = 1e-6
LANES = 128
EXPERT_LANE0 = 32
MOE_ROW_TILE = 256
VMEM_LIMIT = 56 * 1024 * 1024

NEG_INF = float("-inf")


def _cparams(n_axes, vmem=VMEM_LIMIT):
    return pltpu.CompilerParams(dimension_semantics=("arbitrary",) * n_axes,
                                vmem_limit_bytes=vmem)


def _dot(a, b):
    return jnp.dot(a, b, preferred_element_type=F32)


def _dot_nt(a, b):
    return lax.dot_general(a, b, (((1,), (1,)), ((), ())), preferred_element_type=F32)


def _dot_tn(a, b):
    return lax.dot_general(a, b, (((0,), (0,)), ((), ())), preferred_element_type=F32)


def _rms(x, g):
    ms = jnp.mean(x * x, axis=-1, keepdims=True)
    return x * lax.rsqrt(ms + EPS) * g


def _mix_proj_kernel(x_ref, g_ref, w_ref, gq_ref, gk_ref, bd_ref,
                     qr_ref, kr_ref, vr_ref, gr_ref, qa_ref, ka_ref, va_ref, *kv_t_refs):
    act = qr_ref.dtype
    h = _rms(x_ref[...], g_ref[...]).astype(BF16)

    def proj(j):
        return _dot(h, w_ref[:, j * WIDTH:(j + 1) * WIDTH])

    def head_norm(z, g):
        sq = z * z
        hi = sq.astype(BF16)
        lo = (sq - hi.astype(F32)).astype(BF16)
        ss = _dot(hi, bd_ref[...]) + _dot(lo, bd_ref[...])
        return z * lax.rsqrt(ss * (1.0 / HEAD_DIM) + EPS) * g

    qr_ref[...] = proj(0).astype(act)
    kr_ref[...] = proj(1).astype(act)
    vr_ref[...] = proj(2).astype(act)
    gr_ref[...] = proj(3).astype(act)
    qa_ref[...] = head_norm(proj(4), gq_ref[...]).astype(act)
    ka = head_norm(proj(5), gk_ref[...])
    va = proj(6)
    ka_ref[...] = ka.astype(ka_ref.dtype)
    va_ref[...] = va.astype(va_ref.dtype)
    if kv_t_refs:
        kat_ref, vat_ref = kv_t_refs
        kat_ref[0] = ka.T
        vat_ref[0] = va.T


def _mix_proj(x2d, g_mix, w_in_bf, gq_t, gk_t, bd, tm, act_dtype, seq=None):
    t = x2d.shape[0]
    const = lambda i: (0, 0)
    row = lambda i: (i, 0)
    out_act = jax.ShapeDtypeStruct((t, WIDTH), act_dtype)
    kv_dtype = F32 if seq is None else act_dtype
    out_kv = jax.ShapeDtypeStruct((t, WIDTH), kv_dtype)
    out_specs = [pl.BlockSpec((tm, WIDTH), row)] * 7
    out_shape = [out_act] * 5 + [out_kv] * 2
    if seq is not None:
        per_b = seq // tm
        t_spec = pl.BlockSpec((1, WIDTH, tm), lambda i: (i // per_b, 0, i % per_b))
        out_specs += [t_spec, t_spec]
        out_shape += [jax.ShapeDtypeStruct((t // seq, WIDTH, seq), F32)] * 2
    return pl.pallas_call(
        _mix_proj_kernel,
        grid=(t // tm,),
        in_specs=[
            pl.BlockSpec((tm, D_MODEL), row),
            pl.BlockSpec((1, D_MODEL), const),
            pl.BlockSpec((D_MODEL, IN_COLS), const),
            pl.BlockSpec((1, WIDTH), const),
            pl.BlockSpec((1, WIDTH), const),
            pl.BlockSpec((WIDTH, WIDTH), const),
        ],
        out_specs=out_specs,
        out_shape=out_shape,
        compiler_params=_cparams(1),
        name="mix_proj",
    )(x2d, g_mix, w_in_bf, gq_t, gk_t, bd)


def _pair_masks(shape):
    lane = lax.broadcasted_iota(I32, shape, len(shape) - 1)
    return lane < HEAD_DIM


def _segment_mean(x, is_lo):
    zero = jnp.zeros_like(x)
    lo = jnp.sum(jnp.where(is_lo, x, zero), axis=-1, keepdims=True)
    hi = jnp.sum(jnp.where(is_lo, zero, x), axis=-1, keepdims=True)
    return jnp.where(is_lo, lo, hi) * (1.0 / HEAD_DIM)


def _group_norm_gate(o, g_r, g_out, is_lo):
    c = o - _segment_mean(o, is_lo)
    y = c * lax.rsqrt(_segment_mean(c * c, is_lo) + EPS) * g_out
    g = g_r.astype(F32)
    return y * (g * (1.0 / (1.0 + jnp.exp(-g))))


def _retention_kernel(lg_ref, gout_ref, q_ref, k_ref, v_ref, g_ref, o_ref, st_ref, kv_ref, *,
                      n_chunks):
    c_len = RET_CHUNK
    shape = (c_len, LANES)
    is_lo = _pair_masks(shape)
    row = lax.broadcasted_iota(I32, shape, 0)
    col = lax.broadcasted_iota(I32, shape, 1)
    rel = (row - col).astype(F32)
    lg_lane = lg_ref[0]
    lg0 = lg_lane[:, 0:1]
    lg1 = lg_lane[:, HEAD_DIM:HEAD_DIM + 1]
    scale = HEAD_DIM ** -0.5
    causal = rel >= 0.0
    relp = jnp.maximum(rel, 0.0)
    d0 = jnp.where(causal, jnp.exp(lg0 * relp), 0.0) * scale
    d1 = jnp.where(causal, jnp.exp(lg1 * relp), 0.0) * scale
    rowf = row.astype(F32)
    w_k = jnp.exp(lg_lane * (c_len - 1.0 - rowf)) * scale
    w_q = jnp.exp(lg_lane * (rowf + 1.0))
    lg_row = jnp.where(row < HEAD_DIM, lg0, lg1)
    g_chunk = jnp.exp(lg_row * float(c_len))
    same_head = (row < HEAD_DIM) == (col < HEAD_DIM)
    g_out = gout_ref[0]

    def chunk(c):
        return pl.ds(pl.multiple_of(c * c_len, c_len), c_len)

    def outer(i, carry):
        for u in range(RET_UNROLL):
            c = i * RET_UNROLL + u
            kw = (k_ref[0, chunk(c), :].astype(F32) * w_k).astype(BF16)
            kv_ref[c] = jnp.where(same_head, _dot_tn(kw, v_ref[0, chunk(c), :]), 0.0)
        return carry

    lax.fori_loop(0, n_chunks // RET_UNROLL, outer, 0)

    def recur(c, state):
        kv = kv_ref[c]
        kv_ref[c] = state
        return g_chunk * state + kv

    state = lax.fori_loop(0, n_chunks, recur, jnp.zeros(shape, F32))
    st_ref[0, 0] = state[:HEAD_DIM, :HEAD_DIM]
    st_ref[0, 1] = state[HEAD_DIM:, HEAD_DIM:]

    def inner(i, carry):
        for u in range(RET_UNROLL):
            c = i * RET_UNROLL + u
            sl = chunk(c)
            qc = q_ref[0, sl, :]
            kc = k_ref[0, sl, :]
            vc = v_ref[0, sl, :]
            zero = jnp.zeros_like(qc)
            s0 = (_dot_nt(jnp.where(is_lo, qc, zero), kc) * d0).astype(BF16)
            s1 = (_dot_nt(jnp.where(is_lo, zero, qc), kc) * d1).astype(BF16)
            o_in = _dot(s0, jnp.where(is_lo, vc, zero)) + _dot(s1, jnp.where(is_lo, zero, vc))
            o_x = _dot(qc, kv_ref[c].astype(BF16)) * w_q
            o_ref[0, sl, :] = _group_norm_gate(o_in + o_x, g_ref[0, sl, :], g_out,
                                               is_lo).astype(BF16)
        return carry

    lax.fori_loop(0, n_chunks // RET_UNROLL, inner, 0)


def _retention(q, k, v, g, lg_pairs, gout_pairs):
    b, s, _ = q.shape
    blk = pl.BlockSpec((1, s, LANES), lambda i, p: (i, 0, p))
    tab = pl.BlockSpec((1, 1, LANES), lambda i, p: (p, 0, 0))
    return pl.pallas_call(
        functools.partial(_retention_kernel, n_chunks=s // RET_CHUNK),
        grid=(b, N_PAIRS),
        in_specs=[tab, tab, blk, blk, blk, blk],
        out_specs=[blk, pl.BlockSpec((1, 2, HEAD_DIM, HEAD_DIM), lambda i, p: (i, p, 0, 0))],
        out_shape=[jax.ShapeDtypeStruct((b, s, WIDTH), BF16),
                   jax.ShapeDtypeStruct((b, N_HEADS, HEAD_DIM, HEAD_DIM), F32)],
        scratch_shapes=[pltpu.VMEM((s // RET_CHUNK, LANES, LANES), F32)],
        compiler_params=_cparams(2),
        name="retention",
    )(lg_pairs, gout_pairs, q, k, v, g)


def _attn_block(qb, kb, vb, bias0, bias1, is_lo):
    qb = qb.astype(BF16)
    kb = kb.astype(BF16)
    zq = jnp.zeros_like(qb)
    s0 = _dot_nt(jnp.where(is_lo, qb, zq), kb) + bias0
    s1 = _dot_nt(jnp.where(is_lo, zq, qb), kb) + bias1
    m0 = jnp.max(s0, axis=-1, keepdims=True)
    m1 = jnp.max(s1, axis=-1, keepdims=True)
    p0 = jnp.exp(s0 - m0).astype(BF16)
    p1 = jnp.exp(s1 - m1).astype(BF16)
    is_lo_k = _pair_masks(vb.shape)
    one = jnp.ones_like(vb)
    r0 = _dot(p0, jnp.where(is_lo_k, vb, one).astype(BF16))
    r1 = _dot(p1, jnp.where(is_lo_k, one, vb).astype(BF16))
    acc = jnp.where(is_lo, r0, r1)
    l = pltpu.roll(jnp.where(is_lo, r1, r0), HEAD_DIM, 1)
    m = jnp.where(is_lo, m0, m1)
    return acc, m, l


def _dil_attn_kernel(sl_ref, q_ref, k_ref, v_ref, o_ref,
                     qf_ref, kf_ref, vf_ref, acc_ref, m_ref, l_ref, bias_ref, bias1_ref, *, seq):
    blk = ATT_BLOCK
    is_lo = _pair_masks((blk, LANES))
    slope = sl_ref[0]
    slope0 = slope[:, 0:1]
    slope1 = slope[:, HEAD_DIM:HEAD_DIM + 1]

    qf_ref[...] = q_ref[0].astype(F32) * (HEAD_DIM ** -0.5)
    kf_ref[...] = k_ref[0].astype(F32)
    vf_ref[...] = v_ref[0].astype(F32)

    i2 = lax.broadcasted_iota(I32, (blk, 2 * blk), 0)
    j2 = lax.broadcasted_iota(I32, (blk, 2 * blk), 1)
    rel2 = blk + i2 - j2
    ok2 = (rel2 >= 0) & (rel2 <= WIN_STEPS)
    i1 = lax.broadcasted_iota(I32, (blk, blk), 0)
    j1 = lax.broadcasted_iota(I32, (blk, blk), 1)
    rel1 = i1 - j1
    ok1 = rel1 >= 0
    for pi, d in enumerate(DILATIONS):
        dist2 = (rel2 * d).astype(F32)
        dist1 = (rel1 * d).astype(F32)
        bias_ref[2 * pi] = jnp.where(ok2, -slope0 * dist2, NEG_INF)
        bias_ref[2 * pi + 1] = jnp.where(ok2, -slope1 * dist2, NEG_INF)
        bias1_ref[2 * pi] = jnp.where(ok1, -slope0 * dist1, NEG_INF)
        bias1_ref[2 * pi + 1] = jnp.where(ok1, -slope1 * dist1, NEG_INF)

    def rows(start, n, d):
        return pl.ds(start, n) if d == 1 else pl.ds(start, n, stride=d)

    def store(pi, sl, res):
        acc, m, l = res
        acc_ref[pi, sl, :] = acc
        m_ref[pi, sl, :] = m
        l_ref[pi, sl, :] = l

    def run(pi, d, items):
        results = []
        for q0, k0 in items:
            sq = rows(q0, blk, d)
            if k0 is None:
                sk, b_ref = sq, bias1_ref
            else:
                sk, b_ref = rows(k0, 2 * blk, d), bias_ref
            results.append((sq, _attn_block(qf_ref[sq, :], kf_ref[sk, :], vf_ref[sk, :],
                                            b_ref[2 * pi], b_ref[2 * pi + 1], is_lo)))
        for sq, res in results:
            store(pi, sq, res)

    for pi, d in enumerate(DILATIONS):
        n_blocks = seq // d // blk
        step = blk * d
        if d <= ATT_UNROLL:
            run(pi, d, [(r, None) for r in range(d)])
            per_step = ATT_UNROLL // d

            def body(g, carry, pi=pi, d=d, step=step, per_step=per_step):
                items = []
                for u in range(per_step):
                    base = (1 + g * per_step + u) * step
                    if d == 1:
                        base = pl.multiple_of(base, blk)
                    items += [(base + r, base + r - step) for r in range(d)]
                run(pi, d, items)
                return carry

            lax.fori_loop(0, (n_blocks - 1) // per_step, body, 0)
            rest = range(1 + (n_blocks - 1) // per_step * per_step, n_blocks)
            if rest:
                run(pi, d, [(j * step + r, (j - 1) * step + r) for j in rest for r in range(d)])
        else:
            assert n_blocks == 1 and d % ATT_UNROLL == 0

            def body(g, carry, pi=pi, d=d):
                run(pi, d, [(g * ATT_UNROLL + u, None) for u in range(ATT_UNROLL)])
                return carry

            lax.fori_loop(0, d // ATT_UNROLL, body, 0)

    def merge(c, carry):
        sl = pl.ds(pl.multiple_of(c * 256, 256), 256)
        m_all = [m_ref[pi, sl, :] for pi in range(3)]
        m_max = jnp.maximum(jnp.maximum(m_all[0], m_all[1]), m_all[2])
        num = jnp.zeros((256, LANES), F32)
        den = jnp.zeros((256, LANES), F32)
        for pi in range(3):
            w = jnp.exp(m_all[pi] - m_max)
            num = num + w * acc_ref[pi, sl, :]
            den = den + w * l_ref[pi, sl, :]
        o_ref[0, sl, :] = (num / den).astype(BF16)
        return carry

    lax.fori_loop(0, seq // 256, merge, 0)


def _dilated_attention(qa, ka, va, slope_pairs):
    b, s, _ = qa.shape
    blk = pl.BlockSpec((1, s, LANES), lambda i, p: (i, 0, p))
    tab = pl.BlockSpec((1, 1, LANES), lambda i, p: (p, 0, 0))
    return pl.pallas_call(
        functools.partial(_dil_attn_kernel, seq=s),
        grid=(b, N_PAIRS),
        in_specs=[tab, blk, blk, blk],
        out_specs=blk,
        out_shape=jax.ShapeDtypeStruct((b, s, WIDTH), BF16),
        scratch_shapes=[
            pltpu.VMEM((s, LANES), F32),
            pltpu.VMEM((s, LANES), F32),
            pltpu.VMEM((s, LANES), F32),
            pltpu.VMEM((3, s, LANES), F32),
            pltpu.VMEM((3, s, LANES), F32),
            pltpu.VMEM((3, s, LANES), F32),
            pltpu.VMEM((6, ATT_BLOCK, 2 * ATT_BLOCK), F32),
            pltpu.VMEM((6, ATT_BLOCK, ATT_BLOCK), F32),
        ],
        compiler_params=_cparams(2),
        name="dilated_attention",
    )(slope_pairs, qa, ka, va)


def _sample_mixer_kernel(lg_ref, gout_ref, slope_ref, qr_ref, kr_ref, vr_ref, gr_ref,
                         qa_ref, ka_ref, va_ref, st_ref, ck_ref, cv_ref,
                         r_ref, a_ref, sto_ref, bias_ref, cnt_ref, *, n_new, w_buf):
    n = n_new
    scale = HEAD_DIM ** -0.5
    lg = lg_ref[...]
    qr = qr_ref[...]
    kr = kr_ref[...]
    vr = vr_ref[...]
    ri = lax.broadcasted_iota(I32, (n, n), 0)
    rj = lax.broadcasted_iota(I32, (n, n), 1)
    rel = (ri - rj).astype(F32)
    rowf = lax.broadcasted_iota(I32, (n, HEAD_DIM), 0).astype(F32)
    outs = []
    for h in range(N_HEADS):
        hs = slice(h * HEAD_DIM, (h + 1) * HEAD_DIM)
        lg_h = lg[:, h * HEAD_DIM:h * HEAD_DIM + 1]
        qh, kh, vh = qr[:, hs], kr[:, hs], vr[:, hs]
        dm = jnp.where(rel >= 0.0, jnp.exp(lg_h * jnp.maximum(rel, 0.0)), 0.0) * scale
        s = _dot_nt(qh, kh) * dm
        s_prev = st_ref[0, h]
        o = _dot(s, vh) + _dot(qh, s_prev) * jnp.exp(lg_h * (rowf + 1.0))
        kw = kh * (jnp.exp(lg_h * (n - 1.0 - rowf)) * scale)
        sto_ref[0, h] = jnp.exp(lg_h * float(n)) * s_prev + _dot_tn(kw, vh)
        outs.append(o)
    o_r = jnp.concatenate(outs, axis=1)
    normed = []
    for p in range(N_PAIRS):
        ps = slice(p * LANES, (p + 1) * LANES)
        is_lo = _pair_masks((n, LANES))
        normed.append(_group_norm_gate(o_r[:, ps], gr_ref[:, ps], gout_ref[:, ps], is_lo))
    r_ref[...] = jnp.concatenate(normed, axis=1)

    n_rows = N_HEADS * n
    n_keys = w_buf + LANES

    @pl.when(pl.program_id(0) == 0)
    def _():
        rr = lax.broadcasted_iota(I32, (n_rows, n_keys), 0)
        cc = lax.broadcasted_iota(I32, (n_rows, n_keys), 1)
        qi = rr % n
        dist = w_buf + qi - cc
        valid = (dist >= 0) & (cc < w_buf + n)
        cnt = jnp.zeros((n_rows, n_keys), F32)
        for d in DILATIONS:
            hit = valid & (dist % d == 0) & (dist <= WIN_STEPS * d)
            cnt = cnt + jnp.where(hit, 1.0, 0.0)
        cnt_ref[...] = cnt
        slope = slope_ref[...]
        srow = jnp.zeros((n_rows, 1), F32)
        r1 = lax.broadcasted_iota(I32, (n_rows, 1), 0)
        for h in range(N_HEADS):
            srow = jnp.where(r1 // n == h, slope[:, h * HEAD_DIM:h * HEAD_DIM + 1], srow)
        bias_ref[...] = jnp.where(cnt > 0.0, -srow * dist.astype(F32), NEG_INF)

    qa = qa_ref[...]
    q_rows = jnp.concatenate([qa] * N_HEADS, axis=0)
    r2 = lax.broadcasted_iota(I32, (n_rows, WIDTH), 0)
    c2 = lax.broadcasted_iota(I32, (n_rows, WIDTH), 1)
    q_blk = jnp.where(r2 // n == c2 // HEAD_DIM, q_rows, 0.0).astype(BF16)
    pad = jnp.zeros((LANES - n, WIDTH), F32)
    k_new = jnp.concatenate([ka_ref[...], pad], axis=0).astype(BF16)
    v_new = jnp.concatenate([va_ref[...], pad], axis=0).astype(BF16)
    s_old = _dot(q_blk, ck_ref[0].astype(BF16))
    s_new = _dot_nt(q_blk, k_new)
    s = jnp.concatenate([s_old, s_new], axis=1) * scale + bias_ref[...]
    m = jnp.max(s, axis=-1, keepdims=True)
    p = cnt_ref[...] * jnp.exp(s - m)
    den = jnp.sum(p, axis=-1, keepdims=True)
    p = p.astype(BF16)
    o_all = _dot_nt(p[:, :w_buf], cv_ref[0].astype(BF16)) + _dot(p[:, w_buf:], v_new)
    o_all = o_all / den
    heads = [o_all[h * n:(h + 1) * n, h * HEAD_DIM:(h + 1) * HEAD_DIM] for h in range(N_HEADS)]
    a_ref[...] = jnp.concatenate(heads, axis=1)


def _sample_mixer(lg_lane, gout, slope_lane, qr, kr, vr, gr, qa, ka, va, state, ck, cv, n_new):
    b = state.shape[0]
    w_buf = ck.shape[2]
    tok = pl.BlockSpec((n_new, WIDTH), lambda i: (i, 0))
    tab = pl.BlockSpec((1, WIDTH), lambda i: (0, 0))
    st = pl.BlockSpec((1, N_HEADS, HEAD_DIM, HEAD_DIM), lambda i: (i, 0, 0, 0))
    cache = pl.BlockSpec((1, WIDTH, w_buf), lambda i: (i, 0, 0))
    n_rows = N_HEADS * n_new
    return pl.pallas_call(
        functools.partial(_sample_mixer_kernel, n_new=n_new, w_buf=w_buf),
        grid=(b,),
        in_specs=[tab, tab, tab, tok, tok, tok, tok, tok, tok, tok, st, cache, cache],
        out_specs=[tok, tok, st],
        out_shape=[jax.ShapeDtypeStruct((b * n_new, WIDTH), F32),
                   jax.ShapeDtypeStruct((b * n_new, WIDTH), F32),
                   jax.ShapeDtypeStruct(state.shape, F32)],
        scratch_shapes=[pltpu.VMEM((n_rows, w_buf + LANES), F32),
                        pltpu.VMEM((n_rows, w_buf + LANES), F32)],
        compiler_params=_cparams(1),
        name="sample_mixer",
    )(lg_lane, gout, slope_lane, qr, kr, vr, gr, qa, ka, va, state, ck, cv)


def _mem_kv_kernel(x_ref, g_ref, wk_ref, wv_ref, gk_ref, k_ref, v_ref):
    h = _rms(x_ref[...], g_ref[...]).astype(BF16)
    k = _dot(h, wk_ref[...])
    gk = gk_ref[...]
    for hd in range(N_MEM_HEADS):
        hs = slice(hd * MEM_HEAD_DIM, (hd + 1) * MEM_HEAD_DIM)
        k_ref[:, hs] = _rms(k[:, hs], gk)
    v_ref[...] = _dot(h, wv_ref[...])


def _mem_kv(mem2d, g_mem, w_mk_bf, w_mv_bf, g_k_mem, tm=256):
    t = mem2d.shape[0]
    const = lambda i: (0, 0)
    row = lambda i: (i, 0)
    out = jax.ShapeDtypeStruct((t, D_MODEL), F32)
    return pl.pallas_call(
        _mem_kv_kernel,
        grid=(t // tm,),
        in_specs=[pl.BlockSpec((tm, D_MODEL), row), pl.BlockSpec((1, D_MODEL), const),
                  pl.BlockSpec((D_MODEL, D_MODEL), const), pl.BlockSpec((D_MODEL, D_MODEL), const),
                  pl.BlockSpec((1, MEM_HEAD_DIM), const)],
        out_specs=[pl.BlockSpec((tm, D_MODEL), row)] * 2,
        out_shape=[out, out],
        compiler_params=_cparams(1),
        name="mem_kv",
    )(mem2d, g_mem, w_mk_bf, w_mv_bf, g_k_mem)


def _cross_router_kernel(x_ref, r_ref, a_ref, wo_ref, gc_ref, wmq_ref, gqm_ref, mk_ref, mv_ref,
                         wmo_ref, gmoe_ref, wr_ref, br_ref, cin_ref,
                         y_ref, hp_ref, route_ref, cnt_ref, carry_ref, *, tq):
    first = (pl.program_id(0) == 0) & (pl.program_id(1) == 0)

    @pl.when(first)
    def _():
        carry_ref[...] = cin_ref[...]

    ra = jnp.concatenate([r_ref[...], a_ref[...]], axis=1).astype(BF16)
    y1 = x_ref[...] + _dot(ra, wo_ref[...])

    q = _dot(_rms(y1, gc_ref[...]).astype(BF16), wmq_ref[...])
    gqm = gqm_ref[...]
    heads = []
    for hd in range(N_MEM_HEADS):
        hs = slice(hd * MEM_HEAD_DIM, (hd + 1) * MEM_HEAD_DIM)
        qh = _rms(q[:, hs], gqm).astype(BF16)
        s = _dot_nt(qh, mk_ref[0, :, hs].astype(BF16)) * (MEM_HEAD_DIM ** -0.5)
        m = jnp.max(s, axis=-1, keepdims=True)
        p = jnp.exp(s - m)
        p = p / jnp.sum(p, axis=-1, keepdims=True)
        heads.append(_dot(p.astype(BF16), mv_ref[0, :, hs].astype(BF16)))
    o = jnp.concatenate(heads, axis=1).astype(BF16)
    y2 = y1 + _dot(o, wmo_ref[...])
    y_ref[...] = y2

    hm = _rms(y2, gmoe_ref[...])
    half = D_MODEL // 2
    hp_ref[...] = pltpu.pack_elementwise([hm[:, :half], hm[:, half:]], packed_dtype=BF16)

    logits = jnp.dot(hm, wr_ref[...], preferred_element_type=F32,
                     precision=lax.Precision.HIGHEST) + br_ref[...]
    lane = lax.broadcasted_iota(I32, (tq, LANES), 1)
    lanef = lane.astype(F32)
    big = float(LANES)
    is_grp = lane < N_GROUPS
    gl = jnp.where(is_grp, logits, NEG_INF)
    gmax = jnp.max(gl, axis=-1, keepdims=True)
    gidx = jnp.min(jnp.where(gl == gmax, lanef, big), axis=-1, keepdims=True)
    p_top = 1.0 / jnp.sum(jnp.where(is_grp, jnp.exp(logits - gmax), 0.0), axis=-1, keepdims=True)
    lo = EXPERT_LANE0 + N_EXP_PER_GROUP * gidx
    in_grp = (lanef >= lo) & (lanef < lo + N_EXP_PER_GROUP)
    el = jnp.where(in_grp, logits, NEG_INF)
    v1 = jnp.max(el, axis=-1, keepdims=True)
    i1 = jnp.min(jnp.where(el == v1, lanef, big), axis=-1, keepdims=True)
    el2 = jnp.where(lanef == i1, NEG_INF, el)
    v2 = jnp.max(el2, axis=-1, keepdims=True)
    i2 = jnp.min(jnp.where(el2 == v2, lanef, big), axis=-1, keepdims=True)
    e21 = jnp.exp(v2 - v1)
    w1 = p_top / (1.0 + e21)
    w2 = p_top * e21 / (1.0 + e21)

    hit1 = lanef == i1
    hit2 = lanef == i2
    onehot = jnp.where(hit1 | hit2, 1.0, 0.0)
    ti = lax.broadcasted_iota(I32, (tq, tq), 0)
    tj = lax.broadcasted_iota(I32, (tq, tq), 1)
    lower = jnp.where(tj < ti, 1.0, 0.0)
    before = _dot(lower, onehot) + carry_ref[...]
    rank1 = jnp.sum(jnp.where(hit1, before, 0.0), axis=-1, keepdims=True)
    rank2 = jnp.sum(jnp.where(hit2, before, 0.0), axis=-1, keepdims=True)
    carry_ref[...] = carry_ref[...] + jnp.sum(onehot, axis=0, keepdims=True)
    cnt_ref[...] = carry_ref[...]

    route = jnp.zeros((tq, LANES), F32)
    for idx, val in enumerate((i1 - EXPERT_LANE0, i2 - EXPERT_LANE0, w1, w2, rank1, rank2)):
        route = jnp.where(lane == idx, val, route)
    route_ref[...] = route[:, :8]


def _cross_router(x2d, r, a, w_o, g_cross, w_mq, g_q_mem, mk, mv, w_mo, g_moe, w_r, b_r, cnt_in,
                  n_batch, tq):
    t = x2d.shape[0]
    per_b = t // n_batch // tq
    row = lambda i, j: (i * per_b + j, 0)
    const = lambda i, j: (0, 0)
    mem = pl.BlockSpec((1, N_MEM, D_MODEL), lambda i, j: (i, 0, 0))
    wspec = pl.BlockSpec((D_MODEL, D_MODEL), const)
    vec = pl.BlockSpec((1, D_MODEL), const)
    return pl.pallas_call(
        functools.partial(_cross_router_kernel, tq=tq),
        grid=(n_batch, per_b),
        in_specs=[pl.BlockSpec((tq, D_MODEL), row), pl.BlockSpec((tq, WIDTH), row),
                  pl.BlockSpec((tq, WIDTH), row), wspec, vec, wspec,
                  pl.BlockSpec((1, MEM_HEAD_DIM), const), mem, mem, wspec, vec,
                  pl.BlockSpec((D_MODEL, LANES), const), pl.BlockSpec((1, LANES), const),
                  pl.BlockSpec((1, LANES), const)],
        out_specs=[pl.BlockSpec((tq, D_MODEL), row), pl.BlockSpec((tq, D_MODEL // 2), row),
                   pl.BlockSpec((tq, 8), row), pl.BlockSpec((1, LANES), const)],
        out_shape=[jax.ShapeDtypeStruct((t, D_MODEL), F32),
                   jax.ShapeDtypeStruct((t, D_MODEL // 2), jnp.uint32),
                   jax.ShapeDtypeStruct((t, 8), F32),
                   jax.ShapeDtypeStruct((1, LANES), F32)],
        scratch_shapes=[pltpu.VMEM((1, LANES), F32)],
        compiler_params=_cparams(2),
        name="cross_router",
    )(x2d, r, a, w_o, g_cross, w_mq, g_q_mem, mk, mv, w_mo, g_moe, w_r, b_r, cnt_in)


def _row_copy(src, src_row, dst, dst_row, sem):
    return pltpu.make_async_copy(src.at[pl.ds(src_row, 1)], dst.at[pl.ds(dst_row, 1)], sem)


def _dispatch_kernel(slots_ref, hp_ref, xs_in_ref, xs_ref, sem, *, tq):
    del xs_in_ref

    def start(j, carry):
        for kk in range(2):
            _row_copy(hp_ref, j, xs_ref, slots_ref[0, 0, 2 * j + kk], sem).start()
        return carry

    def wait(j, carry):
        for kk in range(2):
            _row_copy(hp_ref, j, xs_ref, slots_ref[0, 0, 2 * j + kk], sem).wait()
        return carry

    lax.fori_loop(0, tq, start, 0, unroll=8)
    lax.fori_loop(0, tq, wait, 0, unroll=8)


def _dispatch(slots, hp, xs, tq):
    t = hp.shape[0]
    return pl.pallas_call(
        functools.partial(_dispatch_kernel, tq=tq),
        grid=(t // tq,),
        in_specs=[pl.BlockSpec((1, 1, 2 * tq), lambda i: (i, 0, 0), memory_space=pltpu.SMEM),
                  pl.BlockSpec((tq, D_MODEL // 2), lambda i: (i, 0)),
                  pl.BlockSpec(memory_space=pl.ANY)],
        out_specs=pl.BlockSpec(memory_space=pl.ANY),
        out_shape=jax.ShapeDtypeStruct(xs.shape, xs.dtype),
        scratch_shapes=[pltpu.SemaphoreType.DMA(())],
        input_output_aliases={2: 0},
        compiler_params=pltpu.CompilerParams(dimension_semantics=("arbitrary",),
                                             has_side_effects=True),
        name="moe_dispatch",
    )(slots.reshape(t // tq, 1, 2 * tq), hp, xs)


def _unpack_pair(words):
    lo = pltpu.unpack_elementwise(words, index=0, packed_dtype=BF16, unpacked_dtype=F32)
    hi = pltpu.unpack_elementwise(words, index=1, packed_dtype=BF16, unpacked_dtype=F32)
    return lo, hi


def _expert_kernel(te_ref, nt_ref, xs_ref, wg_ref, wu_ref, wd_ref, ys_ref):
    del te_ref
    half = D_MODEL // 2

    @pl.when(pl.program_id(0) < nt_ref[0])
    def _():
        lo, hi = _unpack_pair(xs_ref[...])
        lo = lo.astype(BF16)
        hi = hi.astype(BF16)
        wg = wg_ref[0].astype(BF16)
        wu = wu_ref[0].astype(BF16)
        g = _dot(lo, wg[:half]) + _dot(hi, wg[half:])
        u = _dot(lo, wu[:half]) + _dot(hi, wu[half:])
        hid = (g * (1.0 / (1.0 + jnp.exp(-g))) * u).astype(BF16)
        y = _dot(hid, wd_ref[0].astype(BF16))
        ys_ref[...] = pltpu.pack_elementwise([y[:, :half], y[:, half:]], packed_dtype=BF16)

    @pl.when(pl.program_id(0) >= nt_ref[0])
    def _():
        ys_ref[...] = jnp.zeros_like(ys_ref)


def _experts(tile_expert, n_tiles, xs, w_gate, w_up, w_down):
    rows = xs.shape[0]
    n_max = rows // MOE_ROW_TILE

    def xmap(i, te, nt):
        return (jnp.minimum(i, nt[0] - 1), 0)

    def wmap(i, te, nt):
        return (te[i], 0, 0)

    return pl.pallas_call(
        _expert_kernel,
        grid_spec=pltpu.PrefetchScalarGridSpec(
            num_scalar_prefetch=2,
            grid=(n_max,),
            in_specs=[pl.BlockSpec((MOE_ROW_TILE, D_MODEL // 2), xmap),
                      pl.BlockSpec((1, D_MODEL, D_EXPERT), wmap),
                      pl.BlockSpec((1, D_MODEL, D_EXPERT), wmap),
                      pl.BlockSpec((1, D_EXPERT, D_MODEL), wmap)],
            out_specs=pl.BlockSpec((MOE_ROW_TILE, D_MODEL // 2), lambda i, te, nt: (i, 0)),
        ),
        out_shape=jax.ShapeDtypeStruct(xs.shape, xs.dtype),
        compiler_params=_cparams(1),
        name="moe_experts",
    )(tile_expert, n_tiles, xs, w_gate, w_up, w_down)


def _combine_kernel(slots_ref, ys_ref, y_ref, route_ref, o_ref, buf_ref, sem, *, tq):
    def start(j, carry):
        for kk in range(2):
            pltpu.make_async_copy(ys_ref.at[pl.ds(slots_ref[0, 0, 2 * j + kk], 1)],
                                  buf_ref.at[kk, pl.ds(j, 1)], sem).start()
        return carry

    def wait(j, carry):
        for kk in range(2):
            pltpu.make_async_copy(ys_ref.at[pl.ds(slots_ref[0, 0, 2 * j + kk], 1)],
                                  buf_ref.at[kk, pl.ds(j, 1)], sem).wait()
        return carry

    lax.fori_loop(0, tq, start, 0, unroll=8)
    lax.fori_loop(0, tq, wait, 0, unroll=8)
    half = D_MODEL // 2
    route = route_ref[...]
    w1 = route[:, 2:3]
    w2 = route[:, 3:4]
    lo1, hi1 = _unpack_pair(buf_ref[0])
    lo2, hi2 = _unpack_pair(buf_ref[1])
    y = y_ref[...]
    o_ref[:, :half] = y[:, :half] + (w1 * lo1 + w2 * lo2)
    o_ref[:, half:] = y[:, half:] + (w1 * hi1 + w2 * hi2)


def _combine(slots, ys, y2, route, tq):
    t = y2.shape[0]
    row = lambda i: (i, 0)
    return pl.pallas_call(
        functools.partial(_combine_kernel, tq=tq),
        grid=(t // tq,),
        in_specs=[pl.BlockSpec((1, 1, 2 * tq), lambda i: (i, 0, 0), memory_space=pltpu.SMEM),
                  pl.BlockSpec(memory_space=pl.ANY),
                  pl.BlockSpec((tq, D_MODEL), row), pl.BlockSpec((tq, 8), row)],
        out_specs=pl.BlockSpec((tq, D_MODEL), row),
        out_shape=jax.ShapeDtypeStruct((t, D_MODEL), F32),
        scratch_shapes=[pltpu.VMEM((2, tq, D_MODEL // 2), jnp.uint32),
                        pltpu.SemaphoreType.DMA(())],
        compiler_params=_cparams(1),
        name="moe_combine",
    )(slots.reshape(t // tq, 1, 2 * tq), ys, y2, route)


def _lane_table(per_head):
    lanes = jnp.repeat(per_head.astype(F32), HEAD_DIM)
    return lanes.reshape(N_PAIRS, 1, LANES), lanes.reshape(1, WIDTH)


def kernel(x_prompt, x_sample, mem_prompt, state_ret, cache_win_k, cache_win_v, cache_mem_k,
           cache_mem_v, g_mix, w_in, g_ret_out, g_q_att, g_k_att, w_o, g_cross, g_mem, w_mq, w_mk,
           w_mv, g_q_mem, g_k_mem, w_mo, g_moe, w_router_group, b_router_group, w_router_expert,
           b_router_expert, w_exp_gate, w_exp_up, w_exp_down):
    depth = g_mix.shape[0]
    assert depth == 1
    b, s, _ = x_prompt.shape
    bs, ns, _ = x_sample.shape
    t_p, t_s = b * s, bs * ns
    l = 0

    heads = jnp.arange(N_HEADS, dtype=F32)
    lg_pairs, lg_lane = _lane_table(jnp.log(1.0 - 2.0 ** (-5.0 - heads)))
    slope_pairs, slope_lane = _lane_table(2.0 ** (-8.0 * (heads + 1.0) / N_HEADS))
    gout_lane = g_ret_out[l].reshape(1, WIDTH)
    gout_pairs = g_ret_out[l].reshape(N_PAIRS, 1, LANES)
    head_of = jnp.arange(WIDTH) // HEAD_DIM
    bd = (head_of[:, None] == head_of[None, :]).astype(BF16)
    gq_t = jnp.tile(g_q_att[l], N_HEADS).reshape(1, WIDTH)
    gk_t = jnp.tile(g_k_att[l], N_HEADS).reshape(1, WIDTH)
    vec = lambda v: v.reshape(1, -1)
    w_in_bf = w_in[l].astype(BF16)
    w_o_bf, w_mq_bf, w_mk_bf = w_o[l].astype(BF16), w_mq[l].astype(BF16), w_mk[l].astype(BF16)
    w_mv_bf, w_mo_bf = w_mv[l].astype(BF16), w_mo[l].astype(BF16)
    gap = EXPERT_LANE0 - N_GROUPS
    tail = LANES - EXPERT_LANE0 - N_EXPERTS
    w_r = jnp.concatenate(
        [w_router_group[l], jnp.zeros((D_MODEL, gap), F32),
         jnp.moveaxis(w_router_expert[l], 0, 1).reshape(D_MODEL, N_EXPERTS),
         jnp.zeros((D_MODEL, tail), F32)], axis=1)
    b_r = jnp.concatenate([b_router_group[l], jnp.zeros((gap,), F32),
                           b_router_expert[l].reshape(-1), jnp.zeros((tail,), F32)]).reshape(1, LANES)

    xp = x_prompt.reshape(t_p, D_MODEL)
    qr, kr, vr, gr, qa, ka, va, ka_t, va_t = _mix_proj(xp, vec(g_mix[l]), w_in_bf, gq_t, gk_t, bd,
                                                       tm=512, act_dtype=BF16, seq=s)
    shp = lambda z: z.reshape(b, s, WIDTH)
    r_p, state_p = _retention(shp(qr), shp(kr), shp(vr), shp(gr), lg_pairs, gout_pairs)
    a_p = _dilated_attention(shp(qa), shp(ka), shp(va), slope_pairs)
    mk_p, mv_p = _mem_kv(mem_prompt.reshape(b * N_MEM, D_MODEL), vec(g_mem[l]), w_mk_bf, w_mv_bf,
                          vec(g_k_mem[l]))
    zero_cnt = jnp.zeros((1, LANES), F32)
    y2_p, hp_p, route_p, cnt_p = _cross_router(
        xp, r_p.reshape(t_p, WIDTH), a_p.reshape(t_p, WIDTH), w_o_bf, vec(g_cross[l]), w_mq_bf,
        vec(g_q_mem[l]), mk_p.reshape(b, N_MEM, D_MODEL), mv_p.reshape(b, N_MEM, D_MODEL), w_mo_bf,
        vec(g_moe[l]), w_r, b_r, zero_cnt, n_batch=b, tq=512)

    xs_ = x_sample.reshape(t_s, D_MODEL)
    qr, kr, vr, gr, qa, ka_s, va_s = _mix_proj(xs_, vec(g_mix[l]), w_in_bf, gq_t, gk_t, bd, tm=t_s,
                                               act_dtype=F32)
    pos_minor = lambda c: jnp.transpose(c, (0, 2, 3, 1)).reshape(bs, WIDTH, c.shape[1])
    r_s, a_s, state_s = _sample_mixer(lg_lane, gout_lane, slope_lane, qr, kr, vr, gr, qa, ka_s, va_s,
                                      state_ret[l], pos_minor(cache_win_k[l]),
                                      pos_minor(cache_win_v[l]), n_new=ns)
    y2_s, hp_s, route_s, cnt_all = _cross_router(
        xs_, r_s, a_s, w_o_bf, vec(g_cross[l]), w_mq_bf, vec(g_q_mem[l]),
        cache_mem_k[l].reshape(bs, N_MEM, D_MODEL), cache_mem_v[l].reshape(bs, N_MEM, D_MODEL),
        w_mo_bf, vec(g_moe[l]), w_r, b_r, cnt_p, n_batch=bs, tq=ns)

    tile = MOE_ROW_TILE
    counts = cnt_all[0, EXPERT_LANE0:EXPERT_LANE0 + N_EXPERTS].astype(I32)
    tiles_per = (counts + tile - 1) // tile
    tile_end = jnp.cumsum(tiles_per)
    offsets = (tile_end - tiles_per) * tile
    n_tiles = tile_end[-1:]
    n_max = (2 * (t_p + t_s)) // tile + N_EXPERTS
    tile_ids = jnp.minimum(jnp.arange(n_max, dtype=I32), n_tiles[0] - 1)
    tile_expert = jnp.sum((tile_end[None, :] <= tile_ids[:, None]).astype(I32), axis=1)

    def slots_of(route):
        e = route[:, 0:2].astype(I32)
        return offsets[e] + route[:, 4:6].astype(I32)

    slots_p, slots_s = slots_of(route_p), slots_of(route_s)
    xs_sorted = jnp.zeros((n_max * tile, D_MODEL // 2), jnp.uint32)
    xs_sorted = _dispatch(slots_p, hp_p, xs_sorted, tq=512)
    xs_sorted = _dispatch(slots_s, hp_s, xs_sorted, tq=t_s)
    ys_sorted = _experts(tile_expert, n_tiles, xs_sorted, w_exp_gate[l], w_exp_up[l], w_exp_down[l])
    y_p = _combine(slots_p, ys_sorted, y2_p, route_p, tq=256)
    y_s = _combine(slots_s, ys_sorted, y2_s, route_s, tq=t_s)

    from_pos_minor = lambda z: jnp.transpose(z.reshape(b, N_HEADS, HEAD_DIM, s), (0, 3, 1, 2))[None]
    return (y_p.reshape(b, s, D_MODEL), y_s.reshape(bs, ns, D_MODEL),
            state_p[None],
            from_pos_minor(ka_t), from_pos_minor(va_t),
            mk_p.reshape(1, b, N_MEM, N_MEM_HEADS, MEM_HEAD_DIM),
            mv_p.reshape(1, b, N_MEM, N_MEM_HEADS, MEM_HEAD_DIM),
            state_s[None],
            ka_s.reshape(1, bs, ns, N_HEADS, HEAD_DIM), va_s.reshape(1, bs, ns, N_HEADS, HEAD_DIM))
```

```python
import functools

import jax
import jax.numpy as jnp
from jax import lax
from jax.experimental import pallas as pl
from jax.experimental.pallas import tpu as pltpu

F32 = jnp.float32
BF16 = jnp.bfloat16
I32 = jnp.int32

D_MODEL = 1024
HEAD_DIM = 64
N_HEADS = 8
WIDTH = N_HEADS * HEAD_DIM
N_PAIRS = N_HEADS // 2
IN_COLS = 7 * WIDTH
RET_CHUNK = 128
RET_UNROLL = 4
ATT_GROUP = 4
ATT_BLOCK = 128
WIN_STEPS = 128
DILATIONS = (1, 4, 16)
N_MEM = 256
N_MEM_HEADS = 4
MEM_HEAD_DIM = 256
N_GROUPS = 4
N_EXP_PER_GROUP = 8
N_EXPERTS = 32
D_EXPERT = 256
EPS = 1e-6
LANES = 128
EXPERT_LANE0 = 32
MOE_ROW_TILE = 256
VMEM_LIMIT = 56 * 1024 * 1024

NEG_INF = float("-inf")


def _cparams(n_axes, vmem=VMEM_LIMIT):
    return pltpu.CompilerParams(dimension_semantics=("arbitrary",) * n_axes,
                                vmem_limit_bytes=vmem)


def _dot(a, b):
    return jnp.dot(a, b, preferred_element_type=F32)


def _dot_nt(a, b):
    return lax.dot_general(a, b, (((1,), (1,)), ((), ())), preferred_element_type=F32)


def _dot_tn(a, b):
    return lax.dot_general(a, b, (((0,), (0,)), ((), ())), preferred_element_type=F32)


def _rms(x, g):
    ms = jnp.mean(x * x, axis=-1, keepdims=True)
    return x * lax.rsqrt(ms + EPS) * g


def _mix_proj_kernel(x_ref, g_ref, w_ref, gq_ref, gk_ref,
                     qr_ref, kr_ref, vr_ref, gr_ref, qa_ref, ka_ref, va_ref, *kv_t_refs):
    act = qr_ref.dtype
    h = _rms(x_ref[...], g_ref[...]).astype(BF16)

    def proj(j):
        return _dot(h, w_ref[:, j * WIDTH:(j + 1) * WIDTH])

    def head_norm(z, g):
        is_lo = _pair_masks((z.shape[0], LANES))
        parts = []
        for p in range(N_PAIRS):
            zp = z[:, p * LANES:(p + 1) * LANES]
            parts.append(zp * lax.rsqrt(_segment_mean(zp * zp, is_lo) + EPS))
        return jnp.concatenate(parts, axis=1) * g

    qr_ref[...] = proj(0).astype(act)
    kr_ref[...] = proj(1).astype(act)
    vr_ref[...] = proj(2).astype(act)
    gr_ref[...] = proj(3).astype(act)
    qa_ref[...] = head_norm(proj(4), gq_ref[...]).astype(act)
    ka = head_norm(proj(5), gk_ref[...])
    va = proj(6)
    ka_ref[...] = ka.astype(ka_ref.dtype)
    va_ref[...] = va.astype(va_ref.dtype)
    if kv_t_refs:
        kat_ref, vat_ref = kv_t_refs
        kat_ref[0] = ka.T
        vat_ref[0] = va.T


def _mix_proj(x2d, g_mix, w_in_bf, gq_t, gk_t, tm, act_dtype, seq=None):
    t = x2d.shape[0]
    const = lambda i: (0, 0)
    row = lambda i: (i, 0)
    out_act = jax.ShapeDtypeStruct((t, WIDTH), act_dtype)
    kv_dtype = F32 if seq is None else act_dtype
    out_kv = jax.ShapeDtypeStruct((t, WIDTH), kv_dtype)
    out_specs = [pl.BlockSpec((tm, WIDTH), row)] * 7
    out_shape = [out_act] * 5 + [out_kv] * 2
    if seq is not None:
        per_b = seq // tm
        t_spec = pl.BlockSpec((1, WIDTH, tm), lambda i: (i // per_b, 0, i % per_b))
        out_specs += [t_spec, t_spec]
        out_shape += [jax.ShapeDtypeStruct((t // seq, WIDTH, seq), F32)] * 2
    return pl.pallas_call(
        _mix_proj_kernel,
        grid=(t // tm,),
        in_specs=[
            pl.BlockSpec((tm, D_MODEL), row),
            pl.BlockSpec((1, D_MODEL), const),
            pl.BlockSpec((D_MODEL, IN_COLS), const),
            pl.BlockSpec((1, WIDTH), const),
            pl.BlockSpec((1, WIDTH), const),
        ],
        out_specs=out_specs,
        out_shape=out_shape,
        compiler_params=_cparams(1),
        name="mix_proj",
    )(x2d, g_mix, w_in_bf, gq_t, gk_t)


def _pair_masks(shape):
    lane = lax.broadcasted_iota(I32, shape, len(shape) - 1)
    return lane < HEAD_DIM


def _segment_mean(x, is_lo):
    zero = jnp.zeros_like(x)
    lo = jnp.sum(jnp.where(is_lo, x, zero), axis=-1, keepdims=True)
    hi = jnp.sum(jnp.where(is_lo, zero, x), axis=-1, keepdims=True)
    return jnp.where(is_lo, lo, hi) * (1.0 / HEAD_DIM)


def _group_norm_gate(o, g_r, g_out, is_lo):
    c = o - _segment_mean(o, is_lo)
    y = c * lax.rsqrt(_segment_mean(c * c, is_lo) + EPS) * g_out
    g = g_r.astype(F32)
    return y * (g * (1.0 / (1.0 + jnp.exp(-g))))


def _retention_kernel(lg_ref, gout_ref, q_ref, k_ref, v_ref, g_ref, o_ref, st_ref, kv_ref, s_ref,
                      *, n_chunks):
    c_len = RET_CHUNK
    shape = (c_len, LANES)
    is_lo = _pair_masks(shape)
    row = lax.broadcasted_iota(I32, shape, 0)
    col = lax.broadcasted_iota(I32, shape, 1)
    rel = (row - col).astype(F32)
    lg_lane = lg_ref[0]
    lg0 = lg_lane[:, 0:1]
    lg1 = lg_lane[:, HEAD_DIM:HEAD_DIM + 1]
    scale = HEAD_DIM ** -0.5
    causal = rel >= 0.0
    relp = jnp.maximum(rel, 0.0)
    d0 = jnp.where(causal, jnp.exp(lg0 * relp), 0.0) * scale
    d1 = jnp.where(causal, jnp.exp(lg1 * relp), 0.0) * scale
    rowf = row.astype(F32)
    w_k = jnp.exp(lg_lane * (c_len - 1.0 - rowf)) * scale
    w_q = jnp.exp(lg_lane * (rowf + 1.0))
    lg_row = jnp.where(row < HEAD_DIM, lg0, lg1)
    g_chunk = jnp.exp(lg_row * float(c_len))
    same_head = (row < HEAD_DIM) == (col < HEAD_DIM)
    g_out = gout_ref[0]

    def chunk(c):
        return pl.ds(pl.multiple_of(c * c_len, c_len), c_len)

    def outer(i, carry):
        for u in range(RET_UNROLL):
            c = i * RET_UNROLL + u
            kw = (k_ref[0, chunk(c), :].astype(F32) * w_k).astype(BF16)
            kv_ref[c] = jnp.where(same_head, _dot_tn(kw, v_ref[0, chunk(c), :]), 0.0)
        return carry

    lax.fori_loop(0, n_chunks // RET_UNROLL, outer, 0)

    def recur(c, state):
        kv = kv_ref[c]
        kv_ref[c] = state
        return g_chunk * state + kv

    state = lax.fori_loop(0, n_chunks, recur, jnp.zeros(shape, F32))
    st_ref[0, 0] = state[:HEAD_DIM, :HEAD_DIM]
    st_ref[0, 1] = state[HEAD_DIM:, HEAD_DIM:]

    def score_stage(i, slot):
        for u in range(RET_UNROLL):
            sl = chunk(i * RET_UNROLL + u)
            qc = q_ref[0, sl, :]
            kc = k_ref[0, sl, :]
            zero = jnp.zeros_like(qc)
            s_ref[slot, 2 * u] = (_dot_nt(jnp.where(is_lo, qc, zero), kc) * d0).astype(BF16)
            s_ref[slot, 2 * u + 1] = (_dot_nt(jnp.where(is_lo, zero, qc), kc) * d1).astype(BF16)

    def value_stage(i, slot):
        for u in range(RET_UNROLL):
            c = i * RET_UNROLL + u
            sl = chunk(c)
            qc = q_ref[0, sl, :]
            vc = v_ref[0, sl, :]
            zero = jnp.zeros_like(vc)
            o_in = (_dot(s_ref[slot, 2 * u], jnp.where(is_lo, vc, zero))
                    + _dot(s_ref[slot, 2 * u + 1], jnp.where(is_lo, zero, vc)))
            o_x = _dot(qc, kv_ref[c].astype(BF16)) * w_q
            o_ref[0, sl, :] = _group_norm_gate(o_in + o_x, g_ref[0, sl, :], g_out,
                                               is_lo).astype(BF16)

    n_groups = n_chunks // RET_UNROLL
    score_stage(0, 0)

    def inner(i, carry):
        value_stage(i, i % 2)
        score_stage(i + 1, (i + 1) % 2)
        return carry

    lax.fori_loop(0, n_groups - 1, inner, 0)
    value_stage(n_groups - 1, (n_groups - 1) % 2)


def _retention(q, k, v, g, lg_pairs, gout_pairs):
    b, s, _ = q.shape
    blk = pl.BlockSpec((1, s, LANES), lambda i, p: (i, 0, p))
    tab = pl.BlockSpec((1, 1, LANES), lambda i, p: (p, 0, 0))
    return pl.pallas_call(
        functools.partial(_retention_kernel, n_chunks=s // RET_CHUNK),
        grid=(b, N_PAIRS),
        in_specs=[tab, tab, blk, blk, blk, blk],
        out_specs=[blk, pl.BlockSpec((1, 2, HEAD_DIM, HEAD_DIM), lambda i, p: (i, p, 0, 0))],
        out_shape=[jax.ShapeDtypeStruct((b, s, WIDTH), BF16),
                   jax.ShapeDtypeStruct((b, N_HEADS, HEAD_DIM, HEAD_DIM), F32)],
        scratch_shapes=[pltpu.VMEM((s // RET_CHUNK, LANES, LANES), F32),
                        pltpu.VMEM((2, 2 * RET_UNROLL, RET_CHUNK, RET_CHUNK), BF16)],
        compiler_params=_cparams(2),
        name="retention",
    )(lg_pairs, gout_pairs, q, k, v, g)


def _attn_scores(qb, kb, bias0, bias1, is_lo):
    qb = qb.astype(BF16)
    kb = kb.astype(BF16)
    zq = jnp.zeros_like(qb)
    s0 = _dot_nt(jnp.where(is_lo, qb, zq), kb) + bias0
    s1 = _dot_nt(jnp.where(is_lo, zq, qb), kb) + bias1
    return s0, s1


def _attn_values(s0, s1, vb, is_lo):
    m0 = jnp.max(s0, axis=-1, keepdims=True)
    m1 = jnp.max(s1, axis=-1, keepdims=True)
    p0 = jnp.exp(s0 - m0).astype(BF16)
    p1 = jnp.exp(s1 - m1).astype(BF16)
    is_lo_k = _pair_masks(vb.shape)
    one = jnp.ones_like(vb)
    r0 = _dot(p0, jnp.where(is_lo_k, vb, one).astype(BF16))
    r1 = _dot(p1, jnp.where(is_lo_k, one, vb).astype(BF16))
    acc = jnp.where(is_lo, r0, r1)
    l = pltpu.roll(jnp.where(is_lo, r1, r0), HEAD_DIM, 1)
    m = jnp.where(is_lo, m0, m1)
    return acc, m, l


def _dil_attn_kernel(sl_ref, q_ref, k_ref, v_ref, o_ref,
                     qf_ref, kf_ref, vf_ref, acc_ref, m_ref, l_ref, bias_ref, bias1_ref, s_ref, *,
                     seq):
    blk = ATT_BLOCK
    is_lo = _pair_masks((blk, LANES))
    slope = sl_ref[0]
    slope0 = slope[:, 0:1]
    slope1 = slope[:, HEAD_DIM:HEAD_DIM + 1]

    qf_ref[...] = q_ref[0].astype(F32) * (HEAD_DIM ** -0.5)
    kf_ref[...] = k_ref[0].astype(F32)
    vf_ref[...] = v_ref[0].astype(F32)

    i2 = lax.broadcasted_iota(I32, (blk, 2 * blk), 0)
    j2 = lax.broadcasted_iota(I32, (blk, 2 * blk), 1)
    rel2 = blk + i2 - j2
    ok2 = (rel2 >= 0) & (rel2 <= WIN_STEPS)
    i1 = lax.broadcasted_iota(I32, (blk, blk), 0)
    j1 = lax.broadcasted_iota(I32, (blk, blk), 1)
    rel1 = i1 - j1
    ok1 = rel1 >= 0
    for pi, d in enumerate(DILATIONS):
        dist2 = (rel2 * d).astype(F32)
        dist1 = (rel1 * d).astype(F32)
        bias_ref[2 * pi] = jnp.where(ok2, -slope0 * dist2, NEG_INF)
        bias_ref[2 * pi + 1] = jnp.where(ok2, -slope1 * dist2, NEG_INF)
        bias1_ref[2 * pi] = jnp.where(ok1, -slope0 * dist1, NEG_INF)
        bias1_ref[2 * pi + 1] = jnp.where(ok1, -slope1 * dist1, NEG_INF)

    def rows(start, n, d):
        return pl.ds(start, n) if d == 1 else pl.ds(start, n, stride=d)

    def store(pi, sl, res):
        acc, m, l = res
        acc_ref[pi, sl, :] = acc
        m_ref[pi, sl, :] = m
        l_ref[pi, sl, :] = l

    def key_rows(q0, k0, d):
        return rows(q0, blk, d) if k0 is None else rows(k0, 2 * blk, d)

    def score_stage(pi, d, items, slot):
        for u, (q0, k0) in enumerate(items):
            b_ref, nk = (bias1_ref, blk) if k0 is None else (bias_ref, 2 * blk)
            s0, s1 = _attn_scores(qf_ref[rows(q0, blk, d), :], kf_ref[key_rows(q0, k0, d), :],
                                  b_ref[2 * pi], b_ref[2 * pi + 1], is_lo)
            s_ref[slot, 2 * u, :, :nk] = s0
            s_ref[slot, 2 * u + 1, :, :nk] = s1

    def value_stage(pi, d, items, slot):
        for u, (q0, k0) in enumerate(items):
            nk = blk if k0 is None else 2 * blk
            res = _attn_values(s_ref[slot, 2 * u, :, :nk], s_ref[slot, 2 * u + 1, :, :nk],
                               vf_ref[key_rows(q0, k0, d), :], is_lo)
            store(pi, rows(q0, blk, d), res)

    def pipeline(pi, d, n_groups, items_of):
        score_stage(pi, d, items_of(0), 0)

        def body(g, carry):
            value_stage(pi, d, items_of(g), g % 2)
            score_stage(pi, d, items_of(g + 1), (g + 1) % 2)
            return carry

        lax.fori_loop(0, n_groups - 1, body, 0)
        value_stage(pi, d, items_of(n_groups - 1), (n_groups - 1) % 2)

    for pi, d in enumerate(DILATIONS):
        n_blocks = seq // d // blk
        step = blk * d
        firsts = [(r, None) for r in range(d)]
        if n_blocks == 1:
            assert d % ATT_GROUP == 0
            pipeline(pi, d, d // ATT_GROUP,
                     lambda g: [(g * ATT_GROUP + u, None) for u in range(ATT_GROUP)])
            continue
        assert d <= ATT_GROUP
        score_stage(pi, d, firsts, 0)
        value_stage(pi, d, firsts, 0)
        per_group = max(p for p in range(1, ATT_GROUP // d + 1) if (n_blocks - 1) % p == 0)

        def general(g, d=d, step=step, per_group=per_group):
            items = []
            for u in range(per_group):
                base = (1 + g * per_group + u) * step
                if d == 1:
                    base = pl.multiple_of(base, blk)
                items += [(base + r, base + r - step) for r in range(d)]
            return items

        pipeline(pi, d, (n_blocks - 1) // per_group, general)

    def merge(c, carry):
        sl = pl.ds(pl.multiple_of(c * 256, 256), 256)
        m_all = [m_ref[pi, sl, :] for pi in range(3)]
        m_max = jnp.maximum(jnp.maximum(m_all[0], m_all[1]), m_all[2])
        num = jnp.zeros((256, LANES), F32)
        den = jnp.zeros((256, LANES), F32)
        for pi in range(3):
            w = jnp.exp(m_all[pi] - m_max)
            num = num + w * acc_ref[pi, sl, :]
            den = den + w * l_ref[pi, sl, :]
        o_ref[0, sl, :] = (num / den).astype(BF16)
        return carry

    lax.fori_loop(0, seq // 256, merge, 0)


def _dilated_attention(qa, ka, va, slope_pairs):
    b, s, _ = qa.shape
    blk = pl.BlockSpec((1, s, LANES), lambda i, p: (i, 0, p))
    tab = pl.BlockSpec((1, 1, LANES), lambda i, p: (p, 0, 0))
    return pl.pallas_call(
        functools.partial(_dil_attn_kernel, seq=s),
        grid=(b, N_PAIRS),
        in_specs=[tab, blk, blk, blk],
        out_specs=blk,
        out_shape=jax.ShapeDtypeStruct((b, s, WIDTH), BF16),
        scratch_shapes=[
            pltpu.VMEM((s, LANES), F32),
            pltpu.VMEM((s, LANES), F32),
            pltpu.VMEM((s, LANES), F32),
            pltpu.VMEM((3, s, LANES), F32),
            pltpu.VMEM((3, s, LANES), F32),
            pltpu.VMEM((3, s, LANES), F32),
            pltpu.VMEM((6, ATT_BLOCK, 2 * ATT_BLOCK), F32),
            pltpu.VMEM((6, ATT_BLOCK, ATT_BLOCK), F32),
            pltpu.VMEM((2, 2 * ATT_GROUP, ATT_BLOCK, 2 * ATT_BLOCK), F32),
        ],
        compiler_params=_cparams(2),
        name="dilated_attention",
    )(slope_pairs, qa, ka, va)


def _sample_mixer_kernel(lg_ref, gout_ref, slope_ref, qr_ref, kr_ref, vr_ref, gr_ref,
                         qa_ref, ka_ref, va_ref, st_ref, ck_ref, cv_ref,
                         r_ref, a_ref, sto_ref, bias_ref, cnt_ref, *, n_new, w_buf):
    n = n_new
    scale = HEAD_DIM ** -0.5
    lg = lg_ref[...]
    qr = qr_ref[...]
    kr = kr_ref[...]
    vr = vr_ref[...]
    ri = lax.broadcasted_iota(I32, (n, n), 0)
    rj = lax.broadcasted_iota(I32, (n, n), 1)
    rel = (ri - rj).astype(F32)
    rowf = lax.broadcasted_iota(I32, (n, HEAD_DIM), 0).astype(F32)
    outs = []
    for h in range(N_HEADS):
        hs = slice(h * HEAD_DIM, (h + 1) * HEAD_DIM)
        lg_h = lg[:, h * HEAD_DIM:h * HEAD_DIM + 1]
        qh, kh, vh = qr[:, hs], kr[:, hs], vr[:, hs]
        dm = jnp.where(rel >= 0.0, jnp.exp(lg_h * jnp.maximum(rel, 0.0)), 0.0) * scale
        s = _dot_nt(qh, kh) * dm
        s_prev = st_ref[0, h]
        o = _dot(s, vh) + _dot(qh, s_prev) * jnp.exp(lg_h * (rowf + 1.0))
        kw = kh * (jnp.exp(lg_h * (n - 1.0 - rowf)) * scale)
        sto_ref[0, h] = jnp.exp(lg_h * float(n)) * s_prev + _dot_tn(kw, vh)
        outs.append(o)
    o_r = jnp.concatenate(outs, axis=1)
    normed = []
    for p in range(N_PAIRS):
        ps = slice(p * LANES, (p + 1) * LANES)
        is_lo = _pair_masks((n, LANES))
        normed.append(_group_norm_gate(o_r[:, ps], gr_ref[:, ps], gout_ref[:, ps], is_lo))
    r_ref[...] = jnp.concatenate(normed, axis=1)

    n_rows = N_HEADS * n
    n_keys = w_buf + LANES

    @pl.when(pl.program_id(0) == 0)
    def _():
        rr = lax.broadcasted_iota(I32, (n_rows, n_keys), 0)
        cc = lax.broadcasted_iota(I32, (n_rows, n_keys), 1)
        qi = rr % n
        dist = w_buf + qi - cc
        valid = (dist >= 0) & (cc < w_buf + n)
        cnt = jnp.zeros((n_rows, n_keys), F32)
        for d in DILATIONS:
            hit = valid & (dist % d == 0) & (dist <= WIN_STEPS * d)
            cnt = cnt + jnp.where(hit, 1.0, 0.0)
        cnt_ref[...] = cnt
        slope = slope_ref[...]
        srow = jnp.zeros((n_rows, 1), F32)
        r1 = lax.broadcasted_iota(I32, (n_rows, 1), 0)
        for h in range(N_HEADS):
            srow = jnp.where(r1 // n == h, slope[:, h * HEAD_DIM:h * HEAD_DIM + 1], srow)
        bias_ref[...] = jnp.where(cnt > 0.0, -srow * dist.astype(F32), NEG_INF)

    qa = qa_ref[...]
    q_rows = jnp.concatenate([qa] * N_HEADS, axis=0)
    r2 = lax.broadcasted_iota(I32, (n_rows, WIDTH), 0)
    c2 = lax.broadcasted_iota(I32, (n_rows, WIDTH), 1)
    q_blk = jnp.where(r2 // n == c2 // HEAD_DIM, q_rows, 0.0).astype(BF16)
    pad = jnp.zeros((LANES - n, WIDTH), F32)
    k_new = jnp.concatenate([ka_ref[...], pad], axis=0).astype(BF16)
    v_new = jnp.concatenate([va_ref[...], pad], axis=0).astype(BF16)
    s_old = _dot(q_blk, ck_ref[0].astype(BF16))
    s_new = _dot_nt(q_blk, k_new)
    s = jnp.concatenate([s_old, s_new], axis=1) * scale + bias_ref[...]
    m = jnp.max(s, axis=-1, keepdims=True)
    p = cnt_ref[...] * jnp.exp(s - m)
    den = jnp.sum(p, axis=-1, keepdims=True)
    p = p.astype(BF16)
    o_all = _dot_nt(p[:, :w_buf], cv_ref[0].astype(BF16)) + _dot(p[:, w_buf:], v_new)
    o_all = o_all / den
    heads = [o_all[h * n:(h + 1) * n, h * HEAD_DIM:(h + 1) * HEAD_DIM] for h in range(N_HEADS)]
    a_ref[...] = jnp.concatenate(heads, axis=1)


def _sample_mixer(lg_lane, gout, slope_lane, qr, kr, vr, gr, qa, ka, va, state, ck, cv, n_new):
    b = state.shape[0]
    w_buf = ck.shape[2]
    tok = pl.BlockSpec((n_new, WIDTH), lambda i: (i, 0))
    tab = pl.BlockSpec((1, WIDTH), lambda i: (0, 0))
    st = pl.BlockSpec((1, N_HEADS, HEAD_DIM, HEAD_DIM), lambda i: (i, 0, 0, 0))
    cache = pl.BlockSpec((1, WIDTH, w_buf), lambda i: (i, 0, 0))
    n_rows = N_HEADS * n_new
    return pl.pallas_call(
        functools.partial(_sample_mixer_kernel, n_new=n_new, w_buf=w_buf),
        grid=(b,),
        in_specs=[tab, tab, tab, tok, tok, tok, tok, tok, tok, tok, st, cache, cache],
        out_specs=[tok, tok, st],
        out_shape=[jax.ShapeDtypeStruct((b * n_new, WIDTH), F32),
                   jax.ShapeDtypeStruct((b * n_new, WIDTH), F32),
                   jax.ShapeDtypeStruct(state.shape, F32)],
        scratch_shapes=[pltpu.VMEM((n_rows, w_buf + LANES), F32),
                        pltpu.VMEM((n_rows, w_buf + LANES), F32)],
        compiler_params=_cparams(1),
        name="sample_mixer",
    )(lg_lane, gout, slope_lane, qr, kr, vr, gr, qa, ka, va, state, ck, cv)


def _mem_kv_kernel(x_ref, g_ref, wk_ref, wv_ref, gk_ref, k_ref, v_ref):
    h = _rms(x_ref[...], g_ref[...]).astype(BF16)
    k = _dot(h, wk_ref[...])
    gk = gk_ref[...]
    for hd in range(N_MEM_HEADS):
        hs = slice(hd * MEM_HEAD_DIM, (hd + 1) * MEM_HEAD_DIM)
        k_ref[:, hs] = _rms(k[:, hs], gk)
    v_ref[...] = _dot(h, wv_ref[...])


def _mem_kv(mem2d, g_mem, w_mk_bf, w_mv_bf, g_k_mem, tm=256):
    t = mem2d.shape[0]
    const = lambda i: (0, 0)
    row = lambda i: (i, 0)
    out = jax.ShapeDtypeStruct((t, D_MODEL), F32)
    return pl.pallas_call(
        _mem_kv_kernel,
        grid=(t // tm,),
        in_specs=[pl.BlockSpec((tm, D_MODEL), row), pl.BlockSpec((1, D_MODEL), const),
                  pl.BlockSpec((D_MODEL, D_MODEL), const), pl.BlockSpec((D_MODEL, D_MODEL), const),
                  pl.BlockSpec((1, MEM_HEAD_DIM), const)],
        out_specs=[pl.BlockSpec((tm, D_MODEL), row)] * 2,
        out_shape=[out, out],
        compiler_params=_cparams(1),
        name="mem_kv",
    )(mem2d, g_mem, w_mk_bf, w_mv_bf, g_k_mem)


def _cross_router_kernel(x_ref, r_ref, a_ref, wo_ref, gc_ref, wmq_ref, gqm_ref, mk_ref, mv_ref,
                         wmo_ref, gmoe_ref, wrh_ref, wrl_ref, br_ref, cin_ref,
                         y_ref, hp_ref, gate_ref, pair_ref, cnt_ref, carry_ref, *, tq, n_split):
    first = (pl.program_id(0) == 0) & (pl.program_id(1) == 0)

    @pl.when(first)
    def _():
        carry_ref[...] = cin_ref[...]

    half = D_MODEL // 2
    gqm = gqm_ref[...]
    mem_k = [mk_ref[0, :, hd * MEM_HEAD_DIM:(hd + 1) * MEM_HEAD_DIM].astype(BF16)
             for hd in range(N_MEM_HEADS)]
    mem_v = [mv_ref[0, :, hd * MEM_HEAD_DIM:(hd + 1) * MEM_HEAD_DIM].astype(BF16)
             for hd in range(N_MEM_HEADS)]

    def rows_block(rs):
        ra = jnp.concatenate([r_ref[rs, :], a_ref[rs, :]], axis=1).astype(BF16)
        y1 = x_ref[rs, :] + _dot(ra, wo_ref[...])
        q = _dot(_rms(y1, gc_ref[...]).astype(BF16), wmq_ref[...])
        heads = []
        for hd in range(N_MEM_HEADS):
            hs = slice(hd * MEM_HEAD_DIM, (hd + 1) * MEM_HEAD_DIM)
            qh = _rms(q[:, hs], gqm).astype(BF16)
            s = _dot_nt(qh, mem_k[hd]) * (MEM_HEAD_DIM ** -0.5)
            m = jnp.max(s, axis=-1, keepdims=True)
            p = jnp.exp(s - m)
            p = p / jnp.sum(p, axis=-1, keepdims=True)
            heads.append(_dot(p.astype(BF16), mem_v[hd]))
        o = jnp.concatenate(heads, axis=1).astype(BF16)
        y2 = y1 + _dot(o, wmo_ref[...])
        y_ref[rs, :] = y2
        hm = _rms(y2, gmoe_ref[...])
        hp_ref[rs, :] = pltpu.pack_elementwise([hm[:, :half], hm[:, half:]], packed_dtype=BF16)
        hi = hm.astype(BF16)
        lo = (hm - hi.astype(F32)).astype(BF16)
        return _dot(hi, wrh_ref[...]) + _dot(lo, wrh_ref[...]) + _dot(hi, wrl_ref[...])

    sub = tq // n_split
    logits = jnp.concatenate([rows_block(pl.ds(i * sub, sub)) for i in range(n_split)], axis=0)
    logits = logits + br_ref[...]
    lane = lax.broadcasted_iota(I32, (tq, LANES), 1)
    lanef = lane.astype(F32)
    big = float(LANES)
    is_grp = lane < N_GROUPS
    gl = jnp.where(is_grp, logits, NEG_INF)
    gmax = jnp.max(gl, axis=-1, keepdims=True)
    gidx = jnp.min(jnp.where(gl == gmax, lanef, big), axis=-1, keepdims=True)
    p_top = 1.0 / jnp.sum(jnp.where(is_grp, jnp.exp(logits - gmax), 0.0), axis=-1, keepdims=True)
    lo = EXPERT_LANE0 + N_EXP_PER_GROUP * gidx
    in_grp = (lanef >= lo) & (lanef < lo + N_EXP_PER_GROUP)
    el = jnp.where(in_grp, logits, NEG_INF)
    v1 = jnp.max(el, axis=-1, keepdims=True)
    i1 = jnp.min(jnp.where(el == v1, lanef, big), axis=-1, keepdims=True)
    el2 = jnp.where(lanef == i1, NEG_INF, el)
    v2 = jnp.max(el2, axis=-1, keepdims=True)
    i2 = jnp.min(jnp.where(el2 == v2, lanef, big), axis=-1, keepdims=True)
    e21 = jnp.exp(v2 - v1)
    w1 = p_top / (1.0 + e21)
    w2 = p_top * e21 / (1.0 + e21)

    hit1 = lanef == i1
    hit2 = lanef == i2
    onehot = jnp.where(hit1 | hit2, 1.0, 0.0)
    ti = lax.broadcasted_iota(I32, (tq, tq), 0)
    tj = lax.broadcasted_iota(I32, (tq, tq), 1)
    tri = jnp.where(tj < ti, 1.0, 0.0)
    if tq >= 16:
        before = _dot(tri.astype(BF16), onehot.astype(BF16))
    else:
        before = _dot(tri, onehot)
    before = before + carry_ref[...]
    rank1 = jnp.sum(jnp.where(hit1, before, 0.0), axis=-1, keepdims=True)
    rank2 = jnp.sum(jnp.where(hit2, before, 0.0), axis=-1, keepdims=True)
    carry_ref[...] = carry_ref[...] + jnp.sum(onehot, axis=0, keepdims=True)
    cnt_ref[...] = carry_ref[...]

    e1 = i1 - EXPERT_LANE0
    e2 = i2 - EXPERT_LANE0
    gates = jnp.zeros((tq, LANES), F32)
    pairs = jnp.zeros((tq, LANES), F32)
    for idx, val in enumerate((w1, w2)):
        gates = jnp.where(lane == idx, val, gates)
    for idx, val in enumerate((e1, e2, rank1, rank2)):
        pairs = jnp.where(lane == idx, val, pairs)
    gate_ref[...] = gates[:, :2]
    pair_ref[...] = pairs[:, :4].astype(I32)


def _cross_router(x2d, r, a, w_o, g_cross, w_mq, g_q_mem, mk, mv, w_mo, g_moe, w_r_hi, w_r_lo, b_r,
                  cnt_in, n_batch, tq, n_split):
    t = x2d.shape[0]
    per_b = t // n_batch // tq
    row = lambda i, j: (i * per_b + j, 0)
    const = lambda i, j: (0, 0)
    mem = pl.BlockSpec((1, N_MEM, D_MODEL), lambda i, j: (i, 0, 0))
    wspec = pl.BlockSpec((D_MODEL, D_MODEL), const)
    vec = pl.BlockSpec((1, D_MODEL), const)
    rspec = pl.BlockSpec((D_MODEL, LANES), const)
    return pl.pallas_call(
        functools.partial(_cross_router_kernel, tq=tq, n_split=n_split),
        grid=(n_batch, per_b),
        in_specs=[pl.BlockSpec((tq, D_MODEL), row), pl.BlockSpec((tq, WIDTH), row),
                  pl.BlockSpec((tq, WIDTH), row), wspec, vec, wspec,
                  pl.BlockSpec((1, MEM_HEAD_DIM), const), mem, mem, wspec, vec,
                  rspec, rspec, pl.BlockSpec((1, LANES), const),
                  pl.BlockSpec((1, LANES), const)],
        out_specs=[pl.BlockSpec((tq, D_MODEL), row), pl.BlockSpec((tq, D_MODEL // 2), row),
                   pl.BlockSpec((tq, 2), row), pl.BlockSpec((tq, 4), row),
                   pl.BlockSpec((1, LANES), const)],
        out_shape=[jax.ShapeDtypeStruct((t, D_MODEL), F32),
                   jax.ShapeDtypeStruct((t, D_MODEL // 2), jnp.uint32),
                   jax.ShapeDtypeStruct((t, 2), F32),
                   jax.ShapeDtypeStruct((t, 4), I32),
                   jax.ShapeDtypeStruct((1, LANES), F32)],
        scratch_shapes=[pltpu.VMEM((1, LANES), F32)],
        compiler_params=_cparams(2),
        name="cross_router",
    )(x2d, r, a, w_o, g_cross, w_mq, g_q_mem, mk, mv, w_mo, g_moe, w_r_hi, w_r_lo, b_r, cnt_in)


def _row_copy(src, src_row, dst, dst_row, sem):
    return pltpu.make_async_copy(src.at[pl.ds(src_row, 1)], dst.at[pl.ds(dst_row, 1)], sem)


def _pair_slot(off_ref, pairs_ref, j, kk):
    return off_ref[pairs_ref[0, 0, 4 * j + kk]] + pairs_ref[0, 0, 4 * j + 2 + kk]


def _dispatch_kernel(off_ref, zt_ref, pairs_ref, hp_ref, *rest, tq, n_zero):
    if n_zero:
        xs_ref, zero_ref, sem = rest

        @pl.when(pl.program_id(0) == 0)
        def _():
            zero_ref[...] = jnp.zeros_like(zero_ref)
            def fill(i):
                tile = jnp.maximum(zt_ref[i], 0) * MOE_ROW_TILE
                return pltpu.make_async_copy(zero_ref, xs_ref.at[pl.ds(tile, MOE_ROW_TILE)], sem)

            for i in range(n_zero):
                pl.when(zt_ref[i] >= 0)(fill(i).start)
            for i in range(n_zero):
                pl.when(zt_ref[i] >= 0)(fill(i).wait)
    else:
        _, xs_ref, sem = rest

    def start(j, carry):
        for kk in range(2):
            _row_copy(hp_ref, j, xs_ref, _pair_slot(off_ref, pairs_ref, j, kk), sem).start()
        return carry

    def wait(j, carry):
        for kk in range(2):
            _row_copy(hp_ref, j, xs_ref, _pair_slot(off_ref, pairs_ref, j, kk), sem).wait()
        return carry

    lax.fori_loop(0, tq, start, 0, unroll=8)
    lax.fori_loop(0, tq, wait, 0, unroll=8)


def _dispatch(offsets, zero_tiles, pairs, hp, xs, n_rows, tq):
    t = hp.shape[0]
    init = xs is None
    in_specs = [pl.BlockSpec((1, 1, 4 * tq), lambda i, off, zt: (i, 0, 0), memory_space=pltpu.SMEM),
                pl.BlockSpec((tq, D_MODEL // 2), lambda i, off, zt: (i, 0))]
    args = [pairs.reshape(t // tq, 1, 4 * tq), hp]
    scratch = [pltpu.SemaphoreType.DMA(())]
    if init:
        scratch = [pltpu.VMEM((MOE_ROW_TILE, D_MODEL // 2), jnp.uint32)] + scratch
    else:
        in_specs.append(pl.BlockSpec(memory_space=pl.ANY))
        args.append(xs)
    return pl.pallas_call(
        functools.partial(_dispatch_kernel, tq=tq, n_zero=zero_tiles.shape[0] if init else 0),
        grid_spec=pltpu.PrefetchScalarGridSpec(
            num_scalar_prefetch=2, grid=(t // tq,), in_specs=in_specs,
            out_specs=pl.BlockSpec(memory_space=pl.ANY), scratch_shapes=scratch),
        out_shape=jax.ShapeDtypeStruct((n_rows, D_MODEL // 2), jnp.uint32),
        input_output_aliases={} if init else {4: 0},
        compiler_params=pltpu.CompilerParams(dimension_semantics=("arbitrary",),
                                             has_side_effects=True),
        name="moe_dispatch",
    )(offsets, zero_tiles, *args)


def _unpack_pair(words):
    lo = pltpu.unpack_elementwise(words, index=0, packed_dtype=BF16, unpacked_dtype=F32)
    hi = pltpu.unpack_elementwise(words, index=1, packed_dtype=BF16, unpacked_dtype=F32)
    return lo, hi


def _expert_kernel(te_ref, nt_ref, xs_ref, wg_ref, wu_ref, wd_ref, ys_ref):
    del te_ref
    half = D_MODEL // 2

    @pl.when(pl.program_id(0) < nt_ref[0])
    def _():
        lo, hi = _unpack_pair(xs_ref[...])
        lo = lo.astype(BF16)
        hi = hi.astype(BF16)
        wg = wg_ref[0].astype(BF16)
        wu = wu_ref[0].astype(BF16)
        g = _dot(lo, wg[:half]) + _dot(hi, wg[half:])
        u = _dot(lo, wu[:half]) + _dot(hi, wu[half:])
        hid = (g * (1.0 / (1.0 + jnp.exp(-g))) * u).astype(BF16)
        y = _dot(hid, wd_ref[0].astype(BF16))
        ys_ref[...] = pltpu.pack_elementwise([y[:, :half], y[:, half:]], packed_dtype=BF16)

    @pl.when(pl.program_id(0) >= nt_ref[0])
    def _():
        ys_ref[...] = jnp.zeros_like(ys_ref)


def _experts(tile_expert, n_tiles, xs, w_gate, w_up, w_down):
    rows = xs.shape[0]
    n_max = rows // MOE_ROW_TILE

    def xmap(i, te, nt):
        return (jnp.maximum(jnp.minimum(i, nt[0] - 1), 0), 0)

    def wmap(i, te, nt):
        return (te[i], 0, 0)

    return pl.pallas_call(
        _expert_kernel,
        grid_spec=pltpu.PrefetchScalarGridSpec(
            num_scalar_prefetch=2,
            grid=(n_max,),
            in_specs=[pl.BlockSpec((MOE_ROW_TILE, D_MODEL // 2), xmap),
                      pl.BlockSpec((1, D_MODEL, D_EXPERT), wmap),
                      pl.BlockSpec((1, D_MODEL, D_EXPERT), wmap),
                      pl.BlockSpec((1, D_EXPERT, D_MODEL), wmap)],
            out_specs=pl.BlockSpec((MOE_ROW_TILE, D_MODEL // 2), lambda i, te, nt: (i, 0)),
        ),
        out_shape=jax.ShapeDtypeStruct(xs.shape, xs.dtype),
        compiler_params=_cparams(1),
        name="moe_experts",
    )(tile_expert, n_tiles, xs, w_gate, w_up, w_down)


def _combine_kernel(off_ref, pairs_ref, pairs_next_ref, ys_ref, y_ref, gate_ref, o_ref,
                    buf_ref, sem, *, tq):
    step = pl.program_id(0)
    n_steps = pl.num_programs(0)

    def gather(p_ref, slot, wait):
        def body(j, carry):
            for kk in range(2):
                cp = pltpu.make_async_copy(ys_ref.at[pl.ds(_pair_slot(off_ref, p_ref, j, kk), 1)],
                                           buf_ref.at[slot, kk, pl.ds(j, 1)], sem.at[slot])
                cp.wait() if wait else cp.start()
            return carry
        lax.fori_loop(0, tq, body, 0, unroll=8)

    @pl.when(step == 0)
    def _():
        gather(pairs_ref, 0, wait=False)

    @pl.when(step + 1 < n_steps)
    def _():
        gather(pairs_next_ref, (step + 1) % 2, wait=False)

    slot = step % 2
    gather(pairs_ref, slot, wait=True)
    half = D_MODEL // 2
    gates = gate_ref[...]
    w1 = gates[:, 0:1]
    w2 = gates[:, 1:2]
    lo1, hi1 = _unpack_pair(buf_ref[slot, 0])
    lo2, hi2 = _unpack_pair(buf_ref[slot, 1])
    y = y_ref[...]
    o_ref[:, :half] = y[:, :half] + (w1 * lo1 + w2 * lo2)
    o_ref[:, half:] = y[:, half:] + (w1 * hi1 + w2 * hi2)


def _combine(offsets, pairs, ys, y2, gates, tq):
    t = y2.shape[0]
    n = t // tq
    row = lambda i, off: (i, 0)
    smem = lambda f: pl.BlockSpec((1, 1, 4 * tq), f, memory_space=pltpu.SMEM)
    pairs3 = pairs.reshape(n, 1, 4 * tq)
    return pl.pallas_call(
        functools.partial(_combine_kernel, tq=tq),
        grid_spec=pltpu.PrefetchScalarGridSpec(
            num_scalar_prefetch=1, grid=(n,),
            in_specs=[smem(lambda i, off: (i, 0, 0)),
                      smem(lambda i, off: (jnp.minimum(i + 1, n - 1), 0, 0)),
                      pl.BlockSpec(memory_space=pl.ANY),
                      pl.BlockSpec((tq, D_MODEL), row), pl.BlockSpec((tq, 2), row)],
            out_specs=pl.BlockSpec((tq, D_MODEL), row),
            scratch_shapes=[pltpu.VMEM((2, 2, tq, D_MODEL // 2), jnp.uint32),
                            pltpu.SemaphoreType.DMA((2,))]),
        out_shape=jax.ShapeDtypeStruct((t, D_MODEL), F32),
        compiler_params=_cparams(1),
        name="moe_combine",
    )(offsets, pairs3, pairs3, ys, y2, gates)


def _lane_table(per_head):
    lanes = jnp.repeat(per_head.astype(F32), HEAD_DIM)
    return lanes.reshape(N_PAIRS, 1, LANES), lanes.reshape(1, WIDTH)


def kernel(x_prompt, x_sample, mem_prompt, state_ret, cache_win_k, cache_win_v, cache_mem_k,
           cache_mem_v, g_mix, w_in, g_ret_out, g_q_att, g_k_att, w_o, g_cross, g_mem, w_mq, w_mk,
           w_mv, g_q_mem, g_k_mem, w_mo, g_moe, w_router_group, b_router_group, w_router_expert,
           b_router_expert, w_exp_gate, w_exp_up, w_exp_down):
    depth = g_mix.shape[0]
    assert depth == 1
    b, s, _ = x_prompt.shape
    bs, ns, _ = x_sample.shape
    t_p, t_s = b * s, bs * ns
    l = 0

    heads = jnp.arange(N_HEADS, dtype=F32)
    lg_pairs, lg_lane = _lane_table(jnp.log(1.0 - 2.0 ** (-5.0 - heads)))
    slope_pairs, slope_lane = _lane_table(2.0 ** (-8.0 * (heads + 1.0) / N_HEADS))
    gout_lane = g_ret_out[l].reshape(1, WIDTH)
    gout_pairs = g_ret_out[l].reshape(N_PAIRS, 1, LANES)
    gq_t = jnp.tile(g_q_att[l], N_HEADS).reshape(1, WIDTH)
    gk_t = jnp.tile(g_k_att[l], N_HEADS).reshape(1, WIDTH)
    vec = lambda v: v.reshape(1, -1)
    w_in_bf = w_in[l].astype(BF16)
    w_o_bf, w_mq_bf, w_mk_bf = w_o[l].astype(BF16), w_mq[l].astype(BF16), w_mk[l].astype(BF16)
    w_mv_bf, w_mo_bf = w_mv[l].astype(BF16), w_mo[l].astype(BF16)
    gap = EXPERT_LANE0 - N_GROUPS
    tail = LANES - EXPERT_LANE0 - N_EXPERTS
    w_r = jnp.concatenate(
        [w_router_group[l], jnp.zeros((D_MODEL, gap), F32),
         jnp.moveaxis(w_router_expert[l], 0, 1).reshape(D_MODEL, N_EXPERTS),
         jnp.zeros((D_MODEL, tail), F32)], axis=1)
    b_r = jnp.concatenate([b_router_group[l], jnp.zeros((gap,), F32),
                           b_router_expert[l].reshape(-1), jnp.zeros((tail,), F32)]).reshape(1, LANES)
    w_r_hi = w_r.astype(BF16)
    w_r_lo = (w_r - w_r_hi.astype(F32)).astype(BF16)

    xp = x_prompt.reshape(t_p, D_MODEL)
    qr, kr, vr, gr, qa, ka, va, ka_t, va_t = _mix_proj(xp, vec(g_mix[l]), w_in_bf, gq_t, gk_t,
                                                       tm=512, act_dtype=BF16, seq=s)
    shp = lambda z: z.reshape(b, s, WIDTH)
    r_p, state_p = _retention(shp(qr), shp(kr), shp(vr), shp(gr), lg_pairs, gout_pairs)
    a_p = _dilated_attention(shp(qa), shp(ka), shp(va), slope_pairs)
    mk_p, mv_p = _mem_kv(mem_prompt.reshape(b * N_MEM, D_MODEL), vec(g_mem[l]), w_mk_bf, w_mv_bf,
                          vec(g_k_mem[l]))
    zero_cnt = jnp.zeros((1, LANES), F32)
    y2_p, hp_p, gates_p, pairs_p, cnt_p = _cross_router(
        xp, r_p.reshape(t_p, WIDTH), a_p.reshape(t_p, WIDTH), w_o_bf, vec(g_cross[l]), w_mq_bf,
        vec(g_q_mem[l]), mk_p.reshape(b, N_MEM, D_MODEL), mv_p.reshape(b, N_MEM, D_MODEL), w_mo_bf,
        vec(g_moe[l]), w_r_hi, w_r_lo, b_r, zero_cnt, n_batch=b, tq=512, n_split=2)

    xs_ = x_sample.reshape(t_s, D_MODEL)
    qr, kr, vr, gr, qa, ka_s, va_s = _mix_proj(xs_, vec(g_mix[l]), w_in_bf, gq_t, gk_t, tm=t_s,
                                               act_dtype=F32)
    pos_minor = lambda c: jnp.transpose(c, (0, 2, 3, 1)).reshape(bs, WIDTH, c.shape[1])
    r_s, a_s, state_s = _sample_mixer(lg_lane, gout_lane, slope_lane, qr, kr, vr, gr, qa, ka_s, va_s,
                                      state_ret[l], pos_minor(cache_win_k[l]),
                                      pos_minor(cache_win_v[l]), n_new=ns)
    y2_s, hp_s, gates_s, pairs_s, cnt_all = _cross_router(
        xs_, r_s, a_s, w_o_bf, vec(g_cross[l]), w_mq_bf, vec(g_q_mem[l]),
        cache_mem_k[l].reshape(bs, N_MEM, D_MODEL), cache_mem_v[l].reshape(bs, N_MEM, D_MODEL),
        w_mo_bf, vec(g_moe[l]), w_r_hi, w_r_lo, b_r, cnt_p, n_batch=bs, tq=ns, n_split=1)

    tile = MOE_ROW_TILE
    counts = cnt_all[0, EXPERT_LANE0:EXPERT_LANE0 + N_EXPERTS].astype(I32)
    tiles_per = (counts + tile - 1) // tile
    tile_end = jnp.cumsum(tiles_per)
    offsets = (tile_end - tiles_per) * tile
    n_tiles = tile_end[-1:]
    n_max = (2 * (t_p + t_s)) // tile + N_EXPERTS
    tile_ids = jnp.minimum(jnp.arange(n_max, dtype=I32), n_tiles[0] - 1)
    tile_expert = jnp.sum((tile_end[None, :] <= tile_ids[:, None]).astype(I32), axis=1)
    assert t_s <= tile
    tail_ids = n_tiles[0] + jnp.arange(N_EXPERTS, dtype=I32)
    zero_tiles = jnp.concatenate([jnp.where(tiles_per > 0, tile_end - 1, -1),
                                  jnp.where(tiles_per > 1, tile_end - 2, -1),
                                  jnp.where(tail_ids < n_max, tail_ids, -1)]).astype(I32)
    xs_sorted = _dispatch(offsets, zero_tiles, pairs_p, hp_p, None, n_max * tile, tq=1024)
    xs_sorted = _dispatch(offsets, zero_tiles, pairs_s, hp_s, xs_sorted, n_max * tile, tq=t_s)
    ys_sorted = _experts(tile_expert, n_tiles, xs_sorted, w_exp_gate[l], w_exp_up[l], w_exp_down[l])
    y_p = _combine(offsets, pairs_p, ys_sorted, y2_p, gates_p, tq=256)
    y_s = _combine(offsets, pairs_s, ys_sorted, y2_s, gates_s, tq=t_s)

    from_pos_minor = lambda z: jnp.transpose(z.reshape(b, N_HEADS, HEAD_DIM, s), (0, 3, 1, 2))[None]
    return (y_p.reshape(b, s, D_MODEL), y_s.reshape(bs, ns, D_MODEL),
            state_p[None],
            from_pos_minor(ka_t), from_pos_minor(va_t),
            mk_p.reshape(1, b, N_MEM, N_MEM_HEADS, MEM_HEAD_DIM),
            mv_p.reshape(1, b, N_MEM, N_MEM_HEADS, MEM_HEAD_DIM),
            state_s[None],
            ka_s.reshape(1, bs, ns, N_HEADS, HEAD_DIM), va_s.reshape(1, bs, ns, N_HEADS, HEAD_DIM))
```

```python
import functools

import jax
import jax.numpy as jnp
from jax import lax
from jax.experimental import pallas as pl
from jax.experimental.pallas import tpu as pltpu

F32 = jnp.float32
BF16 = jnp.bfloat16
I32 = jnp.int32

D_MODEL = 1024
HEAD_DIM = 64
N_HEADS = 8
WIDTH = N_HEADS * HEAD_DIM
N_PAIRS = N_HEADS // 2
IN_COLS = 7 * WIDTH
RET_CHUNK = 128
RET_UNROLL = 4
ATT_GROUP = 4
ATT_BLOCK = 128
WIN_STEPS = 128
DILATIONS = (1, 4, 16)
N_MEM = 256
N_MEM_HEADS = 4
MEM_HEAD_DIM = 256
N_GROUPS = 4
N_EXP_PER_GROUP = 8
N_EXPERTS = 32
D_EXPERT = 256
EPS = 1e-6
LANES = 128
EXPERT_LANE0 = 32
MOE_ROW_TILE = 512
EXPERT_SPLIT = 2
ZERO_ROWS = 256
ROUTE_BLOCK = 256
RUN_ALIGN = 8
SORT_ROWS = 2 * ROUTE_BLOCK + N_EXPERTS * RUN_ALIGN
SORT_CHUNKS = SORT_ROWS // RUN_ALIGN
VMEM_LIMIT = 56 * 1024 * 1024

NEG_INF = float("-inf")


def _cparams(n_axes, vmem=VMEM_LIMIT):
    return pltpu.CompilerParams(dimension_semantics=("arbitrary",) * n_axes,
                                vmem_limit_bytes=vmem)


def _dot(a, b):
    return jnp.dot(a, b, preferred_element_type=F32)


def _dot_nt(a, b):
    return lax.dot_general(a, b, (((1,), (1,)), ((), ())), preferred_element_type=F32)


def _dot_tn(a, b):
    return lax.dot_general(a, b, (((0,), (0,)), ((), ())), preferred_element_type=F32)


def _rms(x, g):
    ms = jnp.mean(x * x, axis=-1, keepdims=True)
    return x * lax.rsqrt(ms + EPS) * g


def _mix_proj_kernel(x_ref, g_ref, w_ref, gq_ref, gk_ref,
                     qr_ref, kr_ref, vr_ref, gr_ref, qa_ref, ka_ref, va_ref, *kv_t_refs):
    act = qr_ref.dtype
    h = _rms(x_ref[...], g_ref[...]).astype(BF16)

    def proj(j):
        return _dot(h, w_ref[:, j * WIDTH:(j + 1) * WIDTH])

    def head_norm(z, g):
        is_lo = _pair_masks((z.shape[0], LANES))
        parts = []
        for p in range(N_PAIRS):
            zp = z[:, p * LANES:(p + 1) * LANES]
            parts.append(zp * lax.rsqrt(_segment_mean(zp * zp, is_lo) + EPS))
        return jnp.concatenate(parts, axis=1) * g

    qr_ref[...] = proj(0).astype(act)
    kr_ref[...] = proj(1).astype(act)
    vr_ref[...] = proj(2).astype(act)
    gr_ref[...] = proj(3).astype(act)
    qa_ref[...] = head_norm(proj(4), gq_ref[...]).astype(act)
    ka = head_norm(proj(5), gk_ref[...])
    va = proj(6)
    ka_ref[...] = ka.astype(ka_ref.dtype)
    va_ref[...] = va.astype(va_ref.dtype)
    if kv_t_refs:
        kat_ref, vat_ref = kv_t_refs
        kat_ref[0] = ka.T
        vat_ref[0] = va.T


def _mix_proj(x2d, g_mix, w_in_bf, gq_t, gk_t, tm, act_dtype, seq=None):
    t = x2d.shape[0]
    const = lambda i: (0, 0)
    row = lambda i: (i, 0)
    out_act = jax.ShapeDtypeStruct((t, WIDTH), act_dtype)
    kv_dtype = F32 if seq is None else act_dtype
    out_kv = jax.ShapeDtypeStruct((t, WIDTH), kv_dtype)
    out_specs = [pl.BlockSpec((tm, WIDTH), row)] * 7
    out_shape = [out_act] * 5 + [out_kv] * 2
    if seq is not None:
        per_b = seq // tm
        t_spec = pl.BlockSpec((1, WIDTH, tm), lambda i: (i // per_b, 0, i % per_b))
        out_specs += [t_spec, t_spec]
        out_shape += [jax.ShapeDtypeStruct((t // seq, WIDTH, seq), F32)] * 2
    return pl.pallas_call(
        _mix_proj_kernel,
        grid=(t // tm,),
        in_specs=[
            pl.BlockSpec((tm, D_MODEL), row),
            pl.BlockSpec((1, D_MODEL), const),
            pl.BlockSpec((D_MODEL, IN_COLS), const),
            pl.BlockSpec((1, WIDTH), const),
            pl.BlockSpec((1, WIDTH), const),
        ],
        out_specs=out_specs,
        out_shape=out_shape,
        compiler_params=_cparams(1),
        name="mix_proj",
    )(x2d, g_mix, w_in_bf, gq_t, gk_t)


def _pair_masks(shape):
    lane = lax.broadcasted_iota(I32, shape, len(shape) - 1)
    return lane < HEAD_DIM


def _segment_mean(x, is_lo):
    zero = jnp.zeros_like(x)
    lo = jnp.sum(jnp.where(is_lo, x, zero), axis=-1, keepdims=True)
    hi = jnp.sum(jnp.where(is_lo, zero, x), axis=-1, keepdims=True)
    return jnp.where(is_lo, lo, hi) * (1.0 / HEAD_DIM)


def _group_norm_gate(o, g_r, g_out, is_lo):
    c = o - _segment_mean(o, is_lo)
    y = c * lax.rsqrt(_segment_mean(c * c, is_lo) + EPS) * g_out
    g = g_r.astype(F32)
    return y * (g * (1.0 / (1.0 + jnp.exp(-g))))


def _retention_kernel(lg_ref, gout_ref, q_ref, k_ref, v_ref, g_ref, o_ref, st_ref, kv_ref, s_ref,
                      *, n_chunks):
    c_len = RET_CHUNK
    shape = (c_len, LANES)
    is_lo = _pair_masks(shape)
    row = lax.broadcasted_iota(I32, shape, 0)
    col = lax.broadcasted_iota(I32, shape, 1)
    rel = (row - col).astype(F32)
    lg_lane = lg_ref[0]
    lg0 = lg_lane[:, 0:1]
    lg1 = lg_lane[:, HEAD_DIM:HEAD_DIM + 1]
    scale = HEAD_DIM ** -0.5
    causal = rel >= 0.0
    relp = jnp.maximum(rel, 0.0)
    d0 = jnp.where(causal, jnp.exp(lg0 * relp), 0.0) * scale
    d1 = jnp.where(causal, jnp.exp(lg1 * relp), 0.0) * scale
    rowf = row.astype(F32)
    w_k = jnp.exp(lg_lane * (c_len - 1.0 - rowf)) * scale
    w_q = jnp.exp(lg_lane * (rowf + 1.0))
    lg_row = jnp.where(row < HEAD_DIM, lg0, lg1)
    g_chunk = jnp.exp(lg_row * float(c_len))
    same_head = (row < HEAD_DIM) == (col < HEAD_DIM)
    g_out = gout_ref[0]

    def chunk(c):
        return pl.ds(pl.multiple_of(c * c_len, c_len), c_len)

    def outer(i, carry):
        for u in range(RET_UNROLL):
            c = i * RET_UNROLL + u
            kw = (k_ref[0, chunk(c), :].astype(F32) * w_k).astype(BF16)
            kv_ref[c] = jnp.where(same_head, _dot_tn(kw, v_ref[0, chunk(c), :]), 0.0)
        return carry

    lax.fori_loop(0, n_chunks // RET_UNROLL, outer, 0)

    def recur(c, state):
        kv = kv_ref[c]
        kv_ref[c] = state
        return g_chunk * state + kv

    state = lax.fori_loop(0, n_chunks, recur, jnp.zeros(shape, F32))
    st_ref[0, 0] = state[:HEAD_DIM, :HEAD_DIM]
    st_ref[0, 1] = state[HEAD_DIM:, HEAD_DIM:]

    def score_stage(i, slot):
        for u in range(RET_UNROLL):
            sl = chunk(i * RET_UNROLL + u)
            qc = q_ref[0, sl, :]
            kc = k_ref[0, sl, :]
            zero = jnp.zeros_like(qc)
            s_ref[slot, 2 * u] = (_dot_nt(jnp.where(is_lo, qc, zero), kc) * d0).astype(BF16)
            s_ref[slot, 2 * u + 1] = (_dot_nt(jnp.where(is_lo, zero, qc), kc) * d1).astype(BF16)

    def value_stage(i, slot):
        for u in range(RET_UNROLL):
            c = i * RET_UNROLL + u
            sl = chunk(c)
            qc = q_ref[0, sl, :]
            vc = v_ref[0, sl, :]
            zero = jnp.zeros_like(vc)
            o_in = (_dot(s_ref[slot, 2 * u], jnp.where(is_lo, vc, zero))
                    + _dot(s_ref[slot, 2 * u + 1], jnp.where(is_lo, zero, vc)))
            o_x = _dot(qc, kv_ref[c].astype(BF16)) * w_q
            o_ref[0, sl, :] = _group_norm_gate(o_in + o_x, g_ref[0, sl, :], g_out,
                                               is_lo).astype(BF16)

    n_groups = n_chunks // RET_UNROLL
    score_stage(0, 0)

    def inner(i, carry):
        value_stage(i, i % 2)
        score_stage(i + 1, (i + 1) % 2)
        return carry

    lax.fori_loop(0, n_groups - 1, inner, 0)
    value_stage(n_groups - 1, (n_groups - 1) % 2)


def _retention(q, k, v, g, lg_pairs, gout_pairs):
    b, s, _ = q.shape
    blk = pl.BlockSpec((1, s, LANES), lambda i, p: (i, 0, p))
    tab = pl.BlockSpec((1, 1, LANES), lambda i, p: (p, 0, 0))
    return pl.pallas_call(
        functools.partial(_retention_kernel, n_chunks=s // RET_CHUNK),
        grid=(b, N_PAIRS),
        in_specs=[tab, tab, blk, blk, blk, blk],
        out_specs=[blk, pl.BlockSpec((1, 2, HEAD_DIM, HEAD_DIM), lambda i, p: (i, p, 0, 0))],
        out_shape=[jax.ShapeDtypeStruct((b, s, WIDTH), BF16),
                   jax.ShapeDtypeStruct((b, N_HEADS, HEAD_DIM, HEAD_DIM), F32)],
        scratch_shapes=[pltpu.VMEM((s // RET_CHUNK, LANES, LANES), F32),
                        pltpu.VMEM((2, 2 * RET_UNROLL, RET_CHUNK, RET_CHUNK), BF16)],
        compiler_params=_cparams(2),
        name="retention",
    )(lg_pairs, gout_pairs, q, k, v, g)


def _attn_scores(qb, kb, bias0, bias1, is_lo):
    qb = qb.astype(BF16)
    kb = kb.astype(BF16)
    zq = jnp.zeros_like(qb)
    s0 = _dot_nt(jnp.where(is_lo, qb, zq), kb) + bias0
    s1 = _dot_nt(jnp.where(is_lo, zq, qb), kb) + bias1
    return s0, s1


def _attn_values(s0, s1, vb, is_lo):
    m0 = jnp.max(s0, axis=-1, keepdims=True)
    m1 = jnp.max(s1, axis=-1, keepdims=True)
    p0 = jnp.exp(s0 - m0).astype(BF16)
    p1 = jnp.exp(s1 - m1).astype(BF16)
    is_lo_k = _pair_masks(vb.shape)
    one = jnp.ones_like(vb)
    r0 = _dot(p0, jnp.where(is_lo_k, vb, one).astype(BF16))
    r1 = _dot(p1, jnp.where(is_lo_k, one, vb).astype(BF16))
    acc = jnp.where(is_lo, r0, r1)
    l = pltpu.roll(jnp.where(is_lo, r1, r0), HEAD_DIM, 1)
    m = jnp.where(is_lo, m0, m1)
    return acc, m, l


def _dil_attn_kernel(sl_ref, q_ref, k_ref, v_ref, o_ref,
                     qf_ref, kf_ref, vf_ref, acc_ref, m_ref, l_ref, bias_ref, bias1_ref, s_ref, *,
                     seq):
    blk = ATT_BLOCK
    is_lo = _pair_masks((blk, LANES))
    slope = sl_ref[0]
    slope0 = slope[:, 0:1]
    slope1 = slope[:, HEAD_DIM:HEAD_DIM + 1]

    qf_ref[...] = q_ref[0].astype(F32) * (HEAD_DIM ** -0.5)
    kf_ref[...] = k_ref[0].astype(F32)
    vf_ref[...] = v_ref[0].astype(F32)

    @pl.when(pl.program_id(1) == 0)
    def _():
        i2 = lax.broadcasted_iota(I32, (blk, 2 * blk), 0)
        j2 = lax.broadcasted_iota(I32, (blk, 2 * blk), 1)
        rel2 = blk + i2 - j2
        ok2 = (rel2 >= 0) & (rel2 <= WIN_STEPS)
        i1 = lax.broadcasted_iota(I32, (blk, blk), 0)
        j1 = lax.broadcasted_iota(I32, (blk, blk), 1)
        rel1 = i1 - j1
        ok1 = rel1 >= 0
        for pi, d in enumerate(DILATIONS):
            dist2 = (rel2 * d).astype(F32)
            dist1 = (rel1 * d).astype(F32)
            bias_ref[2 * pi] = jnp.where(ok2, -slope0 * dist2, NEG_INF)
            bias_ref[2 * pi + 1] = jnp.where(ok2, -slope1 * dist2, NEG_INF)
            bias1_ref[2 * pi] = jnp.where(ok1, -slope0 * dist1, NEG_INF)
            bias1_ref[2 * pi + 1] = jnp.where(ok1, -slope1 * dist1, NEG_INF)

    def rows(start, n, d):
        return pl.ds(start, n) if d == 1 else pl.ds(start, n, stride=d)

    def store(pi, sl, res):
        acc, m, l = res
        acc_ref[pi, sl, :] = acc
        m_ref[pi, sl, :] = m
        l_ref[pi, sl, :] = l

    def key_rows(q0, k0, d):
        return rows(q0, blk, d) if k0 is None else rows(k0, 2 * blk, d)

    def score_stage(pi, d, items, slot):
        for u, (q0, k0) in enumerate(items):
            b_ref, nk = (bias1_ref, blk) if k0 is None else (bias_ref, 2 * blk)
            s0, s1 = _attn_scores(qf_ref[rows(q0, blk, d), :], kf_ref[key_rows(q0, k0, d), :],
                                  b_ref[2 * pi], b_ref[2 * pi + 1], is_lo)
            s_ref[slot, 2 * u, :, :nk] = s0
            s_ref[slot, 2 * u + 1, :, :nk] = s1

    def value_stage(pi, d, items, slot):
        for u, (q0, k0) in enumerate(items):
            nk = blk if k0 is None else 2 * blk
            res = _attn_values(s_ref[slot, 2 * u, :, :nk], s_ref[slot, 2 * u + 1, :, :nk],
                               vf_ref[key_rows(q0, k0, d), :], is_lo)
            store(pi, rows(q0, blk, d), res)

    def pipeline(pi, d, n_groups, items_of):
        score_stage(pi, d, items_of(0), 0)

        def body(g, carry):
            value_stage(pi, d, items_of(g), g % 2)
            score_stage(pi, d, items_of(g + 1), (g + 1) % 2)
            return carry

        lax.fori_loop(0, n_groups - 1, body, 0)
        value_stage(pi, d, items_of(n_groups - 1), (n_groups - 1) % 2)

    for pi, d in enumerate(DILATIONS):
        n_blocks = seq // d // blk
        step = blk * d
        firsts = [(r, None) for r in range(d)]
        if n_blocks == 1:
            assert d % ATT_GROUP == 0
            pipeline(pi, d, d // ATT_GROUP,
                     lambda g: [(g * ATT_GROUP + u, None) for u in range(ATT_GROUP)])
            continue
        assert d <= ATT_GROUP
        score_stage(pi, d, firsts, 0)
        value_stage(pi, d, firsts, 0)
        per_group = max(p for p in range(1, ATT_GROUP // d + 1) if (n_blocks - 1) % p == 0)

        def general(g, d=d, step=step, per_group=per_group):
            items = []
            for u in range(per_group):
                base = (1 + g * per_group + u) * step
                if d == 1:
                    base = pl.multiple_of(base, blk)
                items += [(base + r, base + r - step) for r in range(d)]
            return items

        pipeline(pi, d, (n_blocks - 1) // per_group, general)

    def merge(c, carry):
        sl = pl.ds(pl.multiple_of(c * 256, 256), 256)
        m_all = [m_ref[pi, sl, :] for pi in range(3)]
        m_max = jnp.maximum(jnp.maximum(m_all[0], m_all[1]), m_all[2])
        num = jnp.zeros((256, LANES), F32)
        den = jnp.zeros((256, LANES), F32)
        for pi in range(3):
            w = jnp.exp(m_all[pi] - m_max)
            num = num + w * acc_ref[pi, sl, :]
            den = den + w * l_ref[pi, sl, :]
        o_ref[0, sl, :] = (num / den).astype(BF16)
        return carry

    lax.fori_loop(0, seq // 256, merge, 0)


def _dilated_attention(qa, ka, va, slope_pairs):
    b, s, _ = qa.shape
    blk = pl.BlockSpec((1, s, LANES), lambda p, i: (i, 0, p))
    tab = pl.BlockSpec((1, 1, LANES), lambda p, i: (p, 0, 0))
    return pl.pallas_call(
        functools.partial(_dil_attn_kernel, seq=s),
        grid=(N_PAIRS, b),
        in_specs=[tab, blk, blk, blk],
        out_specs=blk,
        out_shape=jax.ShapeDtypeStruct((b, s, WIDTH), BF16),
        scratch_shapes=[
            pltpu.VMEM((s, LANES), F32),
            pltpu.VMEM((s, LANES), F32),
            pltpu.VMEM((s, LANES), F32),
            pltpu.VMEM((3, s, LANES), F32),
            pltpu.VMEM((3, s, LANES), F32),
            pltpu.VMEM((3, s, LANES), F32),
            pltpu.VMEM((6, ATT_BLOCK, 2 * ATT_BLOCK), F32),
            pltpu.VMEM((6, ATT_BLOCK, ATT_BLOCK), F32),
            pltpu.VMEM((2, 2 * ATT_GROUP, ATT_BLOCK, 2 * ATT_BLOCK), F32),
        ],
        compiler_params=_cparams(2),
        name="dilated_attention",
    )(slope_pairs, qa, ka, va)


def _sample_mixer_kernel(lg_ref, gout_ref, slope_ref, qr_ref, kr_ref, vr_ref, gr_ref,
                         qa_ref, ka_ref, va_ref, st_ref, ck_ref, cv_ref,
                         r_ref, a_ref, sto_ref, bias_ref, cnt_ref, *, n_new, w_buf):
    n = n_new
    scale = HEAD_DIM ** -0.5
    lg = lg_ref[...]
    qr = qr_ref[...]
    kr = kr_ref[...]
    vr = vr_ref[...]
    ri = lax.broadcasted_iota(I32, (n, n), 0)
    rj = lax.broadcasted_iota(I32, (n, n), 1)
    rel = (ri - rj).astype(F32)
    rowf = lax.broadcasted_iota(I32, (n, HEAD_DIM), 0).astype(F32)
    outs = []
    for h in range(N_HEADS):
        hs = slice(h * HEAD_DIM, (h + 1) * HEAD_DIM)
        lg_h = lg[:, h * HEAD_DIM:h * HEAD_DIM + 1]
        qh, kh, vh = qr[:, hs], kr[:, hs], vr[:, hs]
        dm = jnp.where(rel >= 0.0, jnp.exp(lg_h * jnp.maximum(rel, 0.0)), 0.0) * scale
        s = _dot_nt(qh, kh) * dm
        s_prev = st_ref[0, h]
        o = _dot(s, vh) + _dot(qh, s_prev) * jnp.exp(lg_h * (rowf + 1.0))
        kw = kh * (jnp.exp(lg_h * (n - 1.0 - rowf)) * scale)
        sto_ref[0, h] = jnp.exp(lg_h * float(n)) * s_prev + _dot_tn(kw, vh)
        outs.append(o)
    o_r = jnp.concatenate(outs, axis=1)
    normed = []
    for p in range(N_PAIRS):
        ps = slice(p * LANES, (p + 1) * LANES)
        is_lo = _pair_masks((n, LANES))
        normed.append(_group_norm_gate(o_r[:, ps], gr_ref[:, ps], gout_ref[:, ps], is_lo))
    r_ref[...] = jnp.concatenate(normed, axis=1)

    n_rows = N_HEADS * n
    n_keys = w_buf + LANES

    @pl.when(pl.program_id(0) == 0)
    def _():
        rr = lax.broadcasted_iota(I32, (n_rows, n_keys), 0)
        cc = lax.broadcasted_iota(I32, (n_rows, n_keys), 1)
        qi = rr % n
        dist = w_buf + qi - cc
        valid = (dist >= 0) & (cc < w_buf + n)
        cnt = jnp.zeros((n_rows, n_keys), F32)
        for d in DILATIONS:
            hit = valid & (dist % d == 0) & (dist <= WIN_STEPS * d)
            cnt = cnt + jnp.where(hit, 1.0, 0.0)
        cnt_ref[...] = cnt
        slope = slope_ref[...]
        srow = jnp.zeros((n_rows, 1), F32)
        r1 = lax.broadcasted_iota(I32, (n_rows, 1), 0)
        for h in range(N_HEADS):
            srow = jnp.where(r1 // n == h, slope[:, h * HEAD_DIM:h * HEAD_DIM + 1], srow)
        bias_ref[...] = jnp.where(cnt > 0.0, -srow * dist.astype(F32), NEG_INF)

    qa = qa_ref[...]
    q_rows = jnp.concatenate([qa] * N_HEADS, axis=0)
    r2 = lax.broadcasted_iota(I32, (n_rows, WIDTH), 0)
    c2 = lax.broadcasted_iota(I32, (n_rows, WIDTH), 1)
    q_blk = jnp.where(r2 // n == c2 // HEAD_DIM, q_rows, 0.0).astype(BF16)
    pad = jnp.zeros((LANES - n, WIDTH), F32)
    k_new = jnp.concatenate([ka_ref[...], pad], axis=0).astype(BF16)
    v_new = jnp.concatenate([va_ref[...], pad], axis=0).astype(BF16)
    s_old = _dot(q_blk, ck_ref[0].astype(BF16))
    s_new = _dot_nt(q_blk, k_new)
    s = jnp.concatenate([s_old, s_new], axis=1) * scale + bias_ref[...]
    m = jnp.max(s, axis=-1, keepdims=True)
    p = cnt_ref[...] * jnp.exp(s - m)
    den = jnp.sum(p, axis=-1, keepdims=True)
    p = p.astype(BF16)
    o_all = _dot_nt(p[:, :w_buf], cv_ref[0].astype(BF16)) + _dot(p[:, w_buf:], v_new)
    o_all = o_all / den
    heads = [o_all[h * n:(h + 1) * n, h * HEAD_DIM:(h + 1) * HEAD_DIM] for h in range(N_HEADS)]
    a_ref[...] = jnp.concatenate(heads, axis=1)


def _sample_mixer(lg_lane, gout, slope_lane, qr, kr, vr, gr, qa, ka, va, state, ck, cv, n_new):
    b = state.shape[0]
    w_buf = ck.shape[2]
    tok = pl.BlockSpec((n_new, WIDTH), lambda i: (i, 0))
    tab = pl.BlockSpec((1, WIDTH), lambda i: (0, 0))
    st = pl.BlockSpec((1, N_HEADS, HEAD_DIM, HEAD_DIM), lambda i: (i, 0, 0, 0))
    cache = pl.BlockSpec((1, WIDTH, w_buf), lambda i: (i, 0, 0))
    n_rows = N_HEADS * n_new
    return pl.pallas_call(
        functools.partial(_sample_mixer_kernel, n_new=n_new, w_buf=w_buf),
        grid=(b,),
        in_specs=[tab, tab, tab, tok, tok, tok, tok, tok, tok, tok, st, cache, cache],
        out_specs=[tok, tok, st],
        out_shape=[jax.ShapeDtypeStruct((b * n_new, WIDTH), F32),
                   jax.ShapeDtypeStruct((b * n_new, WIDTH), F32),
                   jax.ShapeDtypeStruct(state.shape, F32)],
        scratch_shapes=[pltpu.VMEM((n_rows, w_buf + LANES), F32),
                        pltpu.VMEM((n_rows, w_buf + LANES), F32)],
        compiler_params=_cparams(1),
        name="sample_mixer",
    )(lg_lane, gout, slope_lane, qr, kr, vr, gr, qa, ka, va, state, ck, cv)


def _mem_kv_kernel(x_ref, g_ref, wk_ref, wv_ref, gk_ref, k_ref, v_ref, kb_ref, vb_ref):
    h = _rms(x_ref[...], g_ref[...]).astype(BF16)
    k = _dot(h, wk_ref[...])
    gk = gk_ref[...]
    v = _dot(h, wv_ref[...])
    vb_ref[...] = v.astype(BF16)
    for hd in range(N_MEM_HEADS):
        hs = slice(hd * MEM_HEAD_DIM, (hd + 1) * MEM_HEAD_DIM)
        kh = _rms(k[:, hs], gk)
        k_ref[:, hd, :] = kh
        kb_ref[:, hs] = kh.astype(BF16)
        v_ref[:, hd, :] = v[:, hs]


def _mem_kv(mem2d, g_mem, w_mk_bf, w_mv_bf, g_k_mem, tm=256):
    t = mem2d.shape[0]
    const = lambda i: (0, 0)
    row = lambda i: (i, 0)
    out = jax.ShapeDtypeStruct((t, N_MEM_HEADS, MEM_HEAD_DIM), F32)
    return pl.pallas_call(
        _mem_kv_kernel,
        grid=(t // tm,),
        in_specs=[pl.BlockSpec((tm, D_MODEL), row), pl.BlockSpec((1, D_MODEL), const),
                  pl.BlockSpec((D_MODEL, D_MODEL), const), pl.BlockSpec((D_MODEL, D_MODEL), const),
                  pl.BlockSpec((1, MEM_HEAD_DIM), const)],
        out_specs=[pl.BlockSpec((tm, N_MEM_HEADS, MEM_HEAD_DIM), lambda i: (i, 0, 0))] * 2
        + [pl.BlockSpec((tm, D_MODEL), row)] * 2,
        out_shape=[out, out] + [jax.ShapeDtypeStruct((t, D_MODEL), BF16)] * 2,
        compiler_params=_cparams(1),
        name="mem_kv",
    )(mem2d, g_mem, w_mk_bf, w_mv_bf, g_k_mem)


def _cross_router_kernel(x_ref, r_ref, a_ref, wo_ref, gc_ref, wmq_ref, gqm_ref, mk_ref, mv_ref,
                         wmo_ref, gmoe_ref, wrh_ref, wrl_ref, br_ref, cin_ref,
                         y_ref, h_ref, gate_ref, *rest, tq, n_split, sorted_mode):
    if sorted_mode:
        lpos_ref, tab_ref, cnt_ref, carry_ref = rest
    else:
        pair_ref, cnt_ref, carry_ref = rest
    first = (pl.program_id(0) == 0) & (pl.program_id(1) == 0)

    @pl.when(first)
    def _():
        carry_ref[...] = cin_ref[...]

    half = D_MODEL // 2
    gqm = gqm_ref[...]
    n_mem = mk_ref.shape[0]
    head_slices = [slice(hd * MEM_HEAD_DIM, (hd + 1) * MEM_HEAD_DIM) for hd in range(N_MEM_HEADS)]
    mem_k = [[mk_ref[j, :, hs].astype(BF16) for hs in head_slices] for j in range(n_mem)]
    mem_v = [[mv_ref[j, :, hs].astype(BF16) for hs in head_slices] for j in range(n_mem)]

    def attend(q, j):
        heads = []
        for hd, hs in enumerate(head_slices):
            qh = _rms(q[:, hs], gqm).astype(BF16)
            s = _dot_nt(qh, mem_k[j][hd]) * (MEM_HEAD_DIM ** -0.5)
            m = jnp.max(s, axis=-1, keepdims=True)
            p = jnp.exp(s - m)
            p = p / jnp.sum(p, axis=-1, keepdims=True)
            heads.append(_dot(p.astype(BF16), mem_v[j][hd]))
        return jnp.concatenate(heads, axis=1)

    def rows_block(rs):
        ra = jnp.concatenate([r_ref[rs, :], a_ref[rs, :]], axis=1).astype(BF16)
        y1 = x_ref[rs, :] + _dot(ra, wo_ref[...])
        q = _dot(_rms(y1, gc_ref[...]).astype(BF16), wmq_ref[...])
        if n_mem == 1:
            o = attend(q, 0)
        else:
            per = q.shape[0] // n_mem
            o = jnp.concatenate([attend(q[j * per:(j + 1) * per], j) for j in range(n_mem)], axis=0)
        y2 = y1 + _dot(o.astype(BF16), wmo_ref[...])
        y_ref[rs, :] = y2
        hm = _rms(y2, gmoe_ref[...])
        if sorted_mode:
            h_ref[rs, :] = hm.astype(BF16)
        else:
            h_ref[rs, :] = pltpu.pack_elementwise([hm[:, :half], hm[:, half:]], packed_dtype=BF16)
        hi = hm.astype(BF16)
        lo = (hm - hi.astype(F32)).astype(BF16)
        return _dot(hi, wrh_ref[...]) + _dot(lo, wrh_ref[...]) + _dot(hi, wrl_ref[...])

    sub = tq // n_split
    logits = jnp.concatenate([rows_block(pl.ds(i * sub, sub)) for i in range(n_split)], axis=0)
    logits = logits + br_ref[...]

    rb = min(tq, ROUTE_BLOCK)
    lane = lax.broadcasted_iota(I32, (rb, LANES), 1)
    lanef = lane.astype(F32)
    big = float(LANES)
    is_grp = lane < N_GROUPS
    ti = lax.broadcasted_iota(I32, (rb, rb), 0)
    tj = lax.broadcasted_iota(I32, (rb, rb), 1)
    tri = jnp.where(tj < ti, 1.0, 0.0)
    ui = lax.broadcasted_iota(I32, (LANES, LANES), 0)
    uj = lax.broadcasted_iota(I32, (LANES, LANES), 1)
    upper = jnp.where(ui < uj, 1.0, 0.0).astype(BF16)

    def pick(hit, table):
        return jnp.sum(jnp.where(hit, table, 0.0), axis=-1, keepdims=True)

    def lanes_of(vals):
        out = jnp.zeros((rb, LANES), F32)
        for idx, val in enumerate(vals):
            out = jnp.where(lane == idx, val, out)
        return out

    for blk_i in range(tq // rb):
        rs = slice(blk_i * rb, (blk_i + 1) * rb)
        lg = logits[rs]
        gl = jnp.where(is_grp, lg, NEG_INF)
        gmax = jnp.max(gl, axis=-1, keepdims=True)
        gidx = jnp.min(jnp.where(gl == gmax, lanef, big), axis=-1, keepdims=True)
        p_top = 1.0 / jnp.sum(jnp.where(is_grp, jnp.exp(lg - gmax), 0.0), axis=-1, keepdims=True)
        lo = EXPERT_LANE0 + N_EXP_PER_GROUP * gidx
        in_grp = (lanef >= lo) & (lanef < lo + N_EXP_PER_GROUP)
        el = jnp.where(in_grp, lg, NEG_INF)
        v1 = jnp.max(el, axis=-1, keepdims=True)
        i1 = jnp.min(jnp.where(el == v1, lanef, big), axis=-1, keepdims=True)
        el2 = jnp.where(lanef == i1, NEG_INF, el)
        v2 = jnp.max(el2, axis=-1, keepdims=True)
        i2 = jnp.min(jnp.where(el2 == v2, lanef, big), axis=-1, keepdims=True)
        e21 = jnp.exp(v2 - v1)
        w1 = p_top / (1.0 + e21)
        w2 = p_top * e21 / (1.0 + e21)

        hit1 = lanef == i1
        hit2 = lanef == i2
        onehot = jnp.where(hit1 | hit2, 1.0, 0.0)
        if rb >= 16:
            within = _dot(tri.astype(BF16), onehot.astype(BF16))
        else:
            within = _dot(tri, onehot)
        carry = carry_ref[...]
        count = jnp.sum(onehot, axis=0, keepdims=True)

        if sorted_mode:
            eighths = jnp.floor((count + (RUN_ALIGN - 1.0)) * (1.0 / RUN_ALIGN))
            carry_ref[...] = carry + eighths * float(RUN_ALIGN)
            lstart = (_dot(jnp.broadcast_to(eighths, (8, LANES)).astype(BF16), upper)[0:1]
                      * float(RUN_ALIGN))
            lpos1 = pick(hit1, within + lstart)
            lpos2 = pick(hit2, within + lstart)
            gate_ref[rs, :] = lanes_of((w1, w2, lpos1, lpos2))[:, :4]
            lpos_ref[blk_i] = lanes_of((lpos1, lpos2)).T[:8]
            tab_ref[blk_i] = jnp.concatenate(
                [count, lstart, carry, jnp.zeros((5, LANES), F32)], axis=0)
        else:
            carry_ref[...] = carry + count
            e1 = i1 - EXPERT_LANE0
            e2 = i2 - EXPERT_LANE0
            rank1 = pick(hit1, within + carry)
            rank2 = pick(hit2, within + carry)
            gate_ref[rs, :] = lanes_of((w1, w2))[:, :2]
            pair_ref[rs, :] = lanes_of((e1, e2, rank1, rank2))[:, :4].astype(I32)
    cnt_ref[...] = carry_ref[...]


def _cross_router(x2d, r, a, w_o, g_cross, w_mq, g_q_mem, mk, mv, w_mo, g_moe, w_r_hi, w_r_lo, b_r,
                  cnt_in, n_batch, tq, n_split, sorted_mode):
    t = x2d.shape[0]
    rb = min(tq, ROUTE_BLOCK)
    n_rb = tq // rb
    mem_per_step = max(1, tq // (t // n_batch))
    n_batch = n_batch // mem_per_step
    assert mem_per_step == 1 or n_split == 1
    per_b = t // n_batch // tq
    row = lambda i, j: (i * per_b + j, 0)
    const = lambda i, j: (0, 0)
    mem = pl.BlockSpec((mem_per_step, N_MEM, D_MODEL), lambda i, j: (i, 0, 0))
    wspec = pl.BlockSpec((D_MODEL, D_MODEL), const)
    vec = pl.BlockSpec((1, D_MODEL), const)
    rspec = pl.BlockSpec((D_MODEL, LANES), const)
    blk3 = lambda i, j: (i * per_b + j, 0, 0)
    if sorted_mode:
        out_specs = [pl.BlockSpec((tq, D_MODEL), row), pl.BlockSpec((tq, D_MODEL), row),
                     pl.BlockSpec((tq, 4), row), pl.BlockSpec((n_rb, 8, rb), blk3),
                     pl.BlockSpec((n_rb, 8, LANES), blk3)]
        out_shape = [jax.ShapeDtypeStruct((t, D_MODEL), F32),
                     jax.ShapeDtypeStruct((t, D_MODEL), BF16),
                     jax.ShapeDtypeStruct((t, 4), F32),
                     jax.ShapeDtypeStruct((t // rb, 8, rb), F32),
                     jax.ShapeDtypeStruct((t // rb, 8, LANES), F32)]
    else:
        out_specs = [pl.BlockSpec((tq, D_MODEL), row), pl.BlockSpec((tq, D_MODEL // 2), row),
                     pl.BlockSpec((tq, 2), row), pl.BlockSpec((tq, 4), row)]
        out_shape = [jax.ShapeDtypeStruct((t, D_MODEL), F32),
                     jax.ShapeDtypeStruct((t, D_MODEL // 2), jnp.uint32),
                     jax.ShapeDtypeStruct((t, 2), F32),
                     jax.ShapeDtypeStruct((t, 4), I32)]
    return pl.pallas_call(
        functools.partial(_cross_router_kernel, tq=tq, n_split=n_split, sorted_mode=sorted_mode),
        grid=(n_batch, per_b),
        in_specs=[pl.BlockSpec((tq, D_MODEL), row), pl.BlockSpec((tq, WIDTH), row),
                  pl.BlockSpec((tq, WIDTH), row), wspec, vec, wspec,
                  pl.BlockSpec((1, MEM_HEAD_DIM), const), mem, mem, wspec, vec,
                  rspec, rspec, pl.BlockSpec((1, LANES), const),
                  pl.BlockSpec((1, LANES), const)],
        out_specs=out_specs + [pl.BlockSpec((1, LANES), const)],
        out_shape=out_shape + [jax.ShapeDtypeStruct((1, LANES), F32)],
        scratch_shapes=[pltpu.VMEM((1, LANES), F32)],
        compiler_params=_cparams(2),
        name="cross_router",
    )(x2d, r, a, w_o, g_cross, w_mq, g_q_mem, mk, mv, w_mo, g_moe, w_r_hi, w_r_lo, b_r, cnt_in)


def _row_copy(src, src_row, dst, dst_row, sem):
    return pltpu.make_async_copy(src.at[pl.ds(src_row, 1)], dst.at[pl.ds(dst_row, 1)], sem)


def _pair_slot(off_ref, pairs_ref, j, kk):
    return off_ref[pairs_ref[0, 0, 4 * j + kk]] + pairs_ref[0, 0, 4 * j + 2 + kk]


def _dispatch_kernel(off_ref, pairs_ref, hp_ref, xs_in_ref, xs_ref, sem, *, tq):
    del xs_in_ref

    def start(j, carry):
        for kk in range(2):
            _row_copy(hp_ref, j, xs_ref, _pair_slot(off_ref, pairs_ref, j, kk), sem).start()
        return carry

    def wait(j, carry):
        for kk in range(2):
            _row_copy(hp_ref, j, xs_ref, _pair_slot(off_ref, pairs_ref, j, kk), sem).wait()
        return carry

    lax.fori_loop(0, tq, start, 0, unroll=8)
    lax.fori_loop(0, tq, wait, 0, unroll=8)


def _dispatch(offsets, pairs, hp, xs, tq):
    t = hp.shape[0]
    return pl.pallas_call(
        functools.partial(_dispatch_kernel, tq=tq),
        grid_spec=pltpu.PrefetchScalarGridSpec(
            num_scalar_prefetch=1, grid=(t // tq,),
            in_specs=[pl.BlockSpec((1, 1, 4 * tq), lambda i, off: (i, 0, 0),
                                   memory_space=pltpu.SMEM),
                      pl.BlockSpec((tq, D_MODEL // 2), lambda i, off: (i, 0)),
                      pl.BlockSpec(memory_space=pl.ANY)],
            out_specs=pl.BlockSpec(memory_space=pl.ANY),
            scratch_shapes=[pltpu.SemaphoreType.DMA(())]),
        out_shape=jax.ShapeDtypeStruct(xs.shape, xs.dtype),
        input_output_aliases={3: 0},
        compiler_params=pltpu.CompilerParams(dimension_semantics=("arbitrary",),
                                             has_side_effects=True),
        name="moe_dispatch",
    )(offsets, pairs.reshape(t // tq, 1, 4 * tq), hp, xs)


def _unpack_pair(words):
    lo = pltpu.unpack_elementwise(words, index=0, packed_dtype=BF16, unpacked_dtype=F32)
    hi = pltpu.unpack_elementwise(words, index=1, packed_dtype=BF16, unpacked_dtype=F32)
    return lo, hi


def _expert_kernel(te_ref, nt_ref, xs_ref, wg_ref, wu_ref, wd_ref, ys_ref):
    del te_ref
    half = D_MODEL // 2

    @pl.when(pl.program_id(0) < nt_ref[0])
    def _():
        wg = wg_ref[0].astype(BF16)
        wu = wu_ref[0].astype(BF16)
        wd = wd_ref[0].astype(BF16)
        sub = xs_ref.shape[0] // EXPERT_SPLIT
        for i in range(EXPERT_SPLIT):
            rs = pl.ds(i * sub, sub)
            lo, hi = _unpack_pair(xs_ref[rs, :])
            lo = lo.astype(BF16)
            hi = hi.astype(BF16)
            g = _dot(lo, wg[:half]) + _dot(hi, wg[half:])
            u = _dot(lo, wu[:half]) + _dot(hi, wu[half:])
            hid = (g * (1.0 / (1.0 + jnp.exp(-g))) * u).astype(BF16)
            y = _dot(hid, wd)
            ys_ref[rs, :] = pltpu.pack_elementwise([y[:, :half], y[:, half:]], packed_dtype=BF16)

    @pl.when(pl.program_id(0) >= nt_ref[0])
    def _():
        ys_ref[...] = jnp.zeros_like(ys_ref)


def _experts(tile_expert, n_tiles, xs, w_gate, w_up, w_down):
    rows = xs.shape[0]
    n_max = rows // MOE_ROW_TILE

    def xmap(i, te, nt):
        return (jnp.maximum(jnp.minimum(i, nt[0] - 1), 0), 0)

    def wmap(i, te, nt):
        return (te[i], 0, 0)

    return pl.pallas_call(
        _expert_kernel,
        grid_spec=pltpu.PrefetchScalarGridSpec(
            num_scalar_prefetch=2,
            grid=(n_max,),
            in_specs=[pl.BlockSpec((MOE_ROW_TILE, D_MODEL // 2), xmap),
                      pl.BlockSpec((1, D_MODEL, D_EXPERT), wmap),
                      pl.BlockSpec((1, D_MODEL, D_EXPERT), wmap),
                      pl.BlockSpec((1, D_EXPERT, D_MODEL), wmap)],
            out_specs=pl.BlockSpec((MOE_ROW_TILE, D_MODEL // 2), lambda i, te, nt: (i, 0)),
        ),
        out_shape=jax.ShapeDtypeStruct(xs.shape, xs.dtype),
        compiler_params=_cparams(1),
        name="moe_experts",
    )(tile_expert, n_tiles, xs, w_gate, w_up, w_down)


def _combine_kernel(off_ref, pairs_ref, pairs_next_ref, ys_ref, y_ref, gate_ref, o_ref,
                    buf_ref, sem, *, tq):
    step = pl.program_id(0)
    n_steps = pl.num_programs(0)

    def gather(p_ref, slot, wait):
        def body(j, carry):
            for kk in range(2):
                cp = pltpu.make_async_copy(ys_ref.at[pl.ds(_pair_slot(off_ref, p_ref, j, kk), 1)],
                                           buf_ref.at[slot, kk, pl.ds(j, 1)], sem.at[slot])
                cp.wait() if wait else cp.start()
            return carry
        lax.fori_loop(0, tq, body, 0, unroll=8)

    @pl.when(step == 0)
    def _():
        gather(pairs_ref, 0, wait=False)

    @pl.when(step + 1 < n_steps)
    def _():
        gather(pairs_next_ref, (step + 1) % 2, wait=False)

    slot = step % 2
    gather(pairs_ref, slot, wait=True)
    half = D_MODEL // 2
    gates = gate_ref[...]
    w1 = gates[:, 0:1]
    w2 = gates[:, 1:2]
    lo1, hi1 = _unpack_pair(buf_ref[slot, 0])
    lo2, hi2 = _unpack_pair(buf_ref[slot, 1])
    y = y_ref[...]
    o_ref[:, :half] = y[:, :half] + (w1 * lo1 + w2 * lo2)
    o_ref[:, half:] = y[:, half:] + (w1 * hi1 + w2 * hi2)


def _combine(offsets, pairs, ys, y2, gates, tq):
    t = y2.shape[0]
    n = t // tq
    row = lambda i, off: (i, 0)
    smem = lambda f: pl.BlockSpec((1, 1, 4 * tq), f, memory_space=pltpu.SMEM)
    pairs3 = pairs.reshape(n, 1, 4 * tq)
    return pl.pallas_call(
        functools.partial(_combine_kernel, tq=tq),
        grid_spec=pltpu.PrefetchScalarGridSpec(
            num_scalar_prefetch=1, grid=(n,),
            in_specs=[smem(lambda i, off: (i, 0, 0)),
                      smem(lambda i, off: (jnp.minimum(i + 1, n - 1), 0, 0)),
                      pl.BlockSpec(memory_space=pl.ANY),
                      pl.BlockSpec((tq, D_MODEL), row), pl.BlockSpec((tq, 2), row)],
            out_specs=pl.BlockSpec((tq, D_MODEL), row),
            scratch_shapes=[pltpu.VMEM((2, 2, tq, D_MODEL // 2), jnp.uint32),
                            pltpu.SemaphoreType.DMA((2,))]),
        out_shape=jax.ShapeDtypeStruct((t, D_MODEL), F32),
        compiler_params=_cparams(1),
        name="moe_combine",
    )(offsets, pairs3, pairs3, ys, y2, gates)


def _for_each_chunk(tot_ref, rows_ref, blk, fn):
    base = blk * SORT_CHUNKS

    def body(k, carry):
        fn(pl.multiple_of(k * RUN_ALIGN, RUN_ALIGN), pl.multiple_of(rows_ref[base + k], RUN_ALIGN))
        return carry

    lax.fori_loop(0, tot_ref[blk], body, 0)


def _sorted_dispatch_kernel(tot_ref, rows_ref, zt_ref, nt_ref, h_ref, lpos_ref, xs_ref,
                            stage_ref, zero_ref, sem, zsem, *, n_zero, n_tiles_max):
    step = pl.program_id(0)
    n_steps = pl.num_programs(0)

    @pl.when(step == 0)
    def _():
        zero_ref[...] = jnp.zeros_like(zero_ref)

        def fill(granule):
            row = pl.multiple_of(granule * ZERO_ROWS, ZERO_ROWS)
            return pltpu.make_async_copy(zero_ref, xs_ref.at[pl.ds(row, ZERO_ROWS)], zsem)

        def tail(do):
            def body(tile, carry):
                do(fill(tile))
                return carry
            lax.fori_loop(nt_ref[0], n_tiles_max, body, 0)

        for i in range(n_zero):
            pl.when(zt_ref[i] >= 0)(fill(jnp.maximum(zt_ref[i], 0)).start)
        tail(lambda cp: cp.start())
        for i in range(n_zero):
            pl.when(zt_ref[i] >= 0)(fill(jnp.maximum(zt_ref[i], 0)).wait)
        tail(lambda cp: cp.wait())

    slot = step % 2
    half = D_MODEL // 2
    rb = h_ref.shape[0]
    lpos = lpos_ref[0]
    jrow = lax.broadcasted_iota(I32, (SORT_ROWS, rb), 0).astype(F32)
    perm = jnp.where((jrow == lpos[0:1]) | (jrow == lpos[1:2]), 1.0, 0.0).astype(BF16)
    x = _dot(perm, h_ref[...])
    stage_ref[slot] = pltpu.pack_elementwise([x[:, :half], x[:, half:]], packed_dtype=BF16)

    def copy(s):
        return lambda l, g: pltpu.make_async_copy(stage_ref.at[s, pl.ds(l, RUN_ALIGN)],
                                                  xs_ref.at[pl.ds(g, RUN_ALIGN)], sem.at[s])

    @pl.when(step > 0)
    def _():
        _for_each_chunk(tot_ref, rows_ref,step - 1,
                        lambda l, g: copy(1 - slot)(l, g).wait())

    _for_each_chunk(tot_ref, rows_ref,step, lambda l, g: copy(slot)(l, g).start())

    @pl.when(step == n_steps - 1)
    def _():
        _for_each_chunk(tot_ref, rows_ref,step, lambda l, g: copy(slot)(l, g).wait())


def _sorted_dispatch(chunk_total, chunk_rows, zero_tiles, n_tiles, h_bf, lpos_t, n_rows):
    n_blocks, _, rb = lpos_t.shape
    return pl.pallas_call(
        functools.partial(_sorted_dispatch_kernel, n_zero=zero_tiles.shape[0],
                          n_tiles_max=n_rows // ZERO_ROWS),
        grid_spec=pltpu.PrefetchScalarGridSpec(
            num_scalar_prefetch=4, grid=(n_blocks,),
            in_specs=[pl.BlockSpec((rb, D_MODEL), lambda i, *_: (i, 0)),
                      pl.BlockSpec((1, 8, rb), lambda i, *_: (i, 0, 0))],
            out_specs=pl.BlockSpec(memory_space=pl.ANY),
            scratch_shapes=[pltpu.VMEM((2, SORT_ROWS, D_MODEL // 2), jnp.uint32),
                            pltpu.VMEM((ZERO_ROWS, D_MODEL // 2), jnp.uint32),
                            pltpu.SemaphoreType.DMA((2,)), pltpu.SemaphoreType.DMA(())]),
        out_shape=jax.ShapeDtypeStruct((n_rows, D_MODEL // 2), jnp.uint32),
        compiler_params=pltpu.CompilerParams(dimension_semantics=("arbitrary",),
                                             has_side_effects=True, vmem_limit_bytes=VMEM_LIMIT),
        name="moe_sorted_dispatch",
    )(chunk_total, chunk_rows, zero_tiles, n_tiles, h_bf, lpos_t)


def _sorted_combine_kernel(tot_ref, rows_ref, ys_ref, y_ref, gate_ref, o_ref, buf_ref, sem):
    step = pl.program_id(0)
    n_steps = pl.num_programs(0)

    def gather(blk, s, wait):
        def fn(l, g):
            cp = pltpu.make_async_copy(ys_ref.at[pl.ds(g, RUN_ALIGN)],
                                       buf_ref.at[s, pl.ds(l, RUN_ALIGN)], sem.at[s])
            cp.wait() if wait else cp.start()
        _for_each_chunk(tot_ref, rows_ref,blk, fn)

    @pl.when(step == 0)
    def _():
        buf_ref[...] = jnp.zeros_like(buf_ref)
        gather(0, 0, wait=False)

    @pl.when(step + 1 < n_steps)
    def _():
        gather(step + 1, (step + 1) % 2, wait=False)

    slot = step % 2
    gather(step, slot, wait=True)
    half = D_MODEL // 2
    rb = y_ref.shape[0]
    gates = gate_ref[...]
    jcol = lax.broadcasted_iota(I32, (rb, SORT_ROWS), 1).astype(F32)
    pick1 = jnp.where(jcol == gates[:, 2:3], 1.0, 0.0).astype(BF16)
    pick2 = jnp.where(jcol == gates[:, 3:4], 1.0, 0.0).astype(BF16)
    lo, hi = _unpack_pair(buf_ref[slot])
    lo = lo.astype(BF16)
    hi = hi.astype(BF16)
    w1 = gates[:, 0:1]
    w2 = gates[:, 1:2]
    y = y_ref[...]
    o_ref[:, :half] = y[:, :half] + (w1 * _dot(pick1, lo) + w2 * _dot(pick2, lo))
    o_ref[:, half:] = y[:, half:] + (w1 * _dot(pick1, hi) + w2 * _dot(pick2, hi))


def _sorted_combine(chunk_total, chunk_rows, ys, y2, gates, rb):
    t = y2.shape[0]
    row = lambda i, *_: (i, 0)
    return pl.pallas_call(
        _sorted_combine_kernel,
        grid_spec=pltpu.PrefetchScalarGridSpec(
            num_scalar_prefetch=2, grid=(t // rb,),
            in_specs=[pl.BlockSpec(memory_space=pl.ANY),
                      pl.BlockSpec((rb, D_MODEL), row), pl.BlockSpec((rb, 4), row)],
            out_specs=pl.BlockSpec((rb, D_MODEL), row),
            scratch_shapes=[pltpu.VMEM((2, SORT_ROWS, D_MODEL // 2), jnp.uint32),
                            pltpu.SemaphoreType.DMA((2,))]),
        out_shape=jax.ShapeDtypeStruct((t, D_MODEL), F32),
        compiler_params=_cparams(1),
        name="moe_sorted_combine",
    )(chunk_total, chunk_rows, ys, y2, gates)


def _lane_table(per_head):
    lanes = jnp.repeat(per_head.astype(F32), HEAD_DIM)
    return lanes.reshape(N_PAIRS, 1, LANES), lanes.reshape(1, WIDTH)


def kernel(x_prompt, x_sample, mem_prompt, state_ret, cache_win_k, cache_win_v, cache_mem_k,
           cache_mem_v, g_mix, w_in, g_ret_out, g_q_att, g_k_att, w_o, g_cross, g_mem, w_mq, w_mk,
           w_mv, g_q_mem, g_k_mem, w_mo, g_moe, w_router_group, b_router_group, w_router_expert,
           b_router_expert, w_exp_gate, w_exp_up, w_exp_down):
    depth = g_mix.shape[0]
    assert depth == 1
    b, s, _ = x_prompt.shape
    bs, ns, _ = x_sample.shape
    t_p, t_s = b * s, bs * ns
    l = 0

    heads = jnp.arange(N_HEADS, dtype=F32)
    lg_pairs, lg_lane = _lane_table(jnp.log(1.0 - 2.0 ** (-5.0 - heads)))
    slope_pairs, slope_lane = _lane_table(2.0 ** (-8.0 * (heads + 1.0) / N_HEADS))
    gout_lane = g_ret_out[l].reshape(1, WIDTH)
    gout_pairs = g_ret_out[l].reshape(N_PAIRS, 1, LANES)
    gq_t = jnp.tile(g_q_att[l], N_HEADS).reshape(1, WIDTH)
    gk_t = jnp.tile(g_k_att[l], N_HEADS).reshape(1, WIDTH)
    vec = lambda v: v.reshape(1, -1)
    w_in_bf = w_in[l].astype(BF16)
    w_o_bf, w_mq_bf, w_mk_bf = w_o[l].astype(BF16), w_mq[l].astype(BF16), w_mk[l].astype(BF16)
    w_mv_bf, w_mo_bf = w_mv[l].astype(BF16), w_mo[l].astype(BF16)
    gap = EXPERT_LANE0 - N_GROUPS
    tail = LANES - EXPERT_LANE0 - N_EXPERTS
    w_r = jnp.concatenate(
        [w_router_group[l], jnp.zeros((D_MODEL, gap), F32),
         jnp.moveaxis(w_router_expert[l], 0, 1).reshape(D_MODEL, N_EXPERTS),
         jnp.zeros((D_MODEL, tail), F32)], axis=1)
    b_r = jnp.concatenate([b_router_group[l], jnp.zeros((gap,), F32),
                           b_router_expert[l].reshape(-1), jnp.zeros((tail,), F32)]).reshape(1, LANES)
    w_r_hi = w_r.astype(BF16)
    w_r_lo = (w_r - w_r_hi.astype(F32)).astype(BF16)

    xp = x_prompt.reshape(t_p, D_MODEL)
    qr, kr, vr, gr, qa, ka, va, ka_t, va_t = _mix_proj(xp, vec(g_mix[l]), w_in_bf, gq_t, gk_t,
                                                       tm=512, act_dtype=BF16, seq=s)
    shp = lambda z: z.reshape(b, s, WIDTH)
    r_p, state_p = _retention(shp(qr), shp(kr), shp(vr), shp(gr), lg_pairs, gout_pairs)
    a_p = _dilated_attention(shp(qa), shp(ka), shp(va), slope_pairs)
    mk_p, mv_p, mk_bf, mv_bf = _mem_kv(mem_prompt.reshape(b * N_MEM, D_MODEL), vec(g_mem[l]), w_mk_bf, w_mv_bf,
                          vec(g_k_mem[l]))
    zero_cnt = jnp.zeros((1, LANES), F32)
    y2_p, h_p, gates_p, lpos_p, tabs_p, cnt_p = _cross_router(
        xp, r_p.reshape(t_p, WIDTH), a_p.reshape(t_p, WIDTH), w_o_bf, vec(g_cross[l]), w_mq_bf,
        vec(g_q_mem[l]), mk_bf.reshape(b, N_MEM, D_MODEL), mv_bf.reshape(b, N_MEM, D_MODEL), w_mo_bf,
        vec(g_moe[l]), w_r_hi, w_r_lo, b_r, zero_cnt, n_batch=b, tq=1024, n_split=4,
        sorted_mode=True)

    xs_ = x_sample.reshape(t_s, D_MODEL)
    qr, kr, vr, gr, qa, ka_s, va_s = _mix_proj(xs_, vec(g_mix[l]), w_in_bf, gq_t, gk_t, tm=t_s,
                                               act_dtype=F32)
    pos_minor = lambda c: jnp.transpose(c, (0, 2, 3, 1)).reshape(bs, WIDTH, c.shape[1])
    r_s, a_s, state_s = _sample_mixer(lg_lane, gout_lane, slope_lane, qr, kr, vr, gr, qa, ka_s, va_s,
                                      state_ret[l], pos_minor(cache_win_k[l]),
                                      pos_minor(cache_win_v[l]), n_new=ns)
    y2_s, hp_s, gates_s, pairs_s, cnt_all = _cross_router(
        xs_, r_s, a_s, w_o_bf, vec(g_cross[l]), w_mq_bf, vec(g_q_mem[l]),
        cache_mem_k[l].reshape(bs, N_MEM, D_MODEL), cache_mem_v[l].reshape(bs, N_MEM, D_MODEL),
        w_mo_bf, vec(g_moe[l]), w_r_hi, w_r_lo, b_r, cnt_p, n_batch=bs, tq=min(bs, 8) * ns, n_split=1,
        sorted_mode=False)

    tile = MOE_ROW_TILE
    lanes_e = slice(EXPERT_LANE0, EXPERT_LANE0 + N_EXPERTS)
    counts = cnt_all[0, lanes_e].astype(I32)
    tiles_per = (counts + tile - 1) // tile
    tile_end = jnp.cumsum(tiles_per)
    offsets = (tile_end - tiles_per) * tile
    n_tiles = tile_end[-1:]
    n_blocks = t_p // ROUTE_BLOCK
    n_max = (2 * (t_p + t_s) + n_blocks * N_EXPERTS * (RUN_ALIGN - 1)) // tile + N_EXPERTS
    tile_ids = jnp.minimum(jnp.arange(n_max, dtype=I32), n_tiles[0] - 1)
    tile_expert = jnp.sum((tile_end[None, :] <= tile_ids[:, None]).astype(I32), axis=1)
    seg_end = tile_end * tile
    first_gran = (offsets + cnt_p[0, lanes_e].astype(I32)) // ZERO_ROWS
    n_gran = (t_s + tile) // ZERO_ROWS + 1
    gran = first_gran[None, :] + jnp.arange(n_gran, dtype=I32)[:, None]
    zero_tiles = jnp.where(gran * ZERO_ROWS < seg_end[None, :], gran, -1).reshape(-1)
    first_tail_gran = n_tiles * (tile // ZERO_ROWS)
    run_chunks = (tabs_p[:, 0, lanes_e].astype(I32) + RUN_ALIGN - 1) // RUN_ALIGN
    run_end = jnp.cumsum(run_chunks, axis=1)
    chunk_total = run_end[:, -1]
    chunk_id = jnp.arange(SORT_CHUNKS, dtype=I32)
    owner = (chunk_id[None, :, None] >= run_end[:, None, :]).astype(I32).sum(axis=2)
    owner_hot = owner[:, :, None] == jnp.arange(N_EXPERTS, dtype=I32)[None, None, :]
    run_row0 = offsets[None, :] + tabs_p[:, 2, lanes_e].astype(I32) - (run_end - run_chunks) * RUN_ALIGN
    chunk_rows = (jnp.where(owner_hot, run_row0[:, None, :], 0).sum(axis=2)
                  + chunk_id[None, :] * RUN_ALIGN).reshape(-1)
    xs_sorted = _sorted_dispatch(chunk_total, chunk_rows, zero_tiles, first_tail_gran, h_p, lpos_p,
                                 n_max * tile)
    xs_sorted = _dispatch(offsets, pairs_s, hp_s, xs_sorted, tq=t_s)
    ys_sorted = _experts(tile_expert, n_tiles, xs_sorted, w_exp_gate[l], w_exp_up[l], w_exp_down[l])
    y_p = _sorted_combine(chunk_total, chunk_rows, ys_sorted, y2_p, gates_p, rb=ROUTE_BLOCK)
    y_s = _combine(offsets, pairs_s, ys_sorted, y2_s, gates_s, tq=t_s)

    from_pos_minor = lambda z: jnp.transpose(z.reshape(b, N_HEADS, HEAD_DIM, s), (0, 3, 1, 2))[None]
    return (y_p.reshape(b, s, D_MODEL), y_s.reshape(bs, ns, D_MODEL),
            state_p[None],
            from_pos_minor(ka_t), from_pos_minor(va_t),
            mk_p.reshape(1, b, N_MEM, N_MEM_HEADS, MEM_HEAD_DIM),
            mv_p.reshape(1, b, N_MEM, N_MEM_HEADS, MEM_HEAD_DIM),
            state_s[None],
            ka_s.reshape(1, bs, ns, N_HEADS, HEAD_DIM), va_s.reshape(1, bs, ns, N_HEADS, HEAD_DIM))
```

```python
import functools

import jax
import jax.numpy as jnp
from jax import lax
from jax.experimental import pallas as pl
from jax.experimental.pallas import tpu as pltpu

F32 = jnp.float32
BF16 = jnp.bfloat16
I32 = jnp.int32

D_MODEL = 1024
HEAD_DIM = 64
N_HEADS = 8
WIDTH = N_HEADS * HEAD_DIM
N_PAIRS = N_HEADS // 2
IN_COLS = 7 * WIDTH
RET_CHUNK = 128
RET_UNROLL = 8
ATT_GROUP = 4
ATT_BLOCK = 128
WIN_STEPS = 128
DILATIONS = (1, 4, 16)
N_MEM = 256
N_MEM_HEADS = 4
MEM_HEAD_DIM = 256
N_GROUPS = 4
N_EXP_PER_GROUP = 8
N_EXPERTS = 32
D_EXPERT = 256
EPS = 1e-6
LANES = 128
EXPERT_LANE0 = 32
MOE_ROW_TILE = 512
EXPERT_SPLIT = 2
ZERO_ROWS = 256
ROUTE_BLOCK = 256
RUN_ALIGN = 8
SORT_ROWS = 2 * ROUTE_BLOCK + N_EXPERTS * RUN_ALIGN
SORT_CHUNKS = SORT_ROWS // RUN_ALIGN
VMEM_LIMIT = 56 * 1024 * 1024

NEG_INF = float("-inf")


def _cparams(n_axes, vmem=VMEM_LIMIT):
    return pltpu.CompilerParams(dimension_semantics=("arbitrary",) * n_axes,
                                vmem_limit_bytes=vmem)


def _dot(a, b):
    return jnp.dot(a, b, preferred_element_type=F32)


def _dot_nt(a, b):
    return lax.dot_general(a, b, (((1,), (1,)), ((), ())), preferred_element_type=F32)


def _dot_tn(a, b):
    return lax.dot_general(a, b, (((0,), (0,)), ((), ())), preferred_element_type=F32)


def _rms(x, g):
    ms = jnp.mean(x * x, axis=-1, keepdims=True)
    return x * lax.rsqrt(ms + EPS) * g


def _mix_proj_kernel(x_ref, g_ref, w_ref, gq_ref, gk_ref,
                     qr_ref, kr_ref, vr_ref, gr_ref, qa_ref, ka_ref, va_ref, *kv_t_refs):
    act = qr_ref.dtype
    h = _rms(x_ref[...], g_ref[...]).astype(BF16)

    def proj(j):
        return _dot(h, w_ref[:, j * WIDTH:(j + 1) * WIDTH])

    def head_norm(z, g):
        is_lo = _pair_masks((z.shape[0], LANES))
        parts = []
        for p in range(N_PAIRS):
            zp = z[:, p * LANES:(p + 1) * LANES]
            parts.append(zp * lax.rsqrt(_segment_mean(zp * zp, is_lo) + EPS))
        return jnp.concatenate(parts, axis=1) * g

    qr_ref[...] = proj(0).astype(act)
    kr_ref[...] = proj(1).astype(act)
    vr_ref[...] = proj(2).astype(act)
    gr_ref[...] = proj(3).astype(act)
    qa_ref[...] = head_norm(proj(4), gq_ref[...]).astype(act)
    ka = head_norm(proj(5), gk_ref[...])
    va = proj(6)
    ka_ref[...] = ka.astype(ka_ref.dtype)
    va_ref[...] = va.astype(va_ref.dtype)
    if kv_t_refs:
        kat_ref, vat_ref = kv_t_refs
        kat_ref[0] = ka.T
        vat_ref[0] = va.T


def _mix_proj(x2d, g_mix, w_in_bf, gq_t, gk_t, tm, act_dtype, seq=None):
    t = x2d.shape[0]
    const = lambda i: (0, 0)
    row = lambda i: (i, 0)
    out_act = jax.ShapeDtypeStruct((t, WIDTH), act_dtype)
    kv_dtype = F32 if seq is None else act_dtype
    out_kv = jax.ShapeDtypeStruct((t, WIDTH), kv_dtype)
    out_specs = [pl.BlockSpec((tm, WIDTH), row)] * 7
    out_shape = [out_act] * 5 + [out_kv] * 2
    if seq is not None:
        per_b = seq // tm
        t_spec = pl.BlockSpec((1, WIDTH, tm), lambda i: (i // per_b, 0, i % per_b))
        out_specs += [t_spec, t_spec]
        out_shape += [jax.ShapeDtypeStruct((t // seq, WIDTH, seq), F32)] * 2
    return pl.pallas_call(
        _mix_proj_kernel,
        grid=(t // tm,),
        in_specs=[
            pl.BlockSpec((tm, D_MODEL), row),
            pl.BlockSpec((1, D_MODEL), const),
            pl.BlockSpec((D_MODEL, IN_COLS), const),
            pl.BlockSpec((1, WIDTH), const),
            pl.BlockSpec((1, WIDTH), const),
        ],
        out_specs=out_specs,
        out_shape=out_shape,
        compiler_params=_cparams(1),
        name="mix_proj",
    )(x2d, g_mix, w_in_bf, gq_t, gk_t)


def _pair_masks(shape):
    lane = lax.broadcasted_iota(I32, shape, len(shape) - 1)
    return lane < HEAD_DIM


def _segment_mean(x, is_lo):
    zero = jnp.zeros_like(x)
    lo = jnp.sum(jnp.where(is_lo, x, zero), axis=-1, keepdims=True)
    hi = jnp.sum(jnp.where(is_lo, zero, x), axis=-1, keepdims=True)
    return jnp.where(is_lo, lo, hi) * (1.0 / HEAD_DIM)


def _group_norm_gate(o, g_r, g_out, is_lo):
    c = o - _segment_mean(o, is_lo)
    y = c * lax.rsqrt(_segment_mean(c * c, is_lo) + EPS) * g_out
    g = g_r.astype(F32)
    return y * (g * (1.0 / (1.0 + jnp.exp(-g))))


def _retention_kernel(lg_ref, gout_ref, q_ref, k_ref, v_ref, g_ref, o_ref, st_ref, kv_ref, s_ref,
                      *, n_chunks):
    c_len = RET_CHUNK
    shape = (c_len, LANES)
    is_lo = _pair_masks(shape)
    row = lax.broadcasted_iota(I32, shape, 0)
    col = lax.broadcasted_iota(I32, shape, 1)
    rel = (row - col).astype(F32)
    lg_lane = lg_ref[0]
    lg0 = lg_lane[:, 0:1]
    lg1 = lg_lane[:, HEAD_DIM:HEAD_DIM + 1]
    scale = HEAD_DIM ** -0.5
    causal = rel >= 0.0
    relp = jnp.maximum(rel, 0.0)
    d0 = jnp.where(causal, jnp.exp(lg0 * relp), 0.0) * scale
    d1 = jnp.where(causal, jnp.exp(lg1 * relp), 0.0) * scale
    rowf = row.astype(F32)
    w_k = jnp.exp(lg_lane * (c_len - 1.0 - rowf)) * scale
    w_q = jnp.exp(lg_lane * (rowf + 1.0))
    lg_row = jnp.where(row < HEAD_DIM, lg0, lg1)
    g_chunk = jnp.exp(lg_row * float(c_len))
    same_head = (row < HEAD_DIM) == (col < HEAD_DIM)
    g_out = gout_ref[0]

    def chunk(c):
        return pl.ds(pl.multiple_of(c * c_len, c_len), c_len)

    def outer(i, carry):
        for u in range(RET_UNROLL):
            c = i * RET_UNROLL + u
            kw = (k_ref[0, chunk(c), :].astype(F32) * w_k).astype(BF16)
            kv_ref[c] = jnp.where(same_head, _dot_tn(kw, v_ref[0, chunk(c), :]), 0.0)
        return carry

    lax.fori_loop(0, n_chunks // RET_UNROLL, outer, 0)

    def recur(c, state):
        kv = kv_ref[c]
        kv_ref[c] = state
        return g_chunk * state + kv

    state = lax.fori_loop(0, n_chunks, recur, jnp.zeros(shape, F32))
    st_ref[0, 0] = state[:HEAD_DIM, :HEAD_DIM]
    st_ref[0, 1] = state[HEAD_DIM:, HEAD_DIM:]

    def score_stage(i, slot):
        for u in range(RET_UNROLL):
            sl = chunk(i * RET_UNROLL + u)
            qc = q_ref[0, sl, :]
            kc = k_ref[0, sl, :]
            zero = jnp.zeros_like(qc)
            s_ref[slot, 2 * u] = (_dot_nt(jnp.where(is_lo, qc, zero), kc) * d0).astype(BF16)
            s_ref[slot, 2 * u + 1] = (_dot_nt(jnp.where(is_lo, zero, qc), kc) * d1).astype(BF16)

    def value_stage(i, slot):
        for u in range(RET_UNROLL):
            c = i * RET_UNROLL + u
            sl = chunk(c)
            qc = q_ref[0, sl, :]
            vc = v_ref[0, sl, :]
            zero = jnp.zeros_like(vc)
            o_in = (_dot(s_ref[slot, 2 * u], jnp.where(is_lo, vc, zero))
                    + _dot(s_ref[slot, 2 * u + 1], jnp.where(is_lo, zero, vc)))
            o_x = _dot(qc, kv_ref[c].astype(BF16)) * w_q
            o_ref[0, sl, :] = _group_norm_gate(o_in + o_x, g_ref[0, sl, :], g_out,
                                               is_lo).astype(BF16)

    n_groups = n_chunks // RET_UNROLL
    score_stage(0, 0)

    def inner(i, carry):
        value_stage(i, i % 2)
        score_stage(i + 1, (i + 1) % 2)
        return carry

    lax.fori_loop(0, n_groups - 1, inner, 0)
    value_stage(n_groups - 1, (n_groups - 1) % 2)


def _retention(q, k, v, g, lg_pairs, gout_pairs):
    b, s, _ = q.shape
    blk = pl.BlockSpec((1, s, LANES), lambda i, p: (i, 0, p))
    tab = pl.BlockSpec((1, 1, LANES), lambda i, p: (p, 0, 0))
    return pl.pallas_call(
        functools.partial(_retention_kernel, n_chunks=s // RET_CHUNK),
        grid=(b, N_PAIRS),
        in_specs=[tab, tab, blk, blk, blk, blk],
        out_specs=[blk, pl.BlockSpec((1, 2, HEAD_DIM, HEAD_DIM), lambda i, p: (i, p, 0, 0))],
        out_shape=[jax.ShapeDtypeStruct((b, s, WIDTH), BF16),
                   jax.ShapeDtypeStruct((b, N_HEADS, HEAD_DIM, HEAD_DIM), F32)],
        scratch_shapes=[pltpu.VMEM((s // RET_CHUNK, LANES, LANES), F32),
                        pltpu.VMEM((2, 2 * RET_UNROLL, RET_CHUNK, RET_CHUNK), BF16)],
        compiler_params=_cparams(2),
        name="retention",
    )(lg_pairs, gout_pairs, q, k, v, g)


def _attn_scores(qb, kb, bias0, bias1, is_lo):
    qb = qb.astype(BF16)
    kb = kb.astype(BF16)
    zq = jnp.zeros_like(qb)
    s0 = _dot_nt(jnp.where(is_lo, qb, zq), kb) + bias0
    s1 = _dot_nt(jnp.where(is_lo, zq, qb), kb) + bias1
    return s0, s1


def _attn_values(s0, s1, vb, is_lo):
    m0 = jnp.max(s0, axis=-1, keepdims=True)
    m1 = jnp.max(s1, axis=-1, keepdims=True)
    p0 = jnp.exp(s0 - m0).astype(BF16)
    p1 = jnp.exp(s1 - m1).astype(BF16)
    is_lo_k = _pair_masks(vb.shape)
    one = jnp.ones_like(vb)
    r0 = _dot(p0, jnp.where(is_lo_k, vb, one).astype(BF16))
    r1 = _dot(p1, jnp.where(is_lo_k, one, vb).astype(BF16))
    acc = jnp.where(is_lo, r0, r1)
    l = pltpu.roll(jnp.where(is_lo, r1, r0), HEAD_DIM, 1)
    m = jnp.where(is_lo, m0, m1)
    return acc, m, l


def _dil_attn_kernel(sl_ref, q_ref, k_ref, v_ref, o_ref,
                     qf_ref, kf_ref, vf_ref, acc_ref, m_ref, l_ref, bias_ref, bias1_ref, s_ref, *,
                     seq):
    blk = ATT_BLOCK
    is_lo = _pair_masks((blk, LANES))
    slope = sl_ref[0]
    slope0 = slope[:, 0:1]
    slope1 = slope[:, HEAD_DIM:HEAD_DIM + 1]

    qf_ref[...] = q_ref[0].astype(F32) * (HEAD_DIM ** -0.5)
    kf_ref[...] = k_ref[0].astype(F32)
    vf_ref[...] = v_ref[0].astype(F32)

    @pl.when(pl.program_id(1) == 0)
    def _():
        i2 = lax.broadcasted_iota(I32, (blk, 2 * blk), 0)
        j2 = lax.broadcasted_iota(I32, (blk, 2 * blk), 1)
        rel2 = blk + i2 - j2
        ok2 = (rel2 >= 0) & (rel2 <= WIN_STEPS)
        i1 = lax.broadcasted_iota(I32, (blk, blk), 0)
        j1 = lax.broadcasted_iota(I32, (blk, blk), 1)
        rel1 = i1 - j1
        ok1 = rel1 >= 0
        for pi, d in enumerate(DILATIONS):
            dist2 = (rel2 * d).astype(F32)
            dist1 = (rel1 * d).astype(F32)
            bias_ref[2 * pi] = jnp.where(ok2, -slope0 * dist2, NEG_INF)
            bias_ref[2 * pi + 1] = jnp.where(ok2, -slope1 * dist2, NEG_INF)
            bias1_ref[2 * pi] = jnp.where(ok1, -slope0 * dist1, NEG_INF)
            bias1_ref[2 * pi + 1] = jnp.where(ok1, -slope1 * dist1, NEG_INF)

    def rows(start, n, d):
        return pl.ds(start, n) if d == 1 else pl.ds(start, n, stride=d)

    def store(pi, sl, res):
        acc, m, l = res
        acc_ref[pi, sl, :] = acc
        m_ref[pi, sl, :] = m
        l_ref[pi, sl, :] = l

    def key_rows(q0, k0, d):
        return rows(q0, blk, d) if k0 is None else rows(k0, 2 * blk, d)

    def score_stage(pi, d, items, slot):
        for u, (q0, k0) in enumerate(items):
            b_ref, nk = (bias1_ref, blk) if k0 is None else (bias_ref, 2 * blk)
            s0, s1 = _attn_scores(qf_ref[rows(q0, blk, d), :], kf_ref[key_rows(q0, k0, d), :],
                                  b_ref[2 * pi], b_ref[2 * pi + 1], is_lo)
            s_ref[slot, 2 * u, :, :nk] = s0
            s_ref[slot, 2 * u + 1, :, :nk] = s1

    def value_stage(pi, d, items, slot):
        for u, (q0, k0) in enumerate(items):
            nk = blk if k0 is None else 2 * blk
            res = _attn_values(s_ref[slot, 2 * u, :, :nk], s_ref[slot, 2 * u + 1, :, :nk],
                               vf_ref[key_rows(q0, k0, d), :], is_lo)
            store(pi, rows(q0, blk, d), res)

    def pipeline(pi, d, n_groups, items_of):
        score_stage(pi, d, items_of(0), 0)

        def body(g, carry):
            value_stage(pi, d, items_of(g), g % 2)
            score_stage(pi, d, items_of(g + 1), (g + 1) % 2)
            return carry

        lax.fori_loop(0, n_groups - 1, body, 0)
        value_stage(pi, d, items_of(n_groups - 1), (n_groups - 1) % 2)

    for pi, d in enumerate(DILATIONS):
        n_blocks = seq // d // blk
        step = blk * d
        firsts = [(r, None) for r in range(d)]
        if n_blocks == 1:
            assert d % ATT_GROUP == 0
            pipeline(pi, d, d // ATT_GROUP,
                     lambda g: [(g * ATT_GROUP + u, None) for u in range(ATT_GROUP)])
            continue
        assert d <= ATT_GROUP
        score_stage(pi, d, firsts, 0)
        value_stage(pi, d, firsts, 0)
        per_group = max(p for p in range(1, ATT_GROUP // d + 1) if (n_blocks - 1) % p == 0)

        def general(g, d=d, step=step, per_group=per_group):
            items = []
            for u in range(per_group):
                base = (1 + g * per_group + u) * step
                if d == 1:
                    base = pl.multiple_of(base, blk)
                items += [(base + r, base + r - step) for r in range(d)]
            return items

        pipeline(pi, d, (n_blocks - 1) // per_group, general)

    def merge(c, carry):
        sl = pl.ds(pl.multiple_of(c * 256, 256), 256)
        m_all = [m_ref[pi, sl, :] for pi in range(3)]
        m_max = jnp.maximum(jnp.maximum(m_all[0], m_all[1]), m_all[2])
        num = jnp.zeros((256, LANES), F32)
        den = jnp.zeros((256, LANES), F32)
        for pi in range(3):
            w = jnp.exp(m_all[pi] - m_max)
            num = num + w * acc_ref[pi, sl, :]
            den = den + w * l_ref[pi, sl, :]
        o_ref[0, sl, :] = (num / den).astype(BF16)
        return carry

    lax.fori_loop(0, seq // 256, merge, 0)


def _dilated_attention(qa, ka, va, slope_pairs):
    b, s, _ = qa.shape
    blk = pl.BlockSpec((1, s, LANES), lambda p, i: (i, 0, p))
    tab = pl.BlockSpec((1, 1, LANES), lambda p, i: (p, 0, 0))
    return pl.pallas_call(
        functools.partial(_dil_attn_kernel, seq=s),
        grid=(N_PAIRS, b),
        in_specs=[tab, blk, blk, blk],
        out_specs=blk,
        out_shape=jax.ShapeDtypeStruct((b, s, WIDTH), BF16),
        scratch_shapes=[
            pltpu.VMEM((s, LANES), F32),
            pltpu.VMEM((s, LANES), F32),
            pltpu.VMEM((s, LANES), F32),
            pltpu.VMEM((3, s, LANES), F32),
            pltpu.VMEM((3, s, LANES), F32),
            pltpu.VMEM((3, s, LANES), F32),
            pltpu.VMEM((6, ATT_BLOCK, 2 * ATT_BLOCK), F32),
            pltpu.VMEM((6, ATT_BLOCK, ATT_BLOCK), F32),
            pltpu.VMEM((2, 2 * ATT_GROUP, ATT_BLOCK, 2 * ATT_BLOCK), F32),
        ],
        compiler_params=_cparams(2),
        name="dilated_attention",
    )(slope_pairs, qa, ka, va)


def _sample_mixer_kernel(lg_ref, gout_ref, slope_ref, qr_ref, kr_ref, vr_ref, gr_ref,
                         qa_ref, ka_ref, va_ref, st_ref, ck_ref, cv_ref,
                         r_ref, a_ref, sto_ref, bias_ref, cnt_ref, *, n_new, w_buf):
    n = n_new
    scale = HEAD_DIM ** -0.5
    lg = lg_ref[...]
    qr = qr_ref[...]
    kr = kr_ref[...]
    vr = vr_ref[...]
    ri = lax.broadcasted_iota(I32, (n, n), 0)
    rj = lax.broadcasted_iota(I32, (n, n), 1)
    rel = (ri - rj).astype(F32)
    rowf = lax.broadcasted_iota(I32, (n, HEAD_DIM), 0).astype(F32)
    outs = []
    for h in range(N_HEADS):
        hs = slice(h * HEAD_DIM, (h + 1) * HEAD_DIM)
        lg_h = lg[:, h * HEAD_DIM:h * HEAD_DIM + 1]
        qh, kh, vh = qr[:, hs], kr[:, hs], vr[:, hs]
        dm = jnp.where(rel >= 0.0, jnp.exp(lg_h * jnp.maximum(rel, 0.0)), 0.0) * scale
        s = _dot_nt(qh, kh) * dm
        s_prev = st_ref[0, h]
        o = _dot(s, vh) + _dot(qh, s_prev) * jnp.exp(lg_h * (rowf + 1.0))
        kw = kh * (jnp.exp(lg_h * (n - 1.0 - rowf)) * scale)
        sto_ref[0, h] = jnp.exp(lg_h * float(n)) * s_prev + _dot_tn(kw, vh)
        outs.append(o)
    o_r = jnp.concatenate(outs, axis=1)
    normed = []
    for p in range(N_PAIRS):
        ps = slice(p * LANES, (p + 1) * LANES)
        is_lo = _pair_masks((n, LANES))
        normed.append(_group_norm_gate(o_r[:, ps], gr_ref[:, ps], gout_ref[:, ps], is_lo))
    r_ref[...] = jnp.concatenate(normed, axis=1)

    n_rows = N_HEADS * n
    n_keys = w_buf + LANES

    @pl.when(pl.program_id(0) == 0)
    def _():
        rr = lax.broadcasted_iota(I32, (n_rows, n_keys), 0)
        cc = lax.broadcasted_iota(I32, (n_rows, n_keys), 1)
        qi = rr % n
        dist = w_buf + qi - cc
        valid = (dist >= 0) & (cc < w_buf + n)
        cnt = jnp.zeros((n_rows, n_keys), F32)
        for d in DILATIONS:
            hit = valid & (dist % d == 0) & (dist <= WIN_STEPS * d)
            cnt = cnt + jnp.where(hit, 1.0, 0.0)
        cnt_ref[...] = cnt
        slope = slope_ref[...]
        srow = jnp.zeros((n_rows, 1), F32)
        r1 = lax.broadcasted_iota(I32, (n_rows, 1), 0)
        for h in range(N_HEADS):
            srow = jnp.where(r1 // n == h, slope[:, h * HEAD_DIM:h * HEAD_DIM + 1], srow)
        bias_ref[...] = jnp.where(cnt > 0.0, -srow * dist.astype(F32), NEG_INF)

    qa = qa_ref[...]
    q_rows = jnp.concatenate([qa] * N_HEADS, axis=0)
    r2 = lax.broadcasted_iota(I32, (n_rows, WIDTH), 0)
    c2 = lax.broadcasted_iota(I32, (n_rows, WIDTH), 1)
    q_blk = jnp.where(r2 // n == c2 // HEAD_DIM, q_rows, 0.0).astype(BF16)
    pad = jnp.zeros((LANES - n, WIDTH), F32)
    k_new = jnp.concatenate([ka_ref[...], pad], axis=0).astype(BF16)
    v_new = jnp.concatenate([va_ref[...], pad], axis=0).astype(BF16)
    s_old = _dot(q_blk, ck_ref[0].astype(BF16))
    s_new = _dot_nt(q_blk, k_new)
    s = jnp.concatenate([s_old, s_new], axis=1) * scale + bias_ref[...]
    m = jnp.max(s, axis=-1, keepdims=True)
    p = cnt_ref[...] * jnp.exp(s - m)
    den = jnp.sum(p, axis=-1, keepdims=True)
    p = p.astype(BF16)
    o_all = _dot_nt(p[:, :w_buf], cv_ref[0].astype(BF16)) + _dot(p[:, w_buf:], v_new)
    o_all = o_all / den
    heads = [o_all[h * n:(h + 1) * n, h * HEAD_DIM:(h + 1) * HEAD_DIM] for h in range(N_HEADS)]
    a_ref[...] = jnp.concatenate(heads, axis=1)


def _sample_mixer(lg_lane, gout, slope_lane, qr, kr, vr, gr, qa, ka, va, state, ck, cv, n_new):
    b = state.shape[0]
    w_buf = ck.shape[2]
    tok = pl.BlockSpec((n_new, WIDTH), lambda i: (i, 0))
    tab = pl.BlockSpec((1, WIDTH), lambda i: (0, 0))
    st = pl.BlockSpec((1, N_HEADS, HEAD_DIM, HEAD_DIM), lambda i: (i, 0, 0, 0))
    cache = pl.BlockSpec((1, WIDTH, w_buf), lambda i: (i, 0, 0))
    n_rows = N_HEADS * n_new
    return pl.pallas_call(
        functools.partial(_sample_mixer_kernel, n_new=n_new, w_buf=w_buf),
        grid=(b,),
        in_specs=[tab, tab, tab, tok, tok, tok, tok, tok, tok, tok, st, cache, cache],
        out_specs=[tok, tok, st],
        out_shape=[jax.ShapeDtypeStruct((b * n_new, WIDTH), F32),
                   jax.ShapeDtypeStruct((b * n_new, WIDTH), F32),
                   jax.ShapeDtypeStruct(state.shape, F32)],
        scratch_shapes=[pltpu.VMEM((n_rows, w_buf + LANES), F32),
                        pltpu.VMEM((n_rows, w_buf + LANES), F32)],
        compiler_params=_cparams(1),
        name="sample_mixer",
    )(lg_lane, gout, slope_lane, qr, kr, vr, gr, qa, ka, va, state, ck, cv)


def _mem_kv_kernel(x_ref, g_ref, wk_ref, wv_ref, gk_ref, k_ref, v_ref, kb_ref, vb_ref):
    h = _rms(x_ref[...], g_ref[...]).astype(BF16)
    k = _dot(h, wk_ref[...])
    gk = gk_ref[...]
    v = _dot(h, wv_ref[...])
    vb_ref[...] = v.astype(BF16)
    for hd in range(N_MEM_HEADS):
        hs = slice(hd * MEM_HEAD_DIM, (hd + 1) * MEM_HEAD_DIM)
        kh = _rms(k[:, hs], gk)
        k_ref[:, hd, :] = kh
        kb_ref[:, hs] = kh.astype(BF16)
        v_ref[:, hd, :] = v[:, hs]


def _mem_kv(mem2d, g_mem, w_mk_bf, w_mv_bf, g_k_mem, tm=256):
    t = mem2d.shape[0]
    const = lambda i: (0, 0)
    row = lambda i: (i, 0)
    out = jax.ShapeDtypeStruct((t, N_MEM_HEADS, MEM_HEAD_DIM), F32)
    return pl.pallas_call(
        _mem_kv_kernel,
        grid=(t // tm,),
        in_specs=[pl.BlockSpec((tm, D_MODEL), row), pl.BlockSpec((1, D_MODEL), const),
                  pl.BlockSpec((D_MODEL, D_MODEL), const), pl.BlockSpec((D_MODEL, D_MODEL), const),
                  pl.BlockSpec((1, MEM_HEAD_DIM), const)],
        out_specs=[pl.BlockSpec((tm, N_MEM_HEADS, MEM_HEAD_DIM), lambda i: (i, 0, 0))] * 2
        + [pl.BlockSpec((tm, D_MODEL), row)] * 2,
        out_shape=[out, out] + [jax.ShapeDtypeStruct((t, D_MODEL), BF16)] * 2,
        compiler_params=_cparams(1),
        name="mem_kv",
    )(mem2d, g_mem, w_mk_bf, w_mv_bf, g_k_mem)


def _cross_router_kernel(x_ref, r_ref, a_ref, wo_ref, gc_ref, wmq_ref, gqm_ref, mk_ref, mv_ref,
                         wmo_ref, gmoe_ref, wrh_ref, wrl_ref, br_ref, cin_ref,
                         y_ref, h_ref, gate_ref, *rest, tq, n_split, sorted_mode, native_mem):
    n_scratch = 3 if native_mem else 1
    if native_mem:
        carry_ref, mem_ref, msem = rest[-n_scratch:]
    else:
        carry_ref, = rest[-n_scratch:]
    if sorted_mode:
        lpos_ref, tab_ref, cnt_ref = rest[:-n_scratch]
    else:
        pair_ref, cnt_ref = rest[:-n_scratch]
    first = (pl.program_id(0) == 0) & (pl.program_id(1) == 0)

    @pl.when(first)
    def _():
        carry_ref[...] = cin_ref[...]

    half = D_MODEL // 2
    gqm = gqm_ref[...]
    head_slices = [slice(hd * MEM_HEAD_DIM, (hd + 1) * MEM_HEAD_DIM) for hd in range(N_MEM_HEADS)]
    if native_mem:
        n_mem = mem_ref.shape[1]
        b0 = pl.program_id(0) * n_mem
        copies = [pltpu.make_async_copy(src.at[b0 + j, :, hd, :], mem_ref.at[kv, j, hd], msem)
                  for kv, src in enumerate((mk_ref, mv_ref))
                  for j in range(n_mem) for hd in range(N_MEM_HEADS)]
        for cp in copies:
            cp.start()
        for cp in copies:
            cp.wait()
        mem_k = [[mem_ref[0, j, hd].astype(BF16) for hd in range(N_MEM_HEADS)] for j in range(n_mem)]
        mem_v = [[mem_ref[1, j, hd].astype(BF16) for hd in range(N_MEM_HEADS)] for j in range(n_mem)]
    else:
        n_mem = mk_ref.shape[0]
        mem_k = [[mk_ref[j, :, hs].astype(BF16) for hs in head_slices] for j in range(n_mem)]
        mem_v = [[mv_ref[j, :, hs].astype(BF16) for hs in head_slices] for j in range(n_mem)]

    def attend(q, j):
        heads = []
        for hd, hs in enumerate(head_slices):
            qh = _rms(q[:, hs], gqm).astype(BF16)
            s = _dot_nt(qh, mem_k[j][hd]) * (MEM_HEAD_DIM ** -0.5)
            m = jnp.max(s, axis=-1, keepdims=True)
            p = jnp.exp(s - m)
            p = p / jnp.sum(p, axis=-1, keepdims=True)
            heads.append(_dot(p.astype(BF16), mem_v[j][hd]))
        return jnp.concatenate(heads, axis=1)

    def rows_block(rs):
        ra = jnp.concatenate([r_ref[rs, :], a_ref[rs, :]], axis=1).astype(BF16)
        y1 = x_ref[rs, :] + _dot(ra, wo_ref[...])
        q = _dot(_rms(y1, gc_ref[...]).astype(BF16), wmq_ref[...])
        if n_mem == 1:
            o = attend(q, 0)
        else:
            per = q.shape[0] // n_mem
            o = jnp.concatenate([attend(q[j * per:(j + 1) * per], j) for j in range(n_mem)], axis=0)
        y2 = y1 + _dot(o.astype(BF16), wmo_ref[...])
        y_ref[rs, :] = y2
        hm = _rms(y2, gmoe_ref[...])
        if sorted_mode:
            h_ref[rs, :] = hm.astype(BF16)
        else:
            h_ref[rs, :] = pltpu.pack_elementwise([hm[:, :half], hm[:, half:]], packed_dtype=BF16)
        hi = hm.astype(BF16)
        lo = (hm - hi.astype(F32)).astype(BF16)
        return _dot(hi, wrh_ref[...]) + _dot(lo, wrh_ref[...]) + _dot(hi, wrl_ref[...])

    sub = tq // n_split
    logits = jnp.concatenate([rows_block(pl.ds(i * sub, sub)) for i in range(n_split)], axis=0)
    logits = logits + br_ref[...]

    rb = min(tq, ROUTE_BLOCK)
    lane = lax.broadcasted_iota(I32, (rb, LANES), 1)
    lanef = lane.astype(F32)
    big = float(LANES)
    is_grp = lane < N_GROUPS
    ti = lax.broadcasted_iota(I32, (rb, rb), 0)
    tj = lax.broadcasted_iota(I32, (rb, rb), 1)
    tri = jnp.where(tj < ti, 1.0, 0.0)
    ui = lax.broadcasted_iota(I32, (LANES, LANES), 0)
    uj = lax.broadcasted_iota(I32, (LANES, LANES), 1)
    upper = jnp.where(ui < uj, 1.0, 0.0).astype(BF16)

    def pick(hit, table):
        return jnp.sum(jnp.where(hit, table, 0.0), axis=-1, keepdims=True)

    def lanes_of(vals):
        out = jnp.zeros((rb, LANES), F32)
        for idx, val in enumerate(vals):
            out = jnp.where(lane == idx, val, out)
        return out

    for blk_i in range(tq // rb):
        rs = slice(blk_i * rb, (blk_i + 1) * rb)
        lg = logits[rs]
        gl = jnp.where(is_grp, lg, NEG_INF)
        gmax = jnp.max(gl, axis=-1, keepdims=True)
        gidx = jnp.min(jnp.where(gl == gmax, lanef, big), axis=-1, keepdims=True)
        p_top = 1.0 / jnp.sum(jnp.where(is_grp, jnp.exp(lg - gmax), 0.0), axis=-1, keepdims=True)
        lo = EXPERT_LANE0 + N_EXP_PER_GROUP * gidx
        in_grp = (lanef >= lo) & (lanef < lo + N_EXP_PER_GROUP)
        el = jnp.where(in_grp, lg, NEG_INF)
        v1 = jnp.max(el, axis=-1, keepdims=True)
        i1 = jnp.min(jnp.where(el == v1, lanef, big), axis=-1, keepdims=True)
        el2 = jnp.where(lanef == i1, NEG_INF, el)
        v2 = jnp.max(el2, axis=-1, keepdims=True)
        i2 = jnp.min(jnp.where(el2 == v2, lanef, big), axis=-1, keepdims=True)
        e21 = jnp.exp(v2 - v1)
        w1 = p_top / (1.0 + e21)
        w2 = p_top * e21 / (1.0 + e21)

        hit1 = lanef == i1
        hit2 = lanef == i2
        onehot = jnp.where(hit1 | hit2, 1.0, 0.0)
        if rb >= 16:
            within = _dot(tri.astype(BF16), onehot.astype(BF16))
        else:
            within = _dot(tri, onehot)
        carry = carry_ref[...]
        count = jnp.sum(onehot, axis=0, keepdims=True)

        if sorted_mode:
            eighths = jnp.floor((count + (RUN_ALIGN - 1.0)) * (1.0 / RUN_ALIGN))
            carry_ref[...] = carry + eighths * float(RUN_ALIGN)
            lstart = (_dot(jnp.broadcast_to(eighths, (8, LANES)).astype(BF16), upper)[0:1]
                      * float(RUN_ALIGN))
            lpos1 = pick(hit1, within + lstart)
            lpos2 = pick(hit2, within + lstart)
            gate_ref[rs, :] = lanes_of((w1, w2, lpos1, lpos2))[:, :4]
            lpos_ref[blk_i] = lanes_of((lpos1, lpos2)).T[:8]
            tab_ref[blk_i] = jnp.concatenate(
                [count, lstart, carry, jnp.zeros((5, LANES), F32)], axis=0)
        else:
            carry_ref[...] = carry + count
            e1 = i1 - EXPERT_LANE0
            e2 = i2 - EXPERT_LANE0
            rank1 = pick(hit1, within + carry)
            rank2 = pick(hit2, within + carry)
            gate_ref[rs, :] = lanes_of((w1, w2))[:, :2]
            pair_ref[rs, :] = lanes_of((e1, e2, rank1, rank2))[:, :4].astype(I32)
    cnt_ref[...] = carry_ref[...]


def _cross_router(x2d, r, a, w_o, g_cross, w_mq, g_q_mem, mk, mv, w_mo, g_moe, w_r_hi, w_r_lo, b_r,
                  cnt_in, n_batch, tq, n_split, sorted_mode):
    t = x2d.shape[0]
    rb = min(tq, ROUTE_BLOCK)
    n_rb = tq // rb
    mem_per_step = max(1, tq // (t // n_batch))
    n_batch = n_batch // mem_per_step
    assert mem_per_step == 1 or n_split == 1
    native_mem = mk.ndim == 4
    scratch = [pltpu.VMEM((1, LANES), F32)]
    if native_mem:
        scratch += [pltpu.VMEM((2, mem_per_step, N_MEM_HEADS, N_MEM, MEM_HEAD_DIM), F32),
                    pltpu.SemaphoreType.DMA(())]
    per_b = t // n_batch // tq
    row = lambda i, j: (i * per_b + j, 0)
    const = lambda i, j: (0, 0)
    if native_mem:
        mem = pl.BlockSpec(memory_space=pl.ANY)
    else:
        mem = pl.BlockSpec((mem_per_step, N_MEM, D_MODEL), lambda i, j: (i, 0, 0))
    wspec = pl.BlockSpec((D_MODEL, D_MODEL), const)
    vec = pl.BlockSpec((1, D_MODEL), const)
    rspec = pl.BlockSpec((D_MODEL, LANES), const)
    blk3 = lambda i, j: (i * per_b + j, 0, 0)
    if sorted_mode:
        out_specs = [pl.BlockSpec((tq, D_MODEL), row), pl.BlockSpec((tq, D_MODEL), row),
                     pl.BlockSpec((tq, 4), row), pl.BlockSpec((n_rb, 8, rb), blk3),
                     pl.BlockSpec((n_rb, 8, LANES), blk3)]
        out_shape = [jax.ShapeDtypeStruct((t, D_MODEL), F32),
                     jax.ShapeDtypeStruct((t, D_MODEL), BF16),
                     jax.ShapeDtypeStruct((t, 4), F32),
                     jax.ShapeDtypeStruct((t // rb, 8, rb), F32),
                     jax.ShapeDtypeStruct((t // rb, 8, LANES), F32)]
    else:
        out_specs = [pl.BlockSpec((tq, D_MODEL), row), pl.BlockSpec((tq, D_MODEL // 2), row),
                     pl.BlockSpec((tq, 2), row), pl.BlockSpec((tq, 4), row)]
        out_shape = [jax.ShapeDtypeStruct((t, D_MODEL), F32),
                     jax.ShapeDtypeStruct((t, D_MODEL // 2), jnp.uint32),
                     jax.ShapeDtypeStruct((t, 2), F32),
                     jax.ShapeDtypeStruct((t, 4), I32)]
    return pl.pallas_call(
        functools.partial(_cross_router_kernel, tq=tq, n_split=n_split, sorted_mode=sorted_mode,
                          native_mem=native_mem),
        grid=(n_batch, per_b),
        in_specs=[pl.BlockSpec((tq, D_MODEL), row), pl.BlockSpec((tq, WIDTH), row),
                  pl.BlockSpec((tq, WIDTH), row), wspec, vec, wspec,
                  pl.BlockSpec((1, MEM_HEAD_DIM), const), mem, mem, wspec, vec,
                  rspec, rspec, pl.BlockSpec((1, LANES), const),
                  pl.BlockSpec((1, LANES), const)],
        out_specs=out_specs + [pl.BlockSpec((1, LANES), const)],
        out_shape=out_shape + [jax.ShapeDtypeStruct((1, LANES), F32)],
        scratch_shapes=scratch,
        compiler_params=_cparams(2),
        name="cross_router",
    )(x2d, r, a, w_o, g_cross, w_mq, g_q_mem, mk, mv, w_mo, g_moe, w_r_hi, w_r_lo, b_r, cnt_in)


def _row_copy(src, src_row, dst, dst_row, sem):
    return pltpu.make_async_copy(src.at[pl.ds(src_row, 1)], dst.at[pl.ds(dst_row, 1)], sem)


def _pair_slot(off_ref, pairs_ref, j, kk):
    return off_ref[pairs_ref[0, 0, 4 * j + kk]] + pairs_ref[0, 0, 4 * j + 2 + kk]


def _dispatch_kernel(off_ref, pairs_ref, hp_ref, xs_in_ref, xs_ref, sem, *, tq):
    del xs_in_ref

    def start(j, carry):
        for kk in range(2):
            _row_copy(hp_ref, j, xs_ref, _pair_slot(off_ref, pairs_ref, j, kk), sem).start()
        return carry

    def wait(j, carry):
        for kk in range(2):
            _row_copy(hp_ref, j, xs_ref, _pair_slot(off_ref, pairs_ref, j, kk), sem).wait()
        return carry

    lax.fori_loop(0, tq, start, 0, unroll=8)
    lax.fori_loop(0, tq, wait, 0, unroll=8)


def _dispatch(offsets, pairs, hp, xs, tq):
    t = hp.shape[0]
    return pl.pallas_call(
        functools.partial(_dispatch_kernel, tq=tq),
        grid_spec=pltpu.PrefetchScalarGridSpec(
            num_scalar_prefetch=1, grid=(t // tq,),
            in_specs=[pl.BlockSpec((1, 1, 4 * tq), lambda i, off: (i, 0, 0),
                                   memory_space=pltpu.SMEM),
                      pl.BlockSpec((tq, D_MODEL // 2), lambda i, off: (i, 0)),
                      pl.BlockSpec(memory_space=pl.ANY)],
            out_specs=pl.BlockSpec(memory_space=pl.ANY),
            scratch_shapes=[pltpu.SemaphoreType.DMA(())]),
        out_shape=jax.ShapeDtypeStruct(xs.shape, xs.dtype),
        input_output_aliases={3: 0},
        compiler_params=pltpu.CompilerParams(dimension_semantics=("arbitrary",),
                                             has_side_effects=True),
        name="moe_dispatch",
    )(offsets, pairs.reshape(t // tq, 1, 4 * tq), hp, xs)


def _unpack_pair(words):
    lo = pltpu.unpack_elementwise(words, index=0, packed_dtype=BF16, unpacked_dtype=F32)
    hi = pltpu.unpack_elementwise(words, index=1, packed_dtype=BF16, unpacked_dtype=F32)
    return lo, hi


def _expert_kernel(te_ref, nt_ref, xs_ref, wg_ref, wu_ref, wd_ref, ys_ref):
    del te_ref
    half = D_MODEL // 2

    @pl.when(pl.program_id(0) < nt_ref[0])
    def _():
        wg = wg_ref[0].astype(BF16)
        wu = wu_ref[0].astype(BF16)
        wd = wd_ref[0].astype(BF16)
        sub = xs_ref.shape[0] // EXPERT_SPLIT
        for i in range(EXPERT_SPLIT):
            rs = pl.ds(i * sub, sub)
            lo, hi = _unpack_pair(xs_ref[rs, :])
            lo = lo.astype(BF16)
            hi = hi.astype(BF16)
            g = _dot(lo, wg[:half]) + _dot(hi, wg[half:])
            u = _dot(lo, wu[:half]) + _dot(hi, wu[half:])
            hid = (g * (1.0 / (1.0 + jnp.exp(-g))) * u).astype(BF16)
            y = _dot(hid, wd)
            ys_ref[rs, :] = pltpu.pack_elementwise([y[:, :half], y[:, half:]], packed_dtype=BF16)

    @pl.when(pl.program_id(0) >= nt_ref[0])
    def _():
        ys_ref[...] = jnp.zeros_like(ys_ref)


def _experts(tile_expert, n_tiles, xs, w_gate, w_up, w_down):
    rows = xs.shape[0]
    n_max = rows // MOE_ROW_TILE

    def xmap(i, te, nt):
        return (jnp.maximum(jnp.minimum(i, nt[0] - 1), 0), 0)

    def wmap(i, te, nt):
        return (te[i], 0, 0)

    return pl.pallas_call(
        _expert_kernel,
        grid_spec=pltpu.PrefetchScalarGridSpec(
            num_scalar_prefetch=2,
            grid=(n_max,),
            in_specs=[pl.BlockSpec((MOE_ROW_TILE, D_MODEL // 2), xmap),
                      pl.BlockSpec((1, D_MODEL, D_EXPERT), wmap),
                      pl.BlockSpec((1, D_MODEL, D_EXPERT), wmap),
                      pl.BlockSpec((1, D_EXPERT, D_MODEL), wmap)],
            out_specs=pl.BlockSpec((MOE_ROW_TILE, D_MODEL // 2), lambda i, te, nt: (i, 0)),
        ),
        out_shape=jax.ShapeDtypeStruct(xs.shape, xs.dtype),
        compiler_params=_cparams(1),
        name="moe_experts",
    )(tile_expert, n_tiles, xs, w_gate, w_up, w_down)


def _combine_kernel(off_ref, pairs_ref, pairs_next_ref, ys_ref, y_ref, gate_ref, o_ref,
                    buf_ref, sem, *, tq):
    step = pl.program_id(0)
    n_steps = pl.num_programs(0)

    def gather(p_ref, slot, wait):
        def body(j, carry):
            for kk in range(2):
                cp = pltpu.make_async_copy(ys_ref.at[pl.ds(_pair_slot(off_ref, p_ref, j, kk), 1)],
                                           buf_ref.at[slot, kk, pl.ds(j, 1)], sem.at[slot])
                cp.wait() if wait else cp.start()
            return carry
        lax.fori_loop(0, tq, body, 0, unroll=8)

    @pl.when(step == 0)
    def _():
        gather(pairs_ref, 0, wait=False)

    @pl.when(step + 1 < n_steps)
    def _():
        gather(pairs_next_ref, (step + 1) % 2, wait=False)

    slot = step % 2
    gather(pairs_ref, slot, wait=True)
    half = D_MODEL // 2
    gates = gate_ref[...]
    w1 = gates[:, 0:1]
    w2 = gates[:, 1:2]
    lo1, hi1 = _unpack_pair(buf_ref[slot, 0])
    lo2, hi2 = _unpack_pair(buf_ref[slot, 1])
    y = y_ref[...]
    o_ref[:, :half] = y[:, :half] + (w1 * lo1 + w2 * lo2)
    o_ref[:, half:] = y[:, half:] + (w1 * hi1 + w2 * hi2)


def _combine(offsets, pairs, ys, y2, gates, tq):
    t = y2.shape[0]
    n = t // tq
    row = lambda i, off: (i, 0)
    smem = lambda f: pl.BlockSpec((1, 1, 4 * tq), f, memory_space=pltpu.SMEM)
    pairs3 = pairs.reshape(n, 1, 4 * tq)
    return pl.pallas_call(
        functools.partial(_combine_kernel, tq=tq),
        grid_spec=pltpu.PrefetchScalarGridSpec(
            num_scalar_prefetch=1, grid=(n,),
            in_specs=[smem(lambda i, off: (i, 0, 0)),
                      smem(lambda i, off: (jnp.minimum(i + 1, n - 1), 0, 0)),
                      pl.BlockSpec(memory_space=pl.ANY),
                      pl.BlockSpec((tq, D_MODEL), row), pl.BlockSpec((tq, 2), row)],
            out_specs=pl.BlockSpec((tq, D_MODEL), row),
            scratch_shapes=[pltpu.VMEM((2, 2, tq, D_MODEL // 2), jnp.uint32),
                            pltpu.SemaphoreType.DMA((2,))]),
        out_shape=jax.ShapeDtypeStruct((t, D_MODEL), F32),
        compiler_params=_cparams(1),
        name="moe_combine",
    )(offsets, pairs3, pairs3, ys, y2, gates)


def _for_each_chunk(tot_ref, rows_ref, blk, fn):
    base = blk * SORT_CHUNKS

    def body(k, carry):
        fn(pl.multiple_of(k * RUN_ALIGN, RUN_ALIGN), pl.multiple_of(rows_ref[base + k], RUN_ALIGN))
        return carry

    lax.fori_loop(0, tot_ref[blk], body, 0)


def _sorted_dispatch_kernel(tot_ref, rows_ref, zt_ref, nt_ref, h_ref, lpos_ref, xs_ref,
                            stage_ref, zero_ref, sem, zsem, *, n_zero, n_tiles_max):
    step = pl.program_id(0)
    n_steps = pl.num_programs(0)

    @pl.when(step == 0)
    def _():
        zero_ref[...] = jnp.zeros_like(zero_ref)

        def fill(granule):
            row = pl.multiple_of(granule * ZERO_ROWS, ZERO_ROWS)
            return pltpu.make_async_copy(zero_ref, xs_ref.at[pl.ds(row, ZERO_ROWS)], zsem)

        def tail(do):
            def body(tile, carry):
                do(fill(tile))
                return carry
            lax.fori_loop(nt_ref[0], n_tiles_max, body, 0)

        for i in range(n_zero):
            pl.when(zt_ref[i] >= 0)(fill(jnp.maximum(zt_ref[i], 0)).start)
        tail(lambda cp: cp.start())
        for i in range(n_zero):
            pl.when(zt_ref[i] >= 0)(fill(jnp.maximum(zt_ref[i], 0)).wait)
        tail(lambda cp: cp.wait())

    slot = step % 2
    half = D_MODEL // 2
    rb = h_ref.shape[0]
    lpos = lpos_ref[0]
    jrow = lax.broadcasted_iota(I32, (SORT_ROWS, rb), 0).astype(F32)
    perm = jnp.where((jrow == lpos[0:1]) | (jrow == lpos[1:2]), 1.0, 0.0).astype(BF16)
    x = _dot(perm, h_ref[...])
    stage_ref[slot] = pltpu.pack_elementwise([x[:, :half], x[:, half:]], packed_dtype=BF16)

    def copy(s):
        return lambda l, g: pltpu.make_async_copy(stage_ref.at[s, pl.ds(l, RUN_ALIGN)],
                                                  xs_ref.at[pl.ds(g, RUN_ALIGN)], sem.at[s])

    @pl.when(step > 0)
    def _():
        _for_each_chunk(tot_ref, rows_ref,step - 1,
                        lambda l, g: copy(1 - slot)(l, g).wait())

    _for_each_chunk(tot_ref, rows_ref,step, lambda l, g: copy(slot)(l, g).start())

    @pl.when(step == n_steps - 1)
    def _():
        _for_each_chunk(tot_ref, rows_ref,step, lambda l, g: copy(slot)(l, g).wait())


def _sorted_dispatch(chunk_total, chunk_rows, zero_tiles, n_tiles, h_bf, lpos_t, n_rows):
    n_blocks, _, rb = lpos_t.shape
    return pl.pallas_call(
        functools.partial(_sorted_dispatch_kernel, n_zero=zero_tiles.shape[0],
                          n_tiles_max=n_rows // ZERO_ROWS),
        grid_spec=pltpu.PrefetchScalarGridSpec(
            num_scalar_prefetch=4, grid=(n_blocks,),
            in_specs=[pl.BlockSpec((rb, D_MODEL), lambda i, *_: (i, 0)),
                      pl.BlockSpec((1, 8, rb), lambda i, *_: (i, 0, 0))],
            out_specs=pl.BlockSpec(memory_space=pl.ANY),
            scratch_shapes=[pltpu.VMEM((2, SORT_ROWS, D_MODEL // 2), jnp.uint32),
                            pltpu.VMEM((ZERO_ROWS, D_MODEL // 2), jnp.uint32),
                            pltpu.SemaphoreType.DMA((2,)), pltpu.SemaphoreType.DMA(())]),
        out_shape=jax.ShapeDtypeStruct((n_rows, D_MODEL // 2), jnp.uint32),
        compiler_params=pltpu.CompilerParams(dimension_semantics=("arbitrary",),
                                             has_side_effects=True, vmem_limit_bytes=VMEM_LIMIT),
        name="moe_sorted_dispatch",
    )(chunk_total, chunk_rows, zero_tiles, n_tiles, h_bf, lpos_t)


def _sorted_combine_kernel(tot_ref, rows_ref, ys_ref, y_ref, gate_ref, o_ref, buf_ref, sem):
    step = pl.program_id(0)
    n_steps = pl.num_programs(0)

    def gather(blk, s, wait):
        def fn(l, g):
            cp = pltpu.make_async_copy(ys_ref.at[pl.ds(g, RUN_ALIGN)],
                                       buf_ref.at[s, pl.ds(l, RUN_ALIGN)], sem.at[s])
            cp.wait() if wait else cp.start()
        _for_each_chunk(tot_ref, rows_ref,blk, fn)

    @pl.when(step == 0)
    def _():
        buf_ref[...] = jnp.zeros_like(buf_ref)
        gather(0, 0, wait=False)

    @pl.when(step + 1 < n_steps)
    def _():
        gather(step + 1, (step + 1) % 2, wait=False)

    slot = step % 2
    gather(step, slot, wait=True)
    half = D_MODEL // 2
    rb = y_ref.shape[0]
    gates = gate_ref[...]
    jcol = lax.broadcasted_iota(I32, (rb, SORT_ROWS), 1).astype(F32)
    pick1 = jnp.where(jcol == gates[:, 2:3], 1.0, 0.0).astype(BF16)
    pick2 = jnp.where(jcol == gates[:, 3:4], 1.0, 0.0).astype(BF16)
    lo, hi = _unpack_pair(buf_ref[slot])
    lo = lo.astype(BF16)
    hi = hi.astype(BF16)
    w1 = gates[:, 0:1]
    w2 = gates[:, 1:2]
    y = y_ref[...]
    o_ref[:, :half] = y[:, :half] + (w1 * _dot(pick1, lo) + w2 * _dot(pick2, lo))
    o_ref[:, half:] = y[:, half:] + (w1 * _dot(pick1, hi) + w2 * _dot(pick2, hi))


def _sorted_combine(chunk_total, chunk_rows, ys, y2, gates, rb):
    t = y2.shape[0]
    row = lambda i, *_: (i, 0)
    return pl.pallas_call(
        _sorted_combine_kernel,
        grid_spec=pltpu.PrefetchScalarGridSpec(
            num_scalar_prefetch=2, grid=(t // rb,),
            in_specs=[pl.BlockSpec(memory_space=pl.ANY),
                      pl.BlockSpec((rb, D_MODEL), row), pl.BlockSpec((rb, 4), row)],
            out_specs=pl.BlockSpec((rb, D_MODEL), row),
            scratch_shapes=[pltpu.VMEM((2, SORT_ROWS, D_MODEL // 2), jnp.uint32),
                            pltpu.SemaphoreType.DMA((2,))]),
        out_shape=jax.ShapeDtypeStruct((t, D_MODEL), F32),
        compiler_params=_cparams(1),
        name="moe_sorted_combine",
    )(chunk_total, chunk_rows, ys, y2, gates)


def _lane_table(per_head):
    lanes = jnp.repeat(per_head.astype(F32), HEAD_DIM)
    return lanes.reshape(N_PAIRS, 1, LANES), lanes.reshape(1, WIDTH)


def kernel(x_prompt, x_sample, mem_prompt, state_ret, cache_win_k, cache_win_v, cache_mem_k,
           cache_mem_v, g_mix, w_in, g_ret_out, g_q_att, g_k_att, w_o, g_cross, g_mem, w_mq, w_mk,
           w_mv, g_q_mem, g_k_mem, w_mo, g_moe, w_router_group, b_router_group, w_router_expert,
           b_router_expert, w_exp_gate, w_exp_up, w_exp_down):
    depth = g_mix.shape[0]
    assert depth == 1
    b, s, _ = x_prompt.shape
    bs, ns, _ = x_sample.shape
    t_p, t_s = b * s, bs * ns
    l = 0

    heads = jnp.arange(N_HEADS, dtype=F32)
    lg_pairs, lg_lane = _lane_table(jnp.log(1.0 - 2.0 ** (-5.0 - heads)))
    slope_pairs, slope_lane = _lane_table(2.0 ** (-8.0 * (heads + 1.0) / N_HEADS))
    gout_lane = g_ret_out[l].reshape(1, WIDTH)
    gout_pairs = g_ret_out[l].reshape(N_PAIRS, 1, LANES)
    gq_t = jnp.tile(g_q_att[l], N_HEADS).reshape(1, WIDTH)
    gk_t = jnp.tile(g_k_att[l], N_HEADS).reshape(1, WIDTH)
    vec = lambda v: v.reshape(1, -1)
    w_in_bf = w_in[l].astype(BF16)
    w_o_bf, w_mq_bf, w_mk_bf = w_o[l].astype(BF16), w_mq[l].astype(BF16), w_mk[l].astype(BF16)
    w_mv_bf, w_mo_bf = w_mv[l].astype(BF16), w_mo[l].astype(BF16)
    gap = EXPERT_LANE0 - N_GROUPS
    tail = LANES - EXPERT_LANE0 - N_EXPERTS
    w_r = jnp.concatenate(
        [w_router_group[l], jnp.zeros((D_MODEL, gap), F32),
         jnp.moveaxis(w_router_expert[l], 0, 1).reshape(D_MODEL, N_EXPERTS),
         jnp.zeros((D_MODEL, tail), F32)], axis=1)
    b_r = jnp.concatenate([b_router_group[l], jnp.zeros((gap,), F32),
                           b_router_expert[l].reshape(-1), jnp.zeros((tail,), F32)]).reshape(1, LANES)
    w_r_hi = w_r.astype(BF16)
    w_r_lo = (w_r - w_r_hi.astype(F32)).astype(BF16)

    xp = x_prompt.reshape(t_p, D_MODEL)
    qr, kr, vr, gr, qa, ka, va, ka_t, va_t = _mix_proj(xp, vec(g_mix[l]), w_in_bf, gq_t, gk_t,
                                                       tm=1024, act_dtype=BF16, seq=s)
    shp = lambda z: z.reshape(b, s, WIDTH)
    r_p, state_p = _retention(shp(qr), shp(kr), shp(vr), shp(gr), lg_pairs, gout_pairs)
    a_p = _dilated_attention(shp(qa), shp(ka), shp(va), slope_pairs)
    mk_p, mv_p, mk_bf, mv_bf = _mem_kv(mem_prompt.reshape(b * N_MEM, D_MODEL), vec(g_mem[l]), w_mk_bf, w_mv_bf,
                          vec(g_k_mem[l]))
    zero_cnt = jnp.zeros((1, LANES), F32)
    y2_p, h_p, gates_p, lpos_p, tabs_p, cnt_p = _cross_router(
        xp, r_p.reshape(t_p, WIDTH), a_p.reshape(t_p, WIDTH), w_o_bf, vec(g_cross[l]), w_mq_bf,
        vec(g_q_mem[l]), mk_bf.reshape(b, N_MEM, D_MODEL), mv_bf.reshape(b, N_MEM, D_MODEL), w_mo_bf,
        vec(g_moe[l]), w_r_hi, w_r_lo, b_r, zero_cnt, n_batch=b, tq=1024, n_split=4,
        sorted_mode=True)

    xs_ = x_sample.reshape(t_s, D_MODEL)
    qr, kr, vr, gr, qa, ka_s, va_s = _mix_proj(xs_, vec(g_mix[l]), w_in_bf, gq_t, gk_t, tm=t_s,
                                               act_dtype=F32)
    pos_minor = lambda c: jnp.transpose(c, (0, 2, 3, 1)).reshape(bs, WIDTH, c.shape[1])
    r_s, a_s, state_s = _sample_mixer(lg_lane, gout_lane, slope_lane, qr, kr, vr, gr, qa, ka_s, va_s,
                                      state_ret[l], pos_minor(cache_win_k[l]),
                                      pos_minor(cache_win_v[l]), n_new=ns)
    y2_s, hp_s, gates_s, pairs_s, cnt_all = _cross_router(
        xs_, r_s, a_s, w_o_bf, vec(g_cross[l]), w_mq_bf, vec(g_q_mem[l]),
        cache_mem_k[l], cache_mem_v[l],
        w_mo_bf, vec(g_moe[l]), w_r_hi, w_r_lo, b_r, cnt_p, n_batch=bs, tq=min(bs, 8) * ns, n_split=1,
        sorted_mode=False)

    tile = MOE_ROW_TILE
    lanes_e = slice(EXPERT_LANE0, EXPERT_LANE0 + N_EXPERTS)
    counts = cnt_all[0, lanes_e].astype(I32)
    tiles_per = (counts + tile - 1) // tile
    tile_end = jnp.cumsum(tiles_per)
    offsets = (tile_end - tiles_per) * tile
    n_tiles = tile_end[-1:]
    n_blocks = t_p // ROUTE_BLOCK
    n_max = (2 * (t_p + t_s) + n_blocks * N_EXPERTS * (RUN_ALIGN - 1)) // tile + N_EXPERTS
    tile_ids = jnp.minimum(jnp.arange(n_max, dtype=I32), n_tiles[0] - 1)
    tile_expert = jnp.sum((tile_end[None, :] <= tile_ids[:, None]).astype(I32), axis=1)
    seg_end = tile_end * tile
    first_gran = (offsets + cnt_p[0, lanes_e].astype(I32)) // ZERO_ROWS
    n_gran = (t_s + tile) // ZERO_ROWS + 1
    gran = first_gran[None, :] + jnp.arange(n_gran, dtype=I32)[:, None]
    zero_tiles = jnp.where(gran * ZERO_ROWS < seg_end[None, :], gran, -1).reshape(-1)
    first_tail_gran = n_tiles * (tile // ZERO_ROWS)
    run_chunks = (tabs_p[:, 0, lanes_e].astype(I32) + RUN_ALIGN - 1) // RUN_ALIGN
    run_end = jnp.cumsum(run_chunks, axis=1)
    chunk_total = run_end[:, -1]
    chunk_id = jnp.arange(SORT_CHUNKS, dtype=I32)
    owner = (chunk_id[None, :, None] >= run_end[:, None, :]).astype(I32).sum(axis=2)
    owner_hot = owner[:, :, None] == jnp.arange(N_EXPERTS, dtype=I32)[None, None, :]
    run_row0 = offsets[None, :] + tabs_p[:, 2, lanes_e].astype(I32) - (run_end - run_chunks) * RUN_ALIGN
    chunk_rows = (jnp.where(owner_hot, run_row0[:, None, :], 0).sum(axis=2)
                  + chunk_id[None, :] * RUN_ALIGN).reshape(-1)
    xs_sorted = _sorted_dispatch(chunk_total, chunk_rows, zero_tiles, first_tail_gran, h_p, lpos_p,
                                 n_max * tile)
    xs_sorted = _dispatch(offsets, pairs_s, hp_s, xs_sorted, tq=t_s)
    ys_sorted = _experts(tile_expert, n_tiles, xs_sorted, w_exp_gate[l], w_exp_up[l], w_exp_down[l])
    y_p = _sorted_combine(chunk_total, chunk_rows, ys_sorted, y2_p, gates_p, rb=ROUTE_BLOCK)
    y_s = _combine(offsets, pairs_s, ys_sorted, y2_s, gates_s, tq=t_s)

    from_pos_minor = lambda z: jnp.transpose(z.reshape(b, N_HEADS, HEAD_DIM, s), (0, 3, 1, 2))[None]
    return (y_p.reshape(b, s, D_MODEL), y_s.reshape(bs, ns, D_MODEL),
            state_p[None],
            from_pos_minor(ka_t), from_pos_minor(va_t),
            mk_p.reshape(1, b, N_MEM, N_MEM_HEADS, MEM_HEAD_DIM),
            mv_p.reshape(1, b, N_MEM, N_MEM_HEADS, MEM_HEAD_DIM),
            state_s[None],
            ka_s.reshape(1, bs, ns, N_HEADS, HEAD_DIM), va_s.reshape(1, bs, ns, N_HEADS, HEAD_DIM))
```

```python
import functools

import jax
import jax.numpy as jnp
from jax import lax
from jax.experimental import pallas as pl
from jax.experimental.pallas import tpu as pltpu

F32 = jnp.float32
BF16 = jnp.bfloat16
I32 = jnp.int32

D_MODEL = 1024
HEAD_DIM = 64
N_HEADS = 8
WIDTH = N_HEADS * HEAD_DIM
N_PAIRS = N_HEADS // 2
IN_COLS = 7 * WIDTH
RET_CHUNK = 128
RET_UNROLL = 8
ATT_GROUP = 4
ATT_BLOCK = 128
WIN_STEPS = 128
DILATIONS = (1, 4, 16)
N_MEM = 256
N_MEM_HEADS = 4
MEM_HEAD_DIM = 256
N_GROUPS = 4
N_EXP_PER_GROUP = 8
N_EXPERTS = 32
D_EXPERT = 256
EPS = 1e-6
LANES = 128
EXPERT_LANE0 = 32
MOE_ROW_TILE = 512
EXPERT_SPLIT = 2
ZERO_ROWS = 256
ROUTE_BLOCK = 256
RUN_ALIGN = 8
SORT_ROWS = 2 * ROUTE_BLOCK + N_EXPERTS * RUN_ALIGN
SORT_CHUNKS = SORT_ROWS // RUN_ALIGN
SORT_BLOCKS_PER_STEP = 2
VMEM_LIMIT = 56 * 1024 * 1024

NEG_INF = float("-inf")


def _cparams(n_axes, vmem=VMEM_LIMIT):
    return pltpu.CompilerParams(dimension_semantics=("arbitrary",) * n_axes,
                                vmem_limit_bytes=vmem)


def _dot(a, b):
    return jnp.dot(a, b, preferred_element_type=F32)


def _dot_nt(a, b):
    return lax.dot_general(a, b, (((1,), (1,)), ((), ())), preferred_element_type=F32)


def _dot_tn(a, b):
    return lax.dot_general(a, b, (((0,), (0,)), ((), ())), preferred_element_type=F32)


def _rms(x, g):
    ms = jnp.mean(x * x, axis=-1, keepdims=True)
    return x * lax.rsqrt(ms + EPS) * g


def _mix_proj_kernel(x_ref, g_ref, w_ref, gq_ref, gk_ref,
                     qr_ref, kr_ref, vr_ref, gr_ref, qa_ref, ka_ref, va_ref, *kv_t_refs):
    act = qr_ref.dtype
    h = _rms(x_ref[...], g_ref[...]).astype(BF16)

    def proj(j):
        return _dot(h, w_ref[:, j * WIDTH:(j + 1) * WIDTH])

    def head_norm(z, g):
        is_lo = _pair_masks((z.shape[0], LANES))
        parts = []
        for p in range(N_PAIRS):
            zp = z[:, p * LANES:(p + 1) * LANES]
            parts.append(zp * lax.rsqrt(_segment_mean(zp * zp, is_lo) + EPS))
        return jnp.concatenate(parts, axis=1) * g

    qr_ref[...] = proj(0).astype(act)
    kr_ref[...] = proj(1).astype(act)
    vr_ref[...] = proj(2).astype(act)
    gr_ref[...] = proj(3).astype(act)
    qa_ref[...] = head_norm(proj(4), gq_ref[...]).astype(act)
    ka = head_norm(proj(5), gk_ref[...])
    va = proj(6)
    ka_ref[...] = ka.astype(ka_ref.dtype)
    va_ref[...] = va.astype(va_ref.dtype)
    if kv_t_refs:
        kat_ref, vat_ref = kv_t_refs
        kat_ref[0] = ka.T
        vat_ref[0] = va.T


def _mix_proj(x2d, g_mix, w_in_bf, gq_t, gk_t, tm, act_dtype, seq=None):
    t = x2d.shape[0]
    const = lambda i: (0, 0)
    row = lambda i: (i, 0)
    out_act = jax.ShapeDtypeStruct((t, WIDTH), act_dtype)
    kv_dtype = F32 if seq is None else act_dtype
    out_kv = jax.ShapeDtypeStruct((t, WIDTH), kv_dtype)
    out_specs = [pl.BlockSpec((tm, WIDTH), row)] * 7
    out_shape = [out_act] * 5 + [out_kv] * 2
    if seq is not None:
        per_b = seq // tm
        t_spec = pl.BlockSpec((1, WIDTH, tm), lambda i: (i // per_b, 0, i % per_b))
        out_specs += [t_spec, t_spec]
        out_shape += [jax.ShapeDtypeStruct((t // seq, WIDTH, seq), F32)] * 2
    return pl.pallas_call(
        _mix_proj_kernel,
        grid=(t // tm,),
        in_specs=[
            pl.BlockSpec((tm, D_MODEL), row),
            pl.BlockSpec((1, D_MODEL), const),
            pl.BlockSpec((D_MODEL, IN_COLS), const),
            pl.BlockSpec((1, WIDTH), const),
            pl.BlockSpec((1, WIDTH), const),
        ],
        out_specs=out_specs,
        out_shape=out_shape,
        compiler_params=_cparams(1),
        name="mix_proj",
    )(x2d, g_mix, w_in_bf, gq_t, gk_t)


def _pair_masks(shape):
    lane = lax.broadcasted_iota(I32, shape, len(shape) - 1)
    return lane < HEAD_DIM


def _segment_mean(x, is_lo):
    zero = jnp.zeros_like(x)
    lo = jnp.sum(jnp.where(is_lo, x, zero), axis=-1, keepdims=True)
    hi = jnp.sum(jnp.where(is_lo, zero, x), axis=-1, keepdims=True)
    return jnp.where(is_lo, lo, hi) * (1.0 / HEAD_DIM)


def _group_norm_gate(o, g_r, g_out, is_lo):
    c = o - _segment_mean(o, is_lo)
    y = c * lax.rsqrt(_segment_mean(c * c, is_lo) + EPS) * g_out
    g = g_r.astype(F32)
    return y * (g * (1.0 / (1.0 + jnp.exp(-g))))


def _retention_kernel(lg_ref, gout_ref, q_ref, k_ref, v_ref, g_ref, o_ref, st_ref, kv_ref, s_ref,
                      *, n_chunks):
    c_len = RET_CHUNK
    shape = (c_len, LANES)
    is_lo = _pair_masks(shape)
    row = lax.broadcasted_iota(I32, shape, 0)
    col = lax.broadcasted_iota(I32, shape, 1)
    rel = (row - col).astype(F32)
    lg_lane = lg_ref[0]
    lg0 = lg_lane[:, 0:1]
    lg1 = lg_lane[:, HEAD_DIM:HEAD_DIM + 1]
    scale = HEAD_DIM ** -0.5
    causal = rel >= 0.0
    relp = jnp.maximum(rel, 0.0)
    d0 = jnp.where(causal, jnp.exp(lg0 * relp), 0.0) * scale
    d1 = jnp.where(causal, jnp.exp(lg1 * relp), 0.0) * scale
    rowf = row.astype(F32)
    w_k = jnp.exp(lg_lane * (c_len - 1.0 - rowf)) * scale
    w_q = jnp.exp(lg_lane * (rowf + 1.0))
    lg_row = jnp.where(row < HEAD_DIM, lg0, lg1)
    g_chunk = jnp.exp(lg_row * float(c_len))
    same_head = (row < HEAD_DIM) == (col < HEAD_DIM)
    g_out = gout_ref[0]

    def chunk(c):
        return pl.ds(pl.multiple_of(c * c_len, c_len), c_len)

    def outer(i, carry):
        for u in range(RET_UNROLL):
            c = i * RET_UNROLL + u
            kw = (k_ref[0, chunk(c), :].astype(F32) * w_k).astype(BF16)
            kv_ref[c] = jnp.where(same_head, _dot_tn(kw, v_ref[0, chunk(c), :]), 0.0)
        return carry

    lax.fori_loop(0, n_chunks // RET_UNROLL, outer, 0)

    def recur(c, state):
        kv = kv_ref[c]
        kv_ref[c] = state
        return g_chunk * state + kv

    state = lax.fori_loop(0, n_chunks, recur, jnp.zeros(shape, F32))
    st_ref[0, 0] = state[:HEAD_DIM, :HEAD_DIM]
    st_ref[0, 1] = state[HEAD_DIM:, HEAD_DIM:]

    def score_stage(i, slot):
        for u in range(RET_UNROLL):
            sl = chunk(i * RET_UNROLL + u)
            qc = q_ref[0, sl, :]
            kc = k_ref[0, sl, :]
            zero = jnp.zeros_like(qc)
            s_ref[slot, 2 * u] = (_dot_nt(jnp.where(is_lo, qc, zero), kc) * d0).astype(BF16)
            s_ref[slot, 2 * u + 1] = (_dot_nt(jnp.where(is_lo, zero, qc), kc) * d1).astype(BF16)

    def value_stage(i, slot):
        for u in range(RET_UNROLL):
            c = i * RET_UNROLL + u
            sl = chunk(c)
            qc = q_ref[0, sl, :]
            vc = v_ref[0, sl, :]
            zero = jnp.zeros_like(vc)
            o_in = (_dot(s_ref[slot, 2 * u], jnp.where(is_lo, vc, zero))
                    + _dot(s_ref[slot, 2 * u + 1], jnp.where(is_lo, zero, vc)))
            o_x = _dot(qc, kv_ref[c].astype(BF16)) * w_q
            o_ref[0, sl, :] = _group_norm_gate(o_in + o_x, g_ref[0, sl, :], g_out,
                                               is_lo).astype(BF16)

    n_groups = n_chunks // RET_UNROLL
    score_stage(0, 0)

    def inner(i, carry):
        value_stage(i, i % 2)
        score_stage(i + 1, (i + 1) % 2)
        return carry

    lax.fori_loop(0, n_groups - 1, inner, 0)
    value_stage(n_groups - 1, (n_groups - 1) % 2)


def _retention(q, k, v, g, lg_pairs, gout_pairs):
    b, s, _ = q.shape
    blk = pl.BlockSpec((1, s, LANES), lambda i, p: (i, 0, p))
    tab = pl.BlockSpec((1, 1, LANES), lambda i, p: (p, 0, 0))
    return pl.pallas_call(
        functools.partial(_retention_kernel, n_chunks=s // RET_CHUNK),
        grid=(b, N_PAIRS),
        in_specs=[tab, tab, blk, blk, blk, blk],
        out_specs=[blk, pl.BlockSpec((1, 2, HEAD_DIM, HEAD_DIM), lambda i, p: (i, p, 0, 0))],
        out_shape=[jax.ShapeDtypeStruct((b, s, WIDTH), BF16),
                   jax.ShapeDtypeStruct((b, N_HEADS, HEAD_DIM, HEAD_DIM), F32)],
        scratch_shapes=[pltpu.VMEM((s // RET_CHUNK, LANES, LANES), F32),
                        pltpu.VMEM((2, 2 * RET_UNROLL, RET_CHUNK, RET_CHUNK), BF16)],
        compiler_params=_cparams(2),
        name="retention",
    )(lg_pairs, gout_pairs, q, k, v, g)


def _attn_scores(qb, kb, bias0, bias1, is_lo):
    qb = qb.astype(BF16)
    kb = kb.astype(BF16)
    zq = jnp.zeros_like(qb)
    s0 = _dot_nt(jnp.where(is_lo, qb, zq), kb) + bias0
    s1 = _dot_nt(jnp.where(is_lo, zq, qb), kb) + bias1
    return s0, s1


def _attn_values(s0, s1, vb, is_lo):
    m0 = jnp.max(s0, axis=-1, keepdims=True)
    m1 = jnp.max(s1, axis=-1, keepdims=True)
    p0 = jnp.exp(s0 - m0).astype(BF16)
    p1 = jnp.exp(s1 - m1).astype(BF16)
    is_lo_k = _pair_masks(vb.shape)
    one = jnp.ones_like(vb)
    r0 = _dot(p0, jnp.where(is_lo_k, vb, one).astype(BF16))
    r1 = _dot(p1, jnp.where(is_lo_k, one, vb).astype(BF16))
    acc = jnp.where(is_lo, r0, r1)
    l = pltpu.roll(jnp.where(is_lo, r1, r0), HEAD_DIM, 1)
    m = jnp.where(is_lo, m0, m1)
    return acc, m, l


def _dil_attn_kernel(sl_ref, q_ref, k_ref, v_ref, o_ref,
                     qf_ref, kf_ref, vf_ref, acc_ref, m_ref, l_ref, bias_ref, bias1_ref, s_ref, *,
                     seq):
    blk = ATT_BLOCK
    is_lo = _pair_masks((blk, LANES))
    slope = sl_ref[0]
    slope0 = slope[:, 0:1]
    slope1 = slope[:, HEAD_DIM:HEAD_DIM + 1]

    qf_ref[...] = q_ref[0].astype(F32) * (HEAD_DIM ** -0.5)
    kf_ref[...] = k_ref[0].astype(F32)
    vf_ref[...] = v_ref[0].astype(F32)

    @pl.when(pl.program_id(1) == 0)
    def _():
        i2 = lax.broadcasted_iota(I32, (blk, 2 * blk), 0)
        j2 = lax.broadcasted_iota(I32, (blk, 2 * blk), 1)
        rel2 = blk + i2 - j2
        ok2 = (rel2 >= 0) & (rel2 <= WIN_STEPS)
        i1 = lax.broadcasted_iota(I32, (blk, blk), 0)
        j1 = lax.broadcasted_iota(I32, (blk, blk), 1)
        rel1 = i1 - j1
        ok1 = rel1 >= 0
        for pi, d in enumerate(DILATIONS):
            dist2 = (rel2 * d).astype(F32)
            dist1 = (rel1 * d).astype(F32)
            bias_ref[2 * pi] = jnp.where(ok2, -slope0 * dist2, NEG_INF)
            bias_ref[2 * pi + 1] = jnp.where(ok2, -slope1 * dist2, NEG_INF)
            bias1_ref[2 * pi] = jnp.where(ok1, -slope0 * dist1, NEG_INF)
            bias1_ref[2 * pi + 1] = jnp.where(ok1, -slope1 * dist1, NEG_INF)

    def rows(start, n, d):
        return pl.ds(start, n) if d == 1 else pl.ds(start, n, stride=d)

    def store(pi, sl, res):
        acc, m, l = res
        acc_ref[pi, sl, :] = acc
        m_ref[pi, sl, :] = m
        l_ref[pi, sl, :] = l

    def key_rows(q0, k0, d):
        return rows(q0, blk, d) if k0 is None else rows(k0, 2 * blk, d)

    def score_stage(pi, d, items, slot):
        for u, (q0, k0) in enumerate(items):
            b_ref, nk = (bias1_ref, blk) if k0 is None else (bias_ref, 2 * blk)
            s0, s1 = _attn_scores(qf_ref[rows(q0, blk, d), :], kf_ref[key_rows(q0, k0, d), :],
                                  b_ref[2 * pi], b_ref[2 * pi + 1], is_lo)
            s_ref[slot, 2 * u, :, :nk] = s0
            s_ref[slot, 2 * u + 1, :, :nk] = s1

    def value_stage(pi, d, items, slot):
        for u, (q0, k0) in enumerate(items):
            nk = blk if k0 is None else 2 * blk
            res = _attn_values(s_ref[slot, 2 * u, :, :nk], s_ref[slot, 2 * u + 1, :, :nk],
                               vf_ref[key_rows(q0, k0, d), :], is_lo)
            store(pi, rows(q0, blk, d), res)

    def pipeline(pi, d, n_groups, items_of):
        score_stage(pi, d, items_of(0), 0)

        def body(g, carry):
            value_stage(pi, d, items_of(g), g % 2)
            score_stage(pi, d, items_of(g + 1), (g + 1) % 2)
            return carry

        lax.fori_loop(0, n_groups - 1, body, 0)
        value_stage(pi, d, items_of(n_groups - 1), (n_groups - 1) % 2)

    for pi, d in enumerate(DILATIONS):
        n_blocks = seq // d // blk
        step = blk * d
        firsts = [(r, None) for r in range(d)]
        if n_blocks == 1:
            assert d % ATT_GROUP == 0
            pipeline(pi, d, d // ATT_GROUP,
                     lambda g: [(g * ATT_GROUP + u, None) for u in range(ATT_GROUP)])
            continue
        assert d <= ATT_GROUP
        score_stage(pi, d, firsts, 0)
        value_stage(pi, d, firsts, 0)
        per_group = max(p for p in range(1, ATT_GROUP // d + 1) if (n_blocks - 1) % p == 0)

        def general(g, d=d, step=step, per_group=per_group):
            items = []
            for u in range(per_group):
                base = (1 + g * per_group + u) * step
                if d == 1:
                    base = pl.multiple_of(base, blk)
                items += [(base + r, base + r - step) for r in range(d)]
            return items

        pipeline(pi, d, (n_blocks - 1) // per_group, general)

    def merge(c, carry):
        sl = pl.ds(pl.multiple_of(c * 256, 256), 256)
        m_all = [m_ref[pi, sl, :] for pi in range(3)]
        m_max = jnp.maximum(jnp.maximum(m_all[0], m_all[1]), m_all[2])
        num = jnp.zeros((256, LANES), F32)
        den = jnp.zeros((256, LANES), F32)
        for pi in range(3):
            w = jnp.exp(m_all[pi] - m_max)
            num = num + w * acc_ref[pi, sl, :]
            den = den + w * l_ref[pi, sl, :]
        o_ref[0, sl, :] = (num / den).astype(BF16)
        return carry

    lax.fori_loop(0, seq // 256, merge, 0)


def _dilated_attention(qa, ka, va, slope_pairs):
    b, s, _ = qa.shape
    blk = pl.BlockSpec((1, s, LANES), lambda p, i: (i, 0, p))
    tab = pl.BlockSpec((1, 1, LANES), lambda p, i: (p, 0, 0))
    return pl.pallas_call(
        functools.partial(_dil_attn_kernel, seq=s),
        grid=(N_PAIRS, b),
        in_specs=[tab, blk, blk, blk],
        out_specs=blk,
        out_shape=jax.ShapeDtypeStruct((b, s, WIDTH), BF16),
        scratch_shapes=[
            pltpu.VMEM((s, LANES), F32),
            pltpu.VMEM((s, LANES), F32),
            pltpu.VMEM((s, LANES), F32),
            pltpu.VMEM((3, s, LANES), F32),
            pltpu.VMEM((3, s, LANES), F32),
            pltpu.VMEM((3, s, LANES), F32),
            pltpu.VMEM((6, ATT_BLOCK, 2 * ATT_BLOCK), F32),
            pltpu.VMEM((6, ATT_BLOCK, ATT_BLOCK), F32),
            pltpu.VMEM((2, 2 * ATT_GROUP, ATT_BLOCK, 2 * ATT_BLOCK), F32),
        ],
        compiler_params=_cparams(2),
        name="dilated_attention",
    )(slope_pairs, qa, ka, va)


def _sample_mixer_kernel(lg_ref, gout_ref, slope_ref, qr_ref, kr_ref, vr_ref, gr_ref,
                         qa_ref, ka_ref, va_ref, st_ref, ck_ref, cv_ref,
                         r_ref, a_ref, sto_ref, bias_ref, cnt_ref, *, n_new, w_buf):
    n = n_new
    scale = HEAD_DIM ** -0.5
    lg = lg_ref[...]
    qr = qr_ref[...]
    kr = kr_ref[...]
    vr = vr_ref[...]
    ri = lax.broadcasted_iota(I32, (n, n), 0)
    rj = lax.broadcasted_iota(I32, (n, n), 1)
    rel = (ri - rj).astype(F32)
    rowf = lax.broadcasted_iota(I32, (n, HEAD_DIM), 0).astype(F32)
    outs = []
    for h in range(N_HEADS):
        hs = slice(h * HEAD_DIM, (h + 1) * HEAD_DIM)
        lg_h = lg[:, h * HEAD_DIM:h * HEAD_DIM + 1]
        qh, kh, vh = qr[:, hs], kr[:, hs], vr[:, hs]
        dm = jnp.where(rel >= 0.0, jnp.exp(lg_h * jnp.maximum(rel, 0.0)), 0.0) * scale
        s = _dot_nt(qh, kh) * dm
        s_prev = st_ref[0, h]
        o = _dot(s, vh) + _dot(qh, s_prev) * jnp.exp(lg_h * (rowf + 1.0))
        kw = kh * (jnp.exp(lg_h * (n - 1.0 - rowf)) * scale)
        sto_ref[0, h] = jnp.exp(lg_h * float(n)) * s_prev + _dot_tn(kw, vh)
        outs.append(o)
    o_r = jnp.concatenate(outs, axis=1)
    normed = []
    for p in range(N_PAIRS):
        ps = slice(p * LANES, (p + 1) * LANES)
        is_lo = _pair_masks((n, LANES))
        normed.append(_group_norm_gate(o_r[:, ps], gr_ref[:, ps], gout_ref[:, ps], is_lo))
    r_ref[...] = jnp.concatenate(normed, axis=1)

    n_rows = N_HEADS * n
    n_keys = w_buf + LANES

    @pl.when(pl.program_id(0) == 0)
    def _():
        rr = lax.broadcasted_iota(I32, (n_rows, n_keys), 0)
        cc = lax.broadcasted_iota(I32, (n_rows, n_keys), 1)
        qi = rr % n
        dist = w_buf + qi - cc
        valid = (dist >= 0) & (cc < w_buf + n)
        cnt = jnp.zeros((n_rows, n_keys), F32)
        for d in DILATIONS:
            hit = valid & (dist % d == 0) & (dist <= WIN_STEPS * d)
            cnt = cnt + jnp.where(hit, 1.0, 0.0)
        cnt_ref[...] = cnt
        slope = slope_ref[...]
        srow = jnp.zeros((n_rows, 1), F32)
        r1 = lax.broadcasted_iota(I32, (n_rows, 1), 0)
        for h in range(N_HEADS):
            srow = jnp.where(r1 // n == h, slope[:, h * HEAD_DIM:h * HEAD_DIM + 1], srow)
        bias_ref[...] = jnp.where(cnt > 0.0, -srow * dist.astype(F32), NEG_INF)

    qa = qa_ref[...]
    q_rows = jnp.concatenate([qa] * N_HEADS, axis=0)
    r2 = lax.broadcasted_iota(I32, (n_rows, WIDTH), 0)
    c2 = lax.broadcasted_iota(I32, (n_rows, WIDTH), 1)
    q_blk = jnp.where(r2 // n == c2 // HEAD_DIM, q_rows, 0.0).astype(BF16)
    pad = jnp.zeros((LANES - n, WIDTH), F32)
    k_new = jnp.concatenate([ka_ref[...], pad], axis=0).astype(BF16)
    v_new = jnp.concatenate([va_ref[...], pad], axis=0).astype(BF16)
    s_old = _dot(q_blk, ck_ref[0].astype(BF16))
    s_new = _dot_nt(q_blk, k_new)
    s = jnp.concatenate([s_old, s_new], axis=1) * scale + bias_ref[...]
    m = jnp.max(s, axis=-1, keepdims=True)
    p = cnt_ref[...] * jnp.exp(s - m)
    den = jnp.sum(p, axis=-1, keepdims=True)
    p = p.astype(BF16)
    o_all = _dot_nt(p[:, :w_buf], cv_ref[0].astype(BF16)) + _dot(p[:, w_buf:], v_new)
    o_all = o_all / den
    heads = [o_all[h * n:(h + 1) * n, h * HEAD_DIM:(h + 1) * HEAD_DIM] for h in range(N_HEADS)]
    a_ref[...] = jnp.concatenate(heads, axis=1)


def _sample_mixer(lg_lane, gout, slope_lane, qr, kr, vr, gr, qa, ka, va, state, ck, cv, n_new):
    b = state.shape[0]
    w_buf = ck.shape[2]
    tok = pl.BlockSpec((n_new, WIDTH), lambda i: (i, 0))
    tab = pl.BlockSpec((1, WIDTH), lambda i: (0, 0))
    st = pl.BlockSpec((1, N_HEADS, HEAD_DIM, HEAD_DIM), lambda i: (i, 0, 0, 0))
    cache = pl.BlockSpec((1, WIDTH, w_buf), lambda i: (i, 0, 0))
    n_rows = N_HEADS * n_new
    return pl.pallas_call(
        functools.partial(_sample_mixer_kernel, n_new=n_new, w_buf=w_buf),
        grid=(b,),
        in_specs=[tab, tab, tab, tok, tok, tok, tok, tok, tok, tok, st, cache, cache],
        out_specs=[tok, tok, st],
        out_shape=[jax.ShapeDtypeStruct((b * n_new, WIDTH), F32),
                   jax.ShapeDtypeStruct((b * n_new, WIDTH), F32),
                   jax.ShapeDtypeStruct(state.shape, F32)],
        scratch_shapes=[pltpu.VMEM((n_rows, w_buf + LANES), F32),
                        pltpu.VMEM((n_rows, w_buf + LANES), F32)],
        compiler_params=_cparams(1),
        name="sample_mixer",
    )(lg_lane, gout, slope_lane, qr, kr, vr, gr, qa, ka, va, state, ck, cv)


def _mem_kv_kernel(x_ref, g_ref, wk_ref, wv_ref, gk_ref, k_ref, v_ref, kb_ref, vb_ref):
    h = _rms(x_ref[...], g_ref[...]).astype(BF16)
    k = _dot(h, wk_ref[...])
    gk = gk_ref[...]
    v = _dot(h, wv_ref[...])
    vb_ref[...] = v.astype(BF16)
    for hd in range(N_MEM_HEADS):
        hs = slice(hd * MEM_HEAD_DIM, (hd + 1) * MEM_HEAD_DIM)
        kh = _rms(k[:, hs], gk)
        k_ref[:, hd, :] = kh
        kb_ref[:, hs] = kh.astype(BF16)
        v_ref[:, hd, :] = v[:, hs]


def _mem_kv(mem2d, g_mem, w_mk_bf, w_mv_bf, g_k_mem, tm=256):
    t = mem2d.shape[0]
    const = lambda i: (0, 0)
    row = lambda i: (i, 0)
    out = jax.ShapeDtypeStruct((t, N_MEM_HEADS, MEM_HEAD_DIM), F32)
    return pl.pallas_call(
        _mem_kv_kernel,
        grid=(t // tm,),
        in_specs=[pl.BlockSpec((tm, D_MODEL), row), pl.BlockSpec((1, D_MODEL), const),
                  pl.BlockSpec((D_MODEL, D_MODEL), const), pl.BlockSpec((D_MODEL, D_MODEL), const),
                  pl.BlockSpec((1, MEM_HEAD_DIM), const)],
        out_specs=[pl.BlockSpec((tm, N_MEM_HEADS, MEM_HEAD_DIM), lambda i: (i, 0, 0))] * 2
        + [pl.BlockSpec((tm, D_MODEL), row)] * 2,
        out_shape=[out, out] + [jax.ShapeDtypeStruct((t, D_MODEL), BF16)] * 2,
        compiler_params=_cparams(1),
        name="mem_kv",
    )(mem2d, g_mem, w_mk_bf, w_mv_bf, g_k_mem)


def _cross_router_kernel(x_ref, r_ref, a_ref, wo_ref, gc_ref, wmq_ref, gqm_ref, mk_ref, mv_ref,
                         wmo_ref, gmoe_ref, wrh_ref, wrl_ref, br_ref, cin_ref,
                         y_ref, h_ref, gate_ref, *rest, tq, n_split, sorted_mode, native_mem):
    n_scratch = 3 if native_mem else 1
    if native_mem:
        carry_ref, mem_ref, msem = rest[-n_scratch:]
    else:
        carry_ref, = rest[-n_scratch:]
    if sorted_mode:
        lpos_ref, tab_ref, cnt_ref = rest[:-n_scratch]
    else:
        pair_ref, cnt_ref = rest[:-n_scratch]
    first = (pl.program_id(0) == 0) & (pl.program_id(1) == 0)

    @pl.when(first)
    def _():
        carry_ref[...] = cin_ref[...]

    half = D_MODEL // 2
    gqm = gqm_ref[...]
    head_slices = [slice(hd * MEM_HEAD_DIM, (hd + 1) * MEM_HEAD_DIM) for hd in range(N_MEM_HEADS)]
    if native_mem:
        n_mem = mem_ref.shape[1]
        b0 = pl.program_id(0) * n_mem
        copies = [pltpu.make_async_copy(src.at[b0 + j, :, hd, :], mem_ref.at[kv, j, hd], msem)
                  for kv, src in enumerate((mk_ref, mv_ref))
                  for j in range(n_mem) for hd in range(N_MEM_HEADS)]
        for cp in copies:
            cp.start()
        for cp in copies:
            cp.wait()
        mem_k = [[mem_ref[0, j, hd].astype(BF16) for hd in range(N_MEM_HEADS)] for j in range(n_mem)]
        mem_v = [[mem_ref[1, j, hd].astype(BF16) for hd in range(N_MEM_HEADS)] for j in range(n_mem)]
    else:
        n_mem = mk_ref.shape[0]
        mem_k = [[mk_ref[j, :, hs].astype(BF16) for hs in head_slices] for j in range(n_mem)]
        mem_v = [[mv_ref[j, :, hs].astype(BF16) for hs in head_slices] for j in range(n_mem)]

    def attend(q, j):
        heads = []
        for hd, hs in enumerate(head_slices):
            qh = _rms(q[:, hs], gqm).astype(BF16)
            s = _dot_nt(qh, mem_k[j][hd]) * (MEM_HEAD_DIM ** -0.5)
            m = jnp.max(s, axis=-1, keepdims=True)
            p = jnp.exp(s - m)
            p = p / jnp.sum(p, axis=-1, keepdims=True)
            heads.append(_dot(p.astype(BF16), mem_v[j][hd]))
        return jnp.concatenate(heads, axis=1)

    def rows_block(rs):
        ra = jnp.concatenate([r_ref[rs, :], a_ref[rs, :]], axis=1).astype(BF16)
        y1 = x_ref[rs, :] + _dot(ra, wo_ref[...])
        q = _dot(_rms(y1, gc_ref[...]).astype(BF16), wmq_ref[...])
        if n_mem == 1:
            o = attend(q, 0)
        else:
            per = q.shape[0] // n_mem
            o = jnp.concatenate([attend(q[j * per:(j + 1) * per], j) for j in range(n_mem)], axis=0)
        y2 = y1 + _dot(o.astype(BF16), wmo_ref[...])
        y_ref[rs, :] = y2
        hm = _rms(y2, gmoe_ref[...])
        if sorted_mode:
            h_ref[rs, :] = hm.astype(BF16)
        else:
            h_ref[rs, :] = pltpu.pack_elementwise([hm[:, :half], hm[:, half:]], packed_dtype=BF16)
        hi = hm.astype(BF16)
        lo = (hm - hi.astype(F32)).astype(BF16)
        return _dot(hi, wrh_ref[...]) + _dot(lo, wrh_ref[...]) + _dot(hi, wrl_ref[...])

    sub = tq // n_split
    logits = jnp.concatenate([rows_block(pl.ds(i * sub, sub)) for i in range(n_split)], axis=0)
    logits = logits + br_ref[...]

    rb = min(tq, ROUTE_BLOCK)
    lane = lax.broadcasted_iota(I32, (rb, LANES), 1)
    lanef = lane.astype(F32)
    big = float(LANES)
    is_grp = lane < N_GROUPS
    ti = lax.broadcasted_iota(I32, (rb, rb), 0)
    tj = lax.broadcasted_iota(I32, (rb, rb), 1)
    tri = jnp.where(tj < ti, 1.0, 0.0)
    ui = lax.broadcasted_iota(I32, (LANES, LANES), 0)
    uj = lax.broadcasted_iota(I32, (LANES, LANES), 1)
    upper = jnp.where(ui < uj, 1.0, 0.0).astype(BF16)

    def pick(hit, table):
        return jnp.sum(jnp.where(hit, table, 0.0), axis=-1, keepdims=True)

    def lanes_of(vals):
        out = jnp.zeros((rb, LANES), F32)
        for idx, val in enumerate(vals):
            out = jnp.where(lane == idx, val, out)
        return out

    for blk_i in range(tq // rb):
        rs = slice(blk_i * rb, (blk_i + 1) * rb)
        lg = logits[rs]
        gl = jnp.where(is_grp, lg, NEG_INF)
        gmax = jnp.max(gl, axis=-1, keepdims=True)
        gidx = jnp.min(jnp.where(gl == gmax, lanef, big), axis=-1, keepdims=True)
        p_top = 1.0 / jnp.sum(jnp.where(is_grp, jnp.exp(lg - gmax), 0.0), axis=-1, keepdims=True)
        lo = EXPERT_LANE0 + N_EXP_PER_GROUP * gidx
        in_grp = (lanef >= lo) & (lanef < lo + N_EXP_PER_GROUP)
        el = jnp.where(in_grp, lg, NEG_INF)
        v1 = jnp.max(el, axis=-1, keepdims=True)
        i1 = jnp.min(jnp.where(el == v1, lanef, big), axis=-1, keepdims=True)
        el2 = jnp.where(lanef == i1, NEG_INF, el)
        v2 = jnp.max(el2, axis=-1, keepdims=True)
        i2 = jnp.min(jnp.where(el2 == v2, lanef, big), axis=-1, keepdims=True)
        e21 = jnp.exp(v2 - v1)
        w1 = p_top / (1.0 + e21)
        w2 = p_top * e21 / (1.0 + e21)

        hit1 = lanef == i1
        hit2 = lanef == i2
        onehot = jnp.where(hit1 | hit2, 1.0, 0.0)
        if rb >= 16:
            within = _dot(tri.astype(BF16), onehot.astype(BF16))
        else:
            within = _dot(tri, onehot)
        carry = carry_ref[...]
        count = jnp.sum(onehot, axis=0, keepdims=True)

        if sorted_mode:
            eighths = jnp.floor((count + (RUN_ALIGN - 1.0)) * (1.0 / RUN_ALIGN))
            carry_ref[...] = carry + eighths * float(RUN_ALIGN)
            lstart = (_dot(jnp.broadcast_to(eighths, (8, LANES)).astype(BF16), upper)[0:1]
                      * float(RUN_ALIGN))
            lpos1 = pick(hit1, within + lstart)
            lpos2 = pick(hit2, within + lstart)
            gate_ref[rs, :] = lanes_of((w1, w2, lpos1, lpos2))[:, :4]
            lpos_ref[blk_i] = lanes_of((lpos1, lpos2)).T[:8]
            tab_ref[blk_i] = jnp.concatenate(
                [count, lstart, carry, jnp.zeros((5, LANES), F32)], axis=0)
        else:
            carry_ref[...] = carry + count
            e1 = i1 - EXPERT_LANE0
            e2 = i2 - EXPERT_LANE0
            rank1 = pick(hit1, within + carry)
            rank2 = pick(hit2, within + carry)
            gate_ref[rs, :] = lanes_of((w1, w2))[:, :2]
            pair_ref[rs, :] = lanes_of((e1, e2, rank1, rank2))[:, :4].astype(I32)
    cnt_ref[...] = carry_ref[...]


def _cross_router(x2d, r, a, w_o, g_cross, w_mq, g_q_mem, mk, mv, w_mo, g_moe, w_r_hi, w_r_lo, b_r,
                  cnt_in, n_batch, tq, n_split, sorted_mode):
    t = x2d.shape[0]
    rb = min(tq, ROUTE_BLOCK)
    n_rb = tq // rb
    mem_per_step = max(1, tq // (t // n_batch))
    n_batch = n_batch // mem_per_step
    assert mem_per_step == 1 or n_split == 1
    native_mem = mk.ndim == 4
    scratch = [pltpu.VMEM((1, LANES), F32)]
    if native_mem:
        scratch += [pltpu.VMEM((2, mem_per_step, N_MEM_HEADS, N_MEM, MEM_HEAD_DIM), F32),
                    pltpu.SemaphoreType.DMA(())]
    per_b = t // n_batch // tq
    row = lambda i, j: (i * per_b + j, 0)
    const = lambda i, j: (0, 0)
    if native_mem:
        mem = pl.BlockSpec(memory_space=pl.ANY)
    else:
        mem = pl.BlockSpec((mem_per_step, N_MEM, D_MODEL), lambda i, j: (i, 0, 0))
    wspec = pl.BlockSpec((D_MODEL, D_MODEL), const)
    vec = pl.BlockSpec((1, D_MODEL), const)
    rspec = pl.BlockSpec((D_MODEL, LANES), const)
    blk3 = lambda i, j: (i * per_b + j, 0, 0)
    if sorted_mode:
        out_specs = [pl.BlockSpec((tq, D_MODEL), row), pl.BlockSpec((tq, D_MODEL), row),
                     pl.BlockSpec((tq, 4), row), pl.BlockSpec((n_rb, 8, rb), blk3),
                     pl.BlockSpec((n_rb, 8, LANES), blk3)]
        out_shape = [jax.ShapeDtypeStruct((t, D_MODEL), F32),
                     jax.ShapeDtypeStruct((t, D_MODEL), BF16),
                     jax.ShapeDtypeStruct((t, 4), F32),
                     jax.ShapeDtypeStruct((t // rb, 8, rb), F32),
                     jax.ShapeDtypeStruct((t // rb, 8, LANES), F32)]
    else:
        out_specs = [pl.BlockSpec((tq, D_MODEL), row), pl.BlockSpec((tq, D_MODEL // 2), row),
                     pl.BlockSpec((tq, 2), row), pl.BlockSpec((tq, 4), row)]
        out_shape = [jax.ShapeDtypeStruct((t, D_MODEL), F32),
                     jax.ShapeDtypeStruct((t, D_MODEL // 2), jnp.uint32),
                     jax.ShapeDtypeStruct((t, 2), F32),
                     jax.ShapeDtypeStruct((t, 4), I32)]
    return pl.pallas_call(
        functools.partial(_cross_router_kernel, tq=tq, n_split=n_split, sorted_mode=sorted_mode,
                          native_mem=native_mem),
        grid=(n_batch, per_b),
        in_specs=[pl.BlockSpec((tq, D_MODEL), row), pl.BlockSpec((tq, WIDTH), row),
                  pl.BlockSpec((tq, WIDTH), row), wspec, vec, wspec,
                  pl.BlockSpec((1, MEM_HEAD_DIM), const), mem, mem, wspec, vec,
                  rspec, rspec, pl.BlockSpec((1, LANES), const),
                  pl.BlockSpec((1, LANES), const)],
        out_specs=out_specs + [pl.BlockSpec((1, LANES), const)],
        out_shape=out_shape + [jax.ShapeDtypeStruct((1, LANES), F32)],
        scratch_shapes=scratch,
        compiler_params=_cparams(2),
        name="cross_router",
    )(x2d, r, a, w_o, g_cross, w_mq, g_q_mem, mk, mv, w_mo, g_moe, w_r_hi, w_r_lo, b_r, cnt_in)


def _row_copy(src, src_row, dst, dst_row, sem):
    return pltpu.make_async_copy(src.at[pl.ds(src_row, 1)], dst.at[pl.ds(dst_row, 1)], sem)


def _pair_slot(off_ref, pairs_ref, j, kk):
    return off_ref[pairs_ref[0, 0, 4 * j + kk]] + pairs_ref[0, 0, 4 * j + 2 + kk]


def _dispatch_kernel(off_ref, pairs_ref, hp_ref, xs_in_ref, xs_ref, sem, *, tq):
    del xs_in_ref

    def start(j, carry):
        for kk in range(2):
            _row_copy(hp_ref, j, xs_ref, _pair_slot(off_ref, pairs_ref, j, kk), sem).start()
        return carry

    def wait(j, carry):
        for kk in range(2):
            _row_copy(hp_ref, j, xs_ref, _pair_slot(off_ref, pairs_ref, j, kk), sem).wait()
        return carry

    lax.fori_loop(0, tq, start, 0, unroll=8)
    lax.fori_loop(0, tq, wait, 0, unroll=8)


def _dispatch(offsets, pairs, hp, xs, tq):
    t = hp.shape[0]
    return pl.pallas_call(
        functools.partial(_dispatch_kernel, tq=tq),
        grid_spec=pltpu.PrefetchScalarGridSpec(
            num_scalar_prefetch=1, grid=(t // tq,),
            in_specs=[pl.BlockSpec((1, 1, 4 * tq), lambda i, off: (i, 0, 0),
                                   memory_space=pltpu.SMEM),
                      pl.BlockSpec((tq, D_MODEL // 2), lambda i, off: (i, 0)),
                      pl.BlockSpec(memory_space=pl.ANY)],
            out_specs=pl.BlockSpec(memory_space=pl.ANY),
            scratch_shapes=[pltpu.SemaphoreType.DMA(())]),
        out_shape=jax.ShapeDtypeStruct(xs.shape, xs.dtype),
        input_output_aliases={3: 0},
        compiler_params=pltpu.CompilerParams(dimension_semantics=("arbitrary",),
                                             has_side_effects=True),
        name="moe_dispatch",
    )(offsets, pairs.reshape(t // tq, 1, 4 * tq), hp, xs)


def _unpack_pair(words):
    lo = pltpu.unpack_elementwise(words, index=0, packed_dtype=BF16, unpacked_dtype=F32)
    hi = pltpu.unpack_elementwise(words, index=1, packed_dtype=BF16, unpacked_dtype=F32)
    return lo, hi


def _expert_kernel(te_ref, nt_ref, xs_ref, wg_ref, wu_ref, wd_ref, ys_ref):
    del te_ref
    half = D_MODEL // 2

    @pl.when(pl.program_id(0) < nt_ref[0])
    def _():
        wg = wg_ref[0].astype(BF16)
        wu = wu_ref[0].astype(BF16)
        wd = wd_ref[0].astype(BF16)
        sub = xs_ref.shape[0] // EXPERT_SPLIT
        for i in range(EXPERT_SPLIT):
            rs = pl.ds(i * sub, sub)
            lo, hi = _unpack_pair(xs_ref[rs, :])
            lo = lo.astype(BF16)
            hi = hi.astype(BF16)
            g = _dot(lo, wg[:half]) + _dot(hi, wg[half:])
            u = _dot(lo, wu[:half]) + _dot(hi, wu[half:])
            hid = (g * (1.0 / (1.0 + jnp.exp(-g))) * u).astype(BF16)
            y = _dot(hid, wd)
            ys_ref[rs, :] = pltpu.pack_elementwise([y[:, :half], y[:, half:]], packed_dtype=BF16)

    @pl.when(pl.program_id(0) >= nt_ref[0])
    def _():
        ys_ref[...] = jnp.zeros_like(ys_ref)


def _experts(tile_expert, n_tiles, xs, w_gate, w_up, w_down):
    rows = xs.shape[0]
    n_max = rows // MOE_ROW_TILE

    def xmap(i, te, nt):
        return (jnp.maximum(jnp.minimum(i, nt[0] - 1), 0), 0)

    def wmap(i, te, nt):
        return (te[i], 0, 0)

    return pl.pallas_call(
        _expert_kernel,
        grid_spec=pltpu.PrefetchScalarGridSpec(
            num_scalar_prefetch=2,
            grid=(n_max,),
            in_specs=[pl.BlockSpec((MOE_ROW_TILE, D_MODEL // 2), xmap),
                      pl.BlockSpec((1, D_MODEL, D_EXPERT), wmap),
                      pl.BlockSpec((1, D_MODEL, D_EXPERT), wmap),
                      pl.BlockSpec((1, D_EXPERT, D_MODEL), wmap)],
            out_specs=pl.BlockSpec((MOE_ROW_TILE, D_MODEL // 2), lambda i, te, nt: (i, 0)),
        ),
        out_shape=jax.ShapeDtypeStruct(xs.shape, xs.dtype),
        compiler_params=_cparams(1),
        name="moe_experts",
    )(tile_expert, n_tiles, xs, w_gate, w_up, w_down)


def _combine_kernel(off_ref, pairs_ref, pairs_next_ref, ys_ref, y_ref, gate_ref, o_ref,
                    buf_ref, sem, *, tq):
    step = pl.program_id(0)
    n_steps = pl.num_programs(0)

    def gather(p_ref, slot, wait):
        def body(j, carry):
            for kk in range(2):
                cp = pltpu.make_async_copy(ys_ref.at[pl.ds(_pair_slot(off_ref, p_ref, j, kk), 1)],
                                           buf_ref.at[slot, kk, pl.ds(j, 1)], sem.at[slot])
                cp.wait() if wait else cp.start()
            return carry
        lax.fori_loop(0, tq, body, 0, unroll=8)

    @pl.when(step == 0)
    def _():
        gather(pairs_ref, 0, wait=False)

    @pl.when(step + 1 < n_steps)
    def _():
        gather(pairs_next_ref, (step + 1) % 2, wait=False)

    slot = step % 2
    gather(pairs_ref, slot, wait=True)
    half = D_MODEL // 2
    gates = gate_ref[...]
    w1 = gates[:, 0:1]
    w2 = gates[:, 1:2]
    lo1, hi1 = _unpack_pair(buf_ref[slot, 0])
    lo2, hi2 = _unpack_pair(buf_ref[slot, 1])
    y = y_ref[...]
    o_ref[:, :half] = y[:, :half] + (w1 * lo1 + w2 * lo2)
    o_ref[:, half:] = y[:, half:] + (w1 * hi1 + w2 * hi2)


def _combine(offsets, pairs, ys, y2, gates, tq):
    t = y2.shape[0]
    n = t // tq
    row = lambda i, off: (i, 0)
    smem = lambda f: pl.BlockSpec((1, 1, 4 * tq), f, memory_space=pltpu.SMEM)
    pairs3 = pairs.reshape(n, 1, 4 * tq)
    return pl.pallas_call(
        functools.partial(_combine_kernel, tq=tq),
        grid_spec=pltpu.PrefetchScalarGridSpec(
            num_scalar_prefetch=1, grid=(n,),
            in_specs=[smem(lambda i, off: (i, 0, 0)),
                      smem(lambda i, off: (jnp.minimum(i + 1, n - 1), 0, 0)),
                      pl.BlockSpec(memory_space=pl.ANY),
                      pl.BlockSpec((tq, D_MODEL), row), pl.BlockSpec((tq, 2), row)],
            out_specs=pl.BlockSpec((tq, D_MODEL), row),
            scratch_shapes=[pltpu.VMEM((2, 2, tq, D_MODEL // 2), jnp.uint32),
                            pltpu.SemaphoreType.DMA((2,))]),
        out_shape=jax.ShapeDtypeStruct((t, D_MODEL), F32),
        compiler_params=_cparams(1),
        name="moe_combine",
    )(offsets, pairs3, pairs3, ys, y2, gates)


def _for_each_chunk(tot_ref, rows_ref, blk, fn):
    base = blk * SORT_CHUNKS

    def body(k, carry):
        fn(pl.multiple_of(k * RUN_ALIGN, RUN_ALIGN), pl.multiple_of(rows_ref[base + k], RUN_ALIGN))
        return carry

    lax.fori_loop(0, tot_ref[blk], body, 0)


def _sorted_dispatch_kernel(tot_ref, rows_ref, zt_ref, nt_ref, h_ref, lpos_ref, xs_ref,
                            stage_ref, zero_ref, sem, zsem, *, n_zero, n_tiles_max):
    step = pl.program_id(0)
    n_steps = pl.num_programs(0)

    @pl.when(step == 0)
    def _():
        zero_ref[...] = jnp.zeros_like(zero_ref)

        def fill(granule):
            row = pl.multiple_of(granule * ZERO_ROWS, ZERO_ROWS)
            return pltpu.make_async_copy(zero_ref, xs_ref.at[pl.ds(row, ZERO_ROWS)], zsem)

        def tail(do):
            def body(tile, carry):
                do(fill(tile))
                return carry
            lax.fori_loop(nt_ref[0], n_tiles_max, body, 0)

        for i in range(n_zero):
            pl.when(zt_ref[i] >= 0)(fill(jnp.maximum(zt_ref[i], 0)).start)
        tail(lambda cp: cp.start())
        for i in range(n_zero):
            pl.when(zt_ref[i] >= 0)(fill(jnp.maximum(zt_ref[i], 0)).wait)
        tail(lambda cp: cp.wait())

    half = D_MODEL // 2
    n_slots = lpos_ref.shape[0]
    rb = h_ref.shape[0] // n_slots
    jrow = lax.broadcasted_iota(I32, (SORT_ROWS, rb), 0).astype(F32)

    def copy(s):
        return lambda l, g: pltpu.make_async_copy(stage_ref.at[s, pl.ds(l, RUN_ALIGN)],
                                                  xs_ref.at[pl.ds(g, RUN_ALIGN)], sem.at[s])

    for s in range(n_slots):
        blk = step * n_slots + s

        @pl.when(step > 0)
        def _():
            _for_each_chunk(tot_ref, rows_ref, blk - n_slots, lambda l, g: copy(s)(l, g).wait())

        lpos = lpos_ref[s]
        perm = jnp.where((jrow == lpos[0:1]) | (jrow == lpos[1:2]), 1.0, 0.0).astype(BF16)
        x = _dot(perm, h_ref[s * rb:(s + 1) * rb, :])
        stage_ref[s] = pltpu.pack_elementwise([x[:, :half], x[:, half:]], packed_dtype=BF16)
        _for_each_chunk(tot_ref, rows_ref, blk, lambda l, g: copy(s)(l, g).start())

    @pl.when(step == n_steps - 1)
    def _():
        for s in range(n_slots):
            _for_each_chunk(tot_ref, rows_ref, step * n_slots + s,
                            lambda l, g: copy(s)(l, g).wait())


def _sorted_dispatch(chunk_total, chunk_rows, zero_tiles, n_tiles, h_bf, lpos_t, n_rows):
    n_blocks, _, rb = lpos_t.shape
    per_step = SORT_BLOCKS_PER_STEP
    return pl.pallas_call(
        functools.partial(_sorted_dispatch_kernel, n_zero=zero_tiles.shape[0],
                          n_tiles_max=n_rows // ZERO_ROWS),
        grid_spec=pltpu.PrefetchScalarGridSpec(
            num_scalar_prefetch=4, grid=(n_blocks // per_step,),
            in_specs=[pl.BlockSpec((per_step * rb, D_MODEL), lambda i, *_: (i, 0)),
                      pl.BlockSpec((per_step, 8, rb), lambda i, *_: (i, 0, 0))],
            out_specs=pl.BlockSpec(memory_space=pl.ANY),
            scratch_shapes=[pltpu.VMEM((per_step, SORT_ROWS, D_MODEL // 2), jnp.uint32),
                            pltpu.VMEM((ZERO_ROWS, D_MODEL // 2), jnp.uint32),
                            pltpu.SemaphoreType.DMA((per_step,)), pltpu.SemaphoreType.DMA(())]),
        out_shape=jax.ShapeDtypeStruct((n_rows, D_MODEL // 2), jnp.uint32),
        compiler_params=pltpu.CompilerParams(dimension_semantics=("arbitrary",),
                                             has_side_effects=True, vmem_limit_bytes=VMEM_LIMIT),
        name="moe_sorted_dispatch",
    )(chunk_total, chunk_rows, zero_tiles, n_tiles, h_bf, lpos_t)


def _sorted_combine_kernel(tot_ref, rows_ref, ys_ref, y_ref, gate_ref, o_ref, buf_ref, sem):
    step = pl.program_id(0)
    n_steps = pl.num_programs(0)

    def gather(blk, s, wait):
        def fn(l, g):
            cp = pltpu.make_async_copy(ys_ref.at[pl.ds(g, RUN_ALIGN)],
                                       buf_ref.at[s, pl.ds(l, RUN_ALIGN)], sem.at[s])
            cp.wait() if wait else cp.start()
        _for_each_chunk(tot_ref, rows_ref,blk, fn)

    per_step = SORT_BLOCKS_PER_STEP
    n_slots = 2 * per_step
    first = step * per_step

    @pl.when(step == 0)
    def _():
        buf_ref[...] = jnp.zeros_like(buf_ref)
        for u in range(per_step):
            gather(u, u, wait=False)

    @pl.when(step + 1 < n_steps)
    def _():
        for u in range(per_step):
            gather(first + per_step + u, (first + per_step + u) % n_slots, wait=False)

    half = D_MODEL // 2
    rb = y_ref.shape[0] // per_step
    jcol = lax.broadcasted_iota(I32, (rb, SORT_ROWS), 1).astype(F32)
    for u in range(per_step):
        slot = (first + u) % n_slots
        gather(first + u, slot, wait=True)
        rs = slice(u * rb, (u + 1) * rb)
        gates = gate_ref[rs, :]
        pick1 = jnp.where(jcol == gates[:, 2:3], 1.0, 0.0).astype(BF16)
        pick2 = jnp.where(jcol == gates[:, 3:4], 1.0, 0.0).astype(BF16)
        lo, hi = _unpack_pair(buf_ref[slot])
        lo = lo.astype(BF16)
        hi = hi.astype(BF16)
        w1 = gates[:, 0:1]
        w2 = gates[:, 1:2]
        y = y_ref[rs, :]
        o_ref[rs, :half] = y[:, :half] + (w1 * _dot(pick1, lo) + w2 * _dot(pick2, lo))
        o_ref[rs, half:] = y[:, half:] + (w1 * _dot(pick1, hi) + w2 * _dot(pick2, hi))


def _sorted_combine(chunk_total, chunk_rows, ys, y2, gates, rb):
    t = y2.shape[0]
    row = lambda i, *_: (i, 0)
    tq = SORT_BLOCKS_PER_STEP * rb
    return pl.pallas_call(
        _sorted_combine_kernel,
        grid_spec=pltpu.PrefetchScalarGridSpec(
            num_scalar_prefetch=2, grid=(t // tq,),
            in_specs=[pl.BlockSpec(memory_space=pl.ANY),
                      pl.BlockSpec((tq, D_MODEL), row), pl.BlockSpec((tq, 4), row)],
            out_specs=pl.BlockSpec((tq, D_MODEL), row),
            scratch_shapes=[pltpu.VMEM((2 * SORT_BLOCKS_PER_STEP, SORT_ROWS, D_MODEL // 2),
                                       jnp.uint32),
                            pltpu.SemaphoreType.DMA((2 * SORT_BLOCKS_PER_STEP,))]),
        out_shape=jax.ShapeDtypeStruct((t, D_MODEL), F32),
        compiler_params=_cparams(1),
        name="moe_sorted_combine",
    )(chunk_total, chunk_rows, ys, y2, gates)


def _lane_table(per_head):
    lanes = jnp.repeat(per_head.astype(F32), HEAD_DIM)
    return lanes.reshape(N_PAIRS, 1, LANES), lanes.reshape(1, WIDTH)


def kernel(x_prompt, x_sample, mem_prompt, state_ret, cache_win_k, cache_win_v, cache_mem_k,
           cache_mem_v, g_mix, w_in, g_ret_out, g_q_att, g_k_att, w_o, g_cross, g_mem, w_mq, w_mk,
           w_mv, g_q_mem, g_k_mem, w_mo, g_moe, w_router_group, b_router_group, w_router_expert,
           b_router_expert, w_exp_gate, w_exp_up, w_exp_down):
    depth = g_mix.shape[0]
    assert depth == 1
    b, s, _ = x_prompt.shape
    bs, ns, _ = x_sample.shape
    t_p, t_s = b * s, bs * ns
    l = 0

    heads = jnp.arange(N_HEADS, dtype=F32)
    lg_pairs, lg_lane = _lane_table(jnp.log(1.0 - 2.0 ** (-5.0 - heads)))
    slope_pairs, slope_lane = _lane_table(2.0 ** (-8.0 * (heads + 1.0) / N_HEADS))
    gout_lane = g_ret_out[l].reshape(1, WIDTH)
    gout_pairs = g_ret_out[l].reshape(N_PAIRS, 1, LANES)
    gq_t = jnp.tile(g_q_att[l], N_HEADS).reshape(1, WIDTH)
    gk_t = jnp.tile(g_k_att[l], N_HEADS).reshape(1, WIDTH)
    vec = lambda v: v.reshape(1, -1)
    w_in_bf = w_in[l].astype(BF16)
    w_o_bf, w_mq_bf, w_mk_bf = w_o[l].astype(BF16), w_mq[l].astype(BF16), w_mk[l].astype(BF16)
    w_mv_bf, w_mo_bf = w_mv[l].astype(BF16), w_mo[l].astype(BF16)
    gap = EXPERT_LANE0 - N_GROUPS
    tail = LANES - EXPERT_LANE0 - N_EXPERTS
    w_r = jnp.concatenate(
        [w_router_group[l], jnp.zeros((D_MODEL, gap), F32),
         jnp.moveaxis(w_router_expert[l], 0, 1).reshape(D_MODEL, N_EXPERTS),
         jnp.zeros((D_MODEL, tail), F32)], axis=1)
    b_r = jnp.concatenate([b_router_group[l], jnp.zeros((gap,), F32),
                           b_router_expert[l].reshape(-1), jnp.zeros((tail,), F32)]).reshape(1, LANES)
    w_r_hi = w_r.astype(BF16)
    w_r_lo = (w_r - w_r_hi.astype(F32)).astype(BF16)

    xp = x_prompt.reshape(t_p, D_MODEL)
    qr, kr, vr, gr, qa, ka, va, ka_t, va_t = _mix_proj(xp, vec(g_mix[l]), w_in_bf, gq_t, gk_t,
                                                       tm=1024, act_dtype=BF16, seq=s)
    shp = lambda z: z.reshape(b, s, WIDTH)
    r_p, state_p = _retention(shp(qr), shp(kr), shp(vr), shp(gr), lg_pairs, gout_pairs)
    a_p = _dilated_attention(shp(qa), shp(ka), shp(va), slope_pairs)
    mk_p, mv_p, mk_bf, mv_bf = _mem_kv(mem_prompt.reshape(b * N_MEM, D_MODEL), vec(g_mem[l]), w_mk_bf, w_mv_bf,
                          vec(g_k_mem[l]))
    zero_cnt = jnp.zeros((1, LANES), F32)
    y2_p, h_p, gates_p, lpos_p, tabs_p, cnt_p = _cross_router(
        xp, r_p.reshape(t_p, WIDTH), a_p.reshape(t_p, WIDTH), w_o_bf, vec(g_cross[l]), w_mq_bf,
        vec(g_q_mem[l]), mk_bf.reshape(b, N_MEM, D_MODEL), mv_bf.reshape(b, N_MEM, D_MODEL), w_mo_bf,
        vec(g_moe[l]), w_r_hi, w_r_lo, b_r, zero_cnt, n_batch=b, tq=1024, n_split=4,
        sorted_mode=True)

    xs_ = x_sample.reshape(t_s, D_MODEL)
    qr, kr, vr, gr, qa, ka_s, va_s = _mix_proj(xs_, vec(g_mix[l]), w_in_bf, gq_t, gk_t, tm=t_s,
                                               act_dtype=F32)
    pos_minor = lambda c: jnp.transpose(c, (0, 2, 3, 1)).reshape(bs, WIDTH, c.shape[1])
    r_s, a_s, state_s = _sample_mixer(lg_lane, gout_lane, slope_lane, qr, kr, vr, gr, qa, ka_s, va_s,
                                      state_ret[l], pos_minor(cache_win_k[l]),
                                      pos_minor(cache_win_v[l]), n_new=ns)
    y2_s, hp_s, gates_s, pairs_s, cnt_all = _cross_router(
        xs_, r_s, a_s, w_o_bf, vec(g_cross[l]), w_mq_bf, vec(g_q_mem[l]),
        cache_mem_k[l], cache_mem_v[l],
        w_mo_bf, vec(g_moe[l]), w_r_hi, w_r_lo, b_r, cnt_p, n_batch=bs, tq=min(bs, 8) * ns, n_split=1,
        sorted_mode=False)

    tile = MOE_ROW_TILE
    lanes_e = slice(EXPERT_LANE0, EXPERT_LANE0 + N_EXPERTS)
    counts = cnt_all[0, lanes_e].astype(I32)
    tiles_per = (counts + tile - 1) // tile
    tile_end = jnp.cumsum(tiles_per)
    offsets = (tile_end - tiles_per) * tile
    n_tiles = tile_end[-1:]
    n_blocks = t_p // ROUTE_BLOCK
    n_max = (2 * (t_p + t_s) + n_blocks * N_EXPERTS * (RUN_ALIGN - 1)) // tile + N_EXPERTS
    tile_ids = jnp.minimum(jnp.arange(n_max, dtype=I32), n_tiles[0] - 1)
    tile_expert = jnp.sum((tile_end[None, :] <= tile_ids[:, None]).astype(I32), axis=1)
    seg_end = tile_end * tile
    first_gran = (offsets + cnt_p[0, lanes_e].astype(I32)) // ZERO_ROWS
    n_gran = (t_s + tile) // ZERO_ROWS + 1
    gran = first_gran[None, :] + jnp.arange(n_gran, dtype=I32)[:, None]
    zero_tiles = jnp.where(gran * ZERO_ROWS < seg_end[None, :], gran, -1).reshape(-1)
    first_tail_gran = n_tiles * (tile // ZERO_ROWS)
    run_chunks = (tabs_p[:, 0, lanes_e].astype(I32) + RUN_ALIGN - 1) // RUN_ALIGN
    run_end = jnp.cumsum(run_chunks, axis=1)
    chunk_total = run_end[:, -1]
    chunk_id = jnp.arange(SORT_CHUNKS, dtype=I32)
    owner = (chunk_id[None, :, None] >= run_end[:, None, :]).astype(I32).sum(axis=2)
    owner_hot = owner[:, :, None] == jnp.arange(N_EXPERTS, dtype=I32)[None, None, :]
    run_row0 = offsets[None, :] + tabs_p[:, 2, lanes_e].astype(I32) - (run_end - run_chunks) * RUN_ALIGN
    chunk_rows = (jnp.where(owner_hot, run_row0[:, None, :], 0).sum(axis=2)
                  + chunk_id[None, :] * RUN_ALIGN).reshape(-1)
    xs_sorted = _sorted_dispatch(chunk_total, chunk_rows, zero_tiles, first_tail_gran, h_p, lpos_p,
                                 n_max * tile)
    xs_sorted = _dispatch(offsets, pairs_s, hp_s, xs_sorted, tq=t_s)
    ys_sorted = _experts(tile_expert, n_tiles, xs_sorted, w_exp_gate[l], w_exp_up[l], w_exp_down[l])
    y_p = _sorted_combine(chunk_total, chunk_rows, ys_sorted, y2_p, gates_p, rb=ROUTE_BLOCK)
    y_s = _combine(offsets, pairs_s, ys_sorted, y2_s, gates_s, tq=t_s)

    from_pos_minor = lambda z: jnp.transpose(z.reshape(b, N_HEADS, HEAD_DIM, s), (0, 3, 1, 2))[None]
    return (y_p.reshape(b, s, D_MODEL), y_s.reshape(bs, ns, D_MODEL),
            state_p[None],
            from_pos_minor(ka_t), from_pos_minor(va_t),
            mk_p.reshape(1, b, N_MEM, N_MEM_HEADS, MEM_HEAD_DIM),
            mv_p.reshape(1, b, N_MEM, N_MEM_HEADS, MEM_HEAD_DIM),
            state_s[None],
            ka_s.reshape(1, bs, ns, N_HEADS, HEAD_DIM), va_s.reshape(1, bs, ns, N_HEADS, HEAD_DIM))
```

```python
import functools

import jax
import jax.numpy as jnp
from jax import lax
from jax.experimental import pallas as pl
from jax.experimental.pallas import tpu as pltpu

F32 = jnp.float32
BF16 = jnp.bfloat16
I32 = jnp.int32

D_MODEL = 1024
HEAD_DIM = 64
N_HEADS = 8
WIDTH = N_HEADS * HEAD_DIM
N_PAIRS = N_HEADS // 2
IN_COLS = 7 * WIDTH
RET_CHUNK = 128
RET_UNROLL = 8
ATT_GROUP = 8
ATT_BLOCK = 128
WIN_STEPS = 128
DILATIONS = (1, 4, 16)
N_MEM = 256
N_MEM_HEADS = 4
MEM_HEAD_DIM = 256
N_GROUPS = 4
N_EXP_PER_GROUP = 8
N_EXPERTS = 32
D_EXPERT = 256
EPS = 1e-6
LANES = 128
EXPERT_LANE0 = 32
MIX_TILE = 1024
CROSS_TILE = 1024
CROSS_SPLIT = 4
SAMPLE_MEMS_PER_STEP = 8
MOE_ROW_TILE = 512
EXPERT_SPLIT = 2
ZERO_ROWS = 256
ROUTE_BLOCK = 256
RUN_ALIGN = 8
SORT_ROWS = 2 * ROUTE_BLOCK + N_EXPERTS * RUN_ALIGN
SORT_CHUNKS = SORT_ROWS // RUN_ALIGN
SORT_BLOCKS_PER_STEP = 2
VMEM_LIMIT = 56 * 1024 * 1024

NEG_INF = float("-inf")


def _cparams(n_axes, vmem=VMEM_LIMIT):
    return pltpu.CompilerParams(dimension_semantics=("arbitrary",) * n_axes,
                                vmem_limit_bytes=vmem)


def _dot(a, b):
    return jnp.dot(a, b, preferred_element_type=F32)


def _dot_nt(a, b):
    return lax.dot_general(a, b, (((1,), (1,)), ((), ())), preferred_element_type=F32)


def _dot_tn(a, b):
    return lax.dot_general(a, b, (((0,), (0,)), ((), ())), preferred_element_type=F32)


def _rms(x, g):
    ms = jnp.mean(x * x, axis=-1, keepdims=True)
    return x * lax.rsqrt(ms + EPS) * g


def _mix_proj_kernel(x_ref, g_ref, w_ref, gq_ref, gk_ref,
                     qr_ref, kr_ref, vr_ref, gr_ref, qa_ref, ka_ref, va_ref, *kv_t_refs):
    act = qr_ref.dtype
    h = _rms(x_ref[...], g_ref[...]).astype(BF16)

    def proj(j):
        return _dot(h, w_ref[:, j * WIDTH:(j + 1) * WIDTH])

    def head_norm(z, g):
        is_lo = _pair_masks((z.shape[0], LANES))
        parts = []
        for p in range(N_PAIRS):
            zp = z[:, p * LANES:(p + 1) * LANES]
            parts.append(zp * lax.rsqrt(_segment_mean(zp * zp, is_lo) + EPS))
        return jnp.concatenate(parts, axis=1) * g

    qr_ref[...] = proj(0).astype(act)
    kr_ref[...] = proj(1).astype(act)
    vr_ref[...] = proj(2).astype(act)
    gr_ref[...] = proj(3).astype(act)
    qa_ref[...] = head_norm(proj(4), gq_ref[...]).astype(act)
    ka = head_norm(proj(5), gk_ref[...])
    va = proj(6)
    ka_ref[...] = ka.astype(ka_ref.dtype)
    va_ref[...] = va.astype(va_ref.dtype)
    if kv_t_refs:
        kat_ref, vat_ref = kv_t_refs
        kat_ref[0] = ka.T
        vat_ref[0] = va.T


def _mix_proj(x2d, g_mix, w_in_bf, gq_t, gk_t, tm, act_dtype, seq=None):
    t = x2d.shape[0]
    const = lambda i: (0, 0)
    row = lambda i: (i, 0)
    out_act = jax.ShapeDtypeStruct((t, WIDTH), act_dtype)
    kv_dtype = F32 if seq is None else act_dtype
    out_kv = jax.ShapeDtypeStruct((t, WIDTH), kv_dtype)
    out_specs = [pl.BlockSpec((tm, WIDTH), row)] * 7
    out_shape = [out_act] * 5 + [out_kv] * 2
    if seq is not None:
        per_b = seq // tm
        t_spec = pl.BlockSpec((1, WIDTH, tm), lambda i: (i // per_b, 0, i % per_b))
        out_specs += [t_spec, t_spec]
        out_shape += [jax.ShapeDtypeStruct((t // seq, WIDTH, seq), F32)] * 2
    return pl.pallas_call(
        _mix_proj_kernel,
        grid=(t // tm,),
        in_specs=[
            pl.BlockSpec((tm, D_MODEL), row),
            pl.BlockSpec((1, D_MODEL), const),
            pl.BlockSpec((D_MODEL, IN_COLS), const),
            pl.BlockSpec((1, WIDTH), const),
            pl.BlockSpec((1, WIDTH), const),
        ],
        out_specs=out_specs,
        out_shape=out_shape,
        compiler_params=_cparams(1),
        name="mix_proj",
    )(x2d, g_mix, w_in_bf, gq_t, gk_t)


def _pair_masks(shape):
    lane = lax.broadcasted_iota(I32, shape, len(shape) - 1)
    return lane < HEAD_DIM


def _segment_mean(x, is_lo):
    zero = jnp.zeros_like(x)
    lo = jnp.sum(jnp.where(is_lo, x, zero), axis=-1, keepdims=True)
    hi = jnp.sum(jnp.where(is_lo, zero, x), axis=-1, keepdims=True)
    return jnp.where(is_lo, lo, hi) * (1.0 / HEAD_DIM)


def _group_norm_gate(o, g_r, g_out, is_lo):
    c = o - _segment_mean(o, is_lo)
    y = c * lax.rsqrt(_segment_mean(c * c, is_lo) + EPS) * g_out
    g = g_r.astype(F32)
    return y * (g * (1.0 / (1.0 + jnp.exp(-g))))


def _retention_kernel(lg_ref, gout_ref, q_ref, k_ref, v_ref, g_ref, o_ref, st_ref, kv_ref, s_ref,
                      *, n_chunks):
    c_len = RET_CHUNK
    shape = (c_len, LANES)
    is_lo = _pair_masks(shape)
    row = lax.broadcasted_iota(I32, shape, 0)
    col = lax.broadcasted_iota(I32, shape, 1)
    rel = (row - col).astype(F32)
    lg_lane = lg_ref[0]
    lg0 = lg_lane[:, 0:1]
    lg1 = lg_lane[:, HEAD_DIM:HEAD_DIM + 1]
    scale = HEAD_DIM ** -0.5
    causal = rel >= 0.0
    relp = jnp.maximum(rel, 0.0)
    d0 = jnp.where(causal, jnp.exp(lg0 * relp), 0.0) * scale
    d1 = jnp.where(causal, jnp.exp(lg1 * relp), 0.0) * scale
    rowf = row.astype(F32)
    w_k = jnp.exp(lg_lane * (c_len - 1.0 - rowf)) * scale
    w_q = jnp.exp(lg_lane * (rowf + 1.0))
    lg_row = jnp.where(row < HEAD_DIM, lg0, lg1)
    g_chunk = jnp.exp(lg_row * float(c_len))
    same_head = (row < HEAD_DIM) == (col < HEAD_DIM)
    g_out = gout_ref[0]

    def chunk(c):
        return pl.ds(pl.multiple_of(c * c_len, c_len), c_len)

    def outer(i, carry):
        for u in range(RET_UNROLL):
            c = i * RET_UNROLL + u
            kw = (k_ref[0, chunk(c), :].astype(F32) * w_k).astype(BF16)
            kv_ref[c] = jnp.where(same_head, _dot_tn(kw, v_ref[0, chunk(c), :]), 0.0)
        return carry

    lax.fori_loop(0, n_chunks // RET_UNROLL, outer, 0)

    def recur(c, state):
        kv = kv_ref[c]
        kv_ref[c] = state
        return g_chunk * state + kv

    state = lax.fori_loop(0, n_chunks, recur, jnp.zeros(shape, F32))
    st_ref[0, 0] = state[:HEAD_DIM, :HEAD_DIM]
    st_ref[0, 1] = state[HEAD_DIM:, HEAD_DIM:]

    def score_stage(i, slot):
        for u in range(RET_UNROLL):
            sl = chunk(i * RET_UNROLL + u)
            qc = q_ref[0, sl, :]
            kc = k_ref[0, sl, :]
            zero = jnp.zeros_like(qc)
            s_ref[slot, 2 * u] = (_dot_nt(jnp.where(is_lo, qc, zero), kc) * d0).astype(BF16)
            s_ref[slot, 2 * u + 1] = (_dot_nt(jnp.where(is_lo, zero, qc), kc) * d1).astype(BF16)

    def value_stage(i, slot):
        for u in range(RET_UNROLL):
            c = i * RET_UNROLL + u
            sl = chunk(c)
            qc = q_ref[0, sl, :]
            vc = v_ref[0, sl, :]
            zero = jnp.zeros_like(vc)
            o_in = (_dot(s_ref[slot, 2 * u], jnp.where(is_lo, vc, zero))
                    + _dot(s_ref[slot, 2 * u + 1], jnp.where(is_lo, zero, vc)))
            o_x = _dot(qc, kv_ref[c].astype(BF16)) * w_q
            o_ref[0, sl, :] = _group_norm_gate(o_in + o_x, g_ref[0, sl, :], g_out,
                                               is_lo).astype(BF16)

    n_groups = n_chunks // RET_UNROLL
    score_stage(0, 0)

    def inner(i, carry):
        value_stage(i, i % 2)
        score_stage(i + 1, (i + 1) % 2)
        return carry

    lax.fori_loop(0, n_groups - 1, inner, 0)
    value_stage(n_groups - 1, (n_groups - 1) % 2)


def _retention(q, k, v, g, lg_pairs, gout_pairs):
    b, s, _ = q.shape
    blk = pl.BlockSpec((1, s, LANES), lambda i, p: (i, 0, p))
    tab = pl.BlockSpec((1, 1, LANES), lambda i, p: (p, 0, 0))
    return pl.pallas_call(
        functools.partial(_retention_kernel, n_chunks=s // RET_CHUNK),
        grid=(b, N_PAIRS),
        in_specs=[tab, tab, blk, blk, blk, blk],
        out_specs=[blk, pl.BlockSpec((1, 2, HEAD_DIM, HEAD_DIM), lambda i, p: (i, p, 0, 0))],
        out_shape=[jax.ShapeDtypeStruct((b, s, WIDTH), BF16),
                   jax.ShapeDtypeStruct((b, N_HEADS, HEAD_DIM, HEAD_DIM), F32)],
        scratch_shapes=[pltpu.VMEM((s // RET_CHUNK, LANES, LANES), F32),
                        pltpu.VMEM((2, 2 * RET_UNROLL, RET_CHUNK, RET_CHUNK), BF16)],
        compiler_params=_cparams(2),
        name="retention",
    )(lg_pairs, gout_pairs, q, k, v, g)


def _attn_scores(qb, kb, bias0, bias1, is_lo):
    qb = qb.astype(BF16)
    kb = kb.astype(BF16)
    zq = jnp.zeros_like(qb)
    s0 = _dot_nt(jnp.where(is_lo, qb, zq), kb) + bias0
    s1 = _dot_nt(jnp.where(is_lo, zq, qb), kb) + bias1
    return s0, s1


def _attn_values(s0, s1, vb, is_lo):
    m0 = jnp.max(s0, axis=-1, keepdims=True)
    m1 = jnp.max(s1, axis=-1, keepdims=True)
    p0 = jnp.exp(s0 - m0).astype(BF16)
    p1 = jnp.exp(s1 - m1).astype(BF16)
    is_lo_k = _pair_masks(vb.shape)
    one = jnp.ones_like(vb)
    r0 = _dot(p0, jnp.where(is_lo_k, vb, one).astype(BF16))
    r1 = _dot(p1, jnp.where(is_lo_k, one, vb).astype(BF16))
    acc = jnp.where(is_lo, r0, r1)
    l = pltpu.roll(jnp.where(is_lo, r1, r0), HEAD_DIM, 1)
    m = jnp.where(is_lo, m0, m1)
    return acc, m, l


def _dil_attn_kernel(sl_ref, q_ref, k_ref, v_ref, o_ref,
                     qf_ref, kf_ref, vf_ref, acc_ref, m_ref, l_ref, bias_ref, bias1_ref, s_ref, *,
                     seq):
    blk = ATT_BLOCK
    is_lo = _pair_masks((blk, LANES))
    slope = sl_ref[0]
    slope0 = slope[:, 0:1]
    slope1 = slope[:, HEAD_DIM:HEAD_DIM + 1]

    qf_ref[...] = q_ref[0].astype(F32) * (HEAD_DIM ** -0.5)
    kf_ref[...] = k_ref[0].astype(F32)
    vf_ref[...] = v_ref[0].astype(F32)

    @pl.when(pl.program_id(1) == 0)
    def _():
        i2 = lax.broadcasted_iota(I32, (blk, 2 * blk), 0)
        j2 = lax.broadcasted_iota(I32, (blk, 2 * blk), 1)
        rel2 = blk + i2 - j2
        ok2 = (rel2 >= 0) & (rel2 <= WIN_STEPS)
        i1 = lax.broadcasted_iota(I32, (blk, blk), 0)
        j1 = lax.broadcasted_iota(I32, (blk, blk), 1)
        rel1 = i1 - j1
        ok1 = rel1 >= 0
        for pi, d in enumerate(DILATIONS):
            dist2 = (rel2 * d).astype(F32)
            dist1 = (rel1 * d).astype(F32)
            bias_ref[2 * pi] = jnp.where(ok2, -slope0 * dist2, NEG_INF)
            bias_ref[2 * pi + 1] = jnp.where(ok2, -slope1 * dist2, NEG_INF)
            bias1_ref[2 * pi] = jnp.where(ok1, -slope0 * dist1, NEG_INF)
            bias1_ref[2 * pi + 1] = jnp.where(ok1, -slope1 * dist1, NEG_INF)

    def rows(start, n, d):
        return pl.ds(start, n) if d == 1 else pl.ds(start, n, stride=d)

    def store(pi, sl, res):
        acc, m, l = res
        acc_ref[pi, sl, :] = acc
        m_ref[pi, sl, :] = m
        l_ref[pi, sl, :] = l

    def key_rows(q0, k0, d):
        return rows(q0, blk, d) if k0 is None else rows(k0, 2 * blk, d)

    def score_stage(pi, d, items, slot):
        for u, (q0, k0) in enumerate(items):
            b_ref, nk = (bias1_ref, blk) if k0 is None else (bias_ref, 2 * blk)
            s0, s1 = _attn_scores(qf_ref[rows(q0, blk, d), :], kf_ref[key_rows(q0, k0, d), :],
                                  b_ref[2 * pi], b_ref[2 * pi + 1], is_lo)
            s_ref[slot, 2 * u, :, :nk] = s0
            s_ref[slot, 2 * u + 1, :, :nk] = s1

    def value_stage(pi, d, items, slot):
        for u, (q0, k0) in enumerate(items):
            nk = blk if k0 is None else 2 * blk
            res = _attn_values(s_ref[slot, 2 * u, :, :nk], s_ref[slot, 2 * u + 1, :, :nk],
                               vf_ref[key_rows(q0, k0, d), :], is_lo)
            store(pi, rows(q0, blk, d), res)

    def pipeline(pi, d, n_groups, items_of):
        score_stage(pi, d, items_of(0), 0)

        def body(g, carry):
            value_stage(pi, d, items_of(g), g % 2)
            score_stage(pi, d, items_of(g + 1), (g + 1) % 2)
            return carry

        lax.fori_loop(0, n_groups - 1, body, 0)
        value_stage(pi, d, items_of(n_groups - 1), (n_groups - 1) % 2)

    for pi, d in enumerate(DILATIONS):
        n_blocks = seq // d // blk
        step = blk * d
        firsts = [(r, None) for r in range(d)]
        if n_blocks == 1:
            assert d % ATT_GROUP == 0
            pipeline(pi, d, d // ATT_GROUP,
                     lambda g: [(g * ATT_GROUP + u, None) for u in range(ATT_GROUP)])
            continue
        assert d <= ATT_GROUP
        score_stage(pi, d, firsts, 0)
        value_stage(pi, d, firsts, 0)
        per_group = max(p for p in range(1, ATT_GROUP // d + 1) if (n_blocks - 1) % p == 0)

        def general(g, d=d, step=step, per_group=per_group):
            items = []
            for u in range(per_group):
                base = (1 + g * per_group + u) * step
                if d == 1:
                    base = pl.multiple_of(base, blk)
                items += [(base + r, base + r - step) for r in range(d)]
            return items

        pipeline(pi, d, (n_blocks - 1) // per_group, general)

    def merge(c, carry):
        sl = pl.ds(pl.multiple_of(c * 256, 256), 256)
        m_all = [m_ref[pi, sl, :] for pi in range(3)]
        m_max = jnp.maximum(jnp.maximum(m_all[0], m_all[1]), m_all[2])
        num = jnp.zeros((256, LANES), F32)
        den = jnp.zeros((256, LANES), F32)
        for pi in range(3):
            w = jnp.exp(m_all[pi] - m_max)
            num = num + w * acc_ref[pi, sl, :]
            den = den + w * l_ref[pi, sl, :]
        o_ref[0, sl, :] = (num / den).astype(BF16)
        return carry

    lax.fori_loop(0, seq // 256, merge, 0)


def _dilated_attention(qa, ka, va, slope_pairs):
    b, s, _ = qa.shape
    blk = pl.BlockSpec((1, s, LANES), lambda p, i: (i, 0, p))
    tab = pl.BlockSpec((1, 1, LANES), lambda p, i: (p, 0, 0))
    return pl.pallas_call(
        functools.partial(_dil_attn_kernel, seq=s),
        grid=(N_PAIRS, b),
        in_specs=[tab, blk, blk, blk],
        out_specs=blk,
        out_shape=jax.ShapeDtypeStruct((b, s, WIDTH), BF16),
        scratch_shapes=[
            pltpu.VMEM((s, LANES), F32),
            pltpu.VMEM((s, LANES), F32),
            pltpu.VMEM((s, LANES), F32),
            pltpu.VMEM((3, s, LANES), F32),
            pltpu.VMEM((3, s, LANES), F32),
            pltpu.VMEM((3, s, LANES), F32),
            pltpu.VMEM((6, ATT_BLOCK, 2 * ATT_BLOCK), F32),
            pltpu.VMEM((6, ATT_BLOCK, ATT_BLOCK), F32),
            pltpu.VMEM((2, 2 * ATT_GROUP, ATT_BLOCK, 2 * ATT_BLOCK), F32),
        ],
        compiler_params=_cparams(2),
        name="dilated_attention",
    )(slope_pairs, qa, ka, va)


def _sample_mixer_kernel(lg_ref, gout_ref, slope_ref, qr_ref, kr_ref, vr_ref, gr_ref,
                         qa_ref, ka_ref, va_ref, st_ref, ck_ref, cv_ref,
                         r_ref, a_ref, sto_ref, bias_ref, cnt_ref, *, n_new, w_buf):
    n = n_new
    scale = HEAD_DIM ** -0.5
    lg = lg_ref[...]
    qr = qr_ref[...]
    kr = kr_ref[...]
    vr = vr_ref[...]
    ri = lax.broadcasted_iota(I32, (n, n), 0)
    rj = lax.broadcasted_iota(I32, (n, n), 1)
    rel = (ri - rj).astype(F32)
    rowf = lax.broadcasted_iota(I32, (n, HEAD_DIM), 0).astype(F32)
    outs = []
    for h in range(N_HEADS):
        hs = slice(h * HEAD_DIM, (h + 1) * HEAD_DIM)
        lg_h = lg[:, h * HEAD_DIM:h * HEAD_DIM + 1]
        qh, kh, vh = qr[:, hs], kr[:, hs], vr[:, hs]
        dm = jnp.where(rel >= 0.0, jnp.exp(lg_h * jnp.maximum(rel, 0.0)), 0.0) * scale
        s = _dot_nt(qh, kh) * dm
        s_prev = st_ref[0, h]
        o = _dot(s, vh) + _dot(qh, s_prev) * jnp.exp(lg_h * (rowf + 1.0))
        kw = kh * (jnp.exp(lg_h * (n - 1.0 - rowf)) * scale)
        sto_ref[0, h] = jnp.exp(lg_h * float(n)) * s_prev + _dot_tn(kw, vh)
        outs.append(o)
    o_r = jnp.concatenate(outs, axis=1)
    normed = []
    for p in range(N_PAIRS):
        ps = slice(p * LANES, (p + 1) * LANES)
        is_lo = _pair_masks((n, LANES))
        normed.append(_group_norm_gate(o_r[:, ps], gr_ref[:, ps], gout_ref[:, ps], is_lo))
    r_ref[...] = jnp.concatenate(normed, axis=1)

    n_rows = N_HEADS * n
    n_keys = w_buf + LANES

    @pl.when(pl.program_id(0) == 0)
    def _():
        rr = lax.broadcasted_iota(I32, (n_rows, n_keys), 0)
        cc = lax.broadcasted_iota(I32, (n_rows, n_keys), 1)
        qi = rr % n
        dist = w_buf + qi - cc
        valid = (dist >= 0) & (cc < w_buf + n)
        cnt = jnp.zeros((n_rows, n_keys), F32)
        for d in DILATIONS:
            hit = valid & (dist % d == 0) & (dist <= WIN_STEPS * d)
            cnt = cnt + jnp.where(hit, 1.0, 0.0)
        cnt_ref[...] = cnt
        slope = slope_ref[...]
        srow = jnp.zeros((n_rows, 1), F32)
        r1 = lax.broadcasted_iota(I32, (n_rows, 1), 0)
        for h in range(N_HEADS):
            srow = jnp.where(r1 // n == h, slope[:, h * HEAD_DIM:h * HEAD_DIM + 1], srow)
        bias_ref[...] = jnp.where(cnt > 0.0, -srow * dist.astype(F32), NEG_INF)

    qa = qa_ref[...]
    q_rows = jnp.concatenate([qa] * N_HEADS, axis=0)
    r2 = lax.broadcasted_iota(I32, (n_rows, WIDTH), 0)
    c2 = lax.broadcasted_iota(I32, (n_rows, WIDTH), 1)
    q_blk = jnp.where(r2 // n == c2 // HEAD_DIM, q_rows, 0.0).astype(BF16)
    pad = jnp.zeros((LANES - n, WIDTH), F32)
    k_new = jnp.concatenate([ka_ref[...], pad], axis=0).astype(BF16)
    v_new = jnp.concatenate([va_ref[...], pad], axis=0).astype(BF16)
    s_old = _dot(q_blk, ck_ref[0].astype(BF16))
    s_new = _dot_nt(q_blk, k_new)
    s = jnp.concatenate([s_old, s_new], axis=1) * scale + bias_ref[...]
    m = jnp.max(s, axis=-1, keepdims=True)
    p = cnt_ref[...] * jnp.exp(s - m)
    den = jnp.sum(p, axis=-1, keepdims=True)
    p = p.astype(BF16)
    o_all = _dot_nt(p[:, :w_buf], cv_ref[0].astype(BF16)) + _dot(p[:, w_buf:], v_new)
    o_all = o_all / den
    heads = [o_all[h * n:(h + 1) * n, h * HEAD_DIM:(h + 1) * HEAD_DIM] for h in range(N_HEADS)]
    a_ref[...] = jnp.concatenate(heads, axis=1)


def _sample_mixer(lg_lane, gout, slope_lane, qr, kr, vr, gr, qa, ka, va, state, ck, cv, n_new):
    b = state.shape[0]
    w_buf = ck.shape[2]
    tok = pl.BlockSpec((n_new, WIDTH), lambda i: (i, 0))
    tab = pl.BlockSpec((1, WIDTH), lambda i: (0, 0))
    st = pl.BlockSpec((1, N_HEADS, HEAD_DIM, HEAD_DIM), lambda i: (i, 0, 0, 0))
    cache = pl.BlockSpec((1, WIDTH, w_buf), lambda i: (i, 0, 0))
    n_rows = N_HEADS * n_new
    return pl.pallas_call(
        functools.partial(_sample_mixer_kernel, n_new=n_new, w_buf=w_buf),
        grid=(b,),
        in_specs=[tab, tab, tab, tok, tok, tok, tok, tok, tok, tok, st, cache, cache],
        out_specs=[tok, tok, st],
        out_shape=[jax.ShapeDtypeStruct((b * n_new, WIDTH), F32),
                   jax.ShapeDtypeStruct((b * n_new, WIDTH), F32),
                   jax.ShapeDtypeStruct(state.shape, F32)],
        scratch_shapes=[pltpu.VMEM((n_rows, w_buf + LANES), F32),
                        pltpu.VMEM((n_rows, w_buf + LANES), F32)],
        compiler_params=_cparams(1),
        name="sample_mixer",
    )(lg_lane, gout, slope_lane, qr, kr, vr, gr, qa, ka, va, state, ck, cv)


def _mem_kv_kernel(x_ref, g_ref, wk_ref, wv_ref, gk_ref, k_ref, v_ref, kb_ref, vb_ref):
    h = _rms(x_ref[...], g_ref[...]).astype(BF16)
    k = _dot(h, wk_ref[...])
    gk = gk_ref[...]
    v = _dot(h, wv_ref[...])
    vb_ref[...] = v.astype(BF16)
    for hd in range(N_MEM_HEADS):
        hs = slice(hd * MEM_HEAD_DIM, (hd + 1) * MEM_HEAD_DIM)
        kh = _rms(k[:, hs], gk)
        k_ref[:, hd, :] = kh
        kb_ref[:, hs] = kh.astype(BF16)
        v_ref[:, hd, :] = v[:, hs]


def _mem_kv(mem2d, g_mem, w_mk_bf, w_mv_bf, g_k_mem, tm=256):
    t = mem2d.shape[0]
    const = lambda i: (0, 0)
    row = lambda i: (i, 0)
    out = jax.ShapeDtypeStruct((t, N_MEM_HEADS, MEM_HEAD_DIM), F32)
    return pl.pallas_call(
        _mem_kv_kernel,
        grid=(t // tm,),
        in_specs=[pl.BlockSpec((tm, D_MODEL), row), pl.BlockSpec((1, D_MODEL), const),
                  pl.BlockSpec((D_MODEL, D_MODEL), const), pl.BlockSpec((D_MODEL, D_MODEL), const),
                  pl.BlockSpec((1, MEM_HEAD_DIM), const)],
        out_specs=[pl.BlockSpec((tm, N_MEM_HEADS, MEM_HEAD_DIM), lambda i: (i, 0, 0))] * 2
        + [pl.BlockSpec((tm, D_MODEL), row)] * 2,
        out_shape=[out, out] + [jax.ShapeDtypeStruct((t, D_MODEL), BF16)] * 2,
        compiler_params=_cparams(1),
        name="mem_kv",
    )(mem2d, g_mem, w_mk_bf, w_mv_bf, g_k_mem)


def _cross_router_kernel(x_ref, r_ref, a_ref, wo_ref, gc_ref, wmq_ref, gqm_ref, mk_ref, mv_ref,
                         wmo_ref, gmoe_ref, wrh_ref, wrl_ref, br_ref, cin_ref,
                         y_ref, h_ref, gate_ref, *rest, tq, n_split, sorted_mode, native_mem):
    n_scratch = 3 if native_mem else 1
    if native_mem:
        carry_ref, mem_ref, msem = rest[-n_scratch:]
    else:
        carry_ref, = rest[-n_scratch:]
    if sorted_mode:
        lpos_ref, tab_ref, cnt_ref = rest[:-n_scratch]
    else:
        pair_ref, cnt_ref = rest[:-n_scratch]
    first = (pl.program_id(0) == 0) & (pl.program_id(1) == 0)

    @pl.when(first)
    def _():
        carry_ref[...] = cin_ref[...]

    half = D_MODEL // 2
    gqm = gqm_ref[...]
    head_slices = [slice(hd * MEM_HEAD_DIM, (hd + 1) * MEM_HEAD_DIM) for hd in range(N_MEM_HEADS)]
    if native_mem:
        n_mem = mem_ref.shape[1]
        b0 = pl.program_id(0) * n_mem
        copies = [pltpu.make_async_copy(src.at[b0 + j, :, hd, :], mem_ref.at[kv, j, hd], msem)
                  for kv, src in enumerate((mk_ref, mv_ref))
                  for j in range(n_mem) for hd in range(N_MEM_HEADS)]
        for cp in copies:
            cp.start()

        def memory():
            for cp in copies:
                cp.wait()
            return [[[mem_ref[kv, j, hd].astype(BF16) for hd in range(N_MEM_HEADS)]
                     for j in range(n_mem)] for kv in range(2)]
    else:
        n_mem = mk_ref.shape[0]
        mem_kv = [[[ref[j, :, hs].astype(BF16) for hs in head_slices] for j in range(n_mem)]
                  for ref in (mk_ref, mv_ref)]
        memory = lambda: mem_kv

    def attend(q, mem_k, mem_v):
        heads = []
        for hd, hs in enumerate(head_slices):
            qh = _rms(q[:, hs], gqm).astype(BF16)
            s = _dot_nt(qh, mem_k[hd]) * (MEM_HEAD_DIM ** -0.5)
            m = jnp.max(s, axis=-1, keepdims=True)
            p = jnp.exp(s - m)
            p = p / jnp.sum(p, axis=-1, keepdims=True)
            heads.append(_dot(p.astype(BF16), mem_v[hd]))
        return jnp.concatenate(heads, axis=1)

    def rows_block(rs):
        ra = jnp.concatenate([r_ref[rs, :], a_ref[rs, :]], axis=1).astype(BF16)
        y1 = x_ref[rs, :] + _dot(ra, wo_ref[...])
        q = _dot(_rms(y1, gc_ref[...]).astype(BF16), wmq_ref[...])
        mem_k, mem_v = memory()
        per = q.shape[0] // n_mem
        o = jnp.concatenate([attend(q[j * per:(j + 1) * per], mem_k[j], mem_v[j])
                             for j in range(n_mem)], axis=0)
        y2 = y1 + _dot(o.astype(BF16), wmo_ref[...])
        y_ref[rs, :] = y2
        hm = _rms(y2, gmoe_ref[...])
        if sorted_mode:
            h_ref[rs, :] = hm.astype(BF16)
        else:
            h_ref[rs, :] = pltpu.pack_elementwise([hm[:, :half], hm[:, half:]], packed_dtype=BF16)
        hi = hm.astype(BF16)
        lo = (hm - hi.astype(F32)).astype(BF16)
        return _dot(hi, wrh_ref[...]) + _dot(lo, wrh_ref[...]) + _dot(hi, wrl_ref[...])

    sub = tq // n_split
    logits = jnp.concatenate([rows_block(pl.ds(i * sub, sub)) for i in range(n_split)], axis=0)
    logits = logits + br_ref[...]

    rb = min(tq, ROUTE_BLOCK)
    lane = lax.broadcasted_iota(I32, (rb, LANES), 1)
    lanef = lane.astype(F32)
    big = float(LANES)
    is_grp = lane < N_GROUPS
    ti = lax.broadcasted_iota(I32, (rb, rb), 0)
    tj = lax.broadcasted_iota(I32, (rb, rb), 1)
    tri = jnp.where(tj < ti, 1.0, 0.0)
    ui = lax.broadcasted_iota(I32, (LANES, LANES), 0)
    uj = lax.broadcasted_iota(I32, (LANES, LANES), 1)
    upper = jnp.where(ui < uj, 1.0, 0.0).astype(BF16)

    def pick(hit, table):
        return jnp.sum(jnp.where(hit, table, 0.0), axis=-1, keepdims=True)

    def lanes_of(vals):
        out = jnp.zeros((rb, LANES), F32)
        for idx, val in enumerate(vals):
            out = jnp.where(lane == idx, val, out)
        return out

    for blk_i in range(tq // rb):
        rs = slice(blk_i * rb, (blk_i + 1) * rb)
        lg = logits[rs]
        gl = jnp.where(is_grp, lg, NEG_INF)
        gmax = jnp.max(gl, axis=-1, keepdims=True)
        gidx = jnp.min(jnp.where(gl == gmax, lanef, big), axis=-1, keepdims=True)
        p_top = 1.0 / jnp.sum(jnp.where(is_grp, jnp.exp(lg - gmax), 0.0), axis=-1, keepdims=True)
        lo = EXPERT_LANE0 + N_EXP_PER_GROUP * gidx
        in_grp = (lanef >= lo) & (lanef < lo + N_EXP_PER_GROUP)
        el = jnp.where(in_grp, lg, NEG_INF)
        v1 = jnp.max(el, axis=-1, keepdims=True)
        i1 = jnp.min(jnp.where(el == v1, lanef, big), axis=-1, keepdims=True)
        el2 = jnp.where(lanef == i1, NEG_INF, el)
        v2 = jnp.max(el2, axis=-1, keepdims=True)
        i2 = jnp.min(jnp.where(el2 == v2, lanef, big), axis=-1, keepdims=True)
        e21 = jnp.exp(v2 - v1)
        w1 = p_top / (1.0 + e21)
        w2 = p_top * e21 / (1.0 + e21)

        hit1 = lanef == i1
        hit2 = lanef == i2
        onehot = jnp.where(hit1 | hit2, 1.0, 0.0)
        if rb >= 16:
            within = _dot(tri.astype(BF16), onehot.astype(BF16))
        else:
            within = _dot(tri, onehot)
        carry = carry_ref[...]
        count = jnp.sum(onehot, axis=0, keepdims=True)

        if sorted_mode:
            eighths = jnp.floor((count + (RUN_ALIGN - 1.0)) * (1.0 / RUN_ALIGN))
            carry_ref[...] = carry + eighths * float(RUN_ALIGN)
            lstart = (_dot(jnp.broadcast_to(eighths, (8, LANES)).astype(BF16), upper)[0:1]
                      * float(RUN_ALIGN))
            lpos1 = pick(hit1, within + lstart)
            lpos2 = pick(hit2, within + lstart)
            gate_ref[rs, :] = lanes_of((w1, w2, lpos1, lpos2))[:, :4]
            lpos_ref[blk_i] = lanes_of((lpos1, lpos2)).T[:8]
            tab_ref[blk_i] = jnp.concatenate(
                [count, lstart, carry, jnp.zeros((5, LANES), F32)], axis=0)
        else:
            carry_ref[...] = carry + count
            e1 = i1 - EXPERT_LANE0
            e2 = i2 - EXPERT_LANE0
            rank1 = pick(hit1, within + carry)
            rank2 = pick(hit2, within + carry)
            gate_ref[rs, :] = lanes_of((w1, w2))[:, :2]
            pair_ref[rs, :] = lanes_of((e1, e2, rank1, rank2))[:, :4].astype(I32)
    cnt_ref[...] = carry_ref[...]


def _cross_router(x2d, r, a, w_o, g_cross, w_mq, g_q_mem, mk, mv, w_mo, g_moe, w_r_hi, w_r_lo, b_r,
                  cnt_in, n_batch, tq, n_split, sorted_mode):
    t = x2d.shape[0]
    rb = min(tq, ROUTE_BLOCK)
    n_rb = tq // rb
    mem_per_step = max(1, tq // (t // n_batch))
    n_batch = n_batch // mem_per_step
    assert mem_per_step == 1 or n_split == 1
    native_mem = mk.ndim == 4
    scratch = [pltpu.VMEM((1, LANES), F32)]
    if native_mem:
        scratch += [pltpu.VMEM((2, mem_per_step, N_MEM_HEADS, N_MEM, MEM_HEAD_DIM), F32),
                    pltpu.SemaphoreType.DMA(())]
    per_b = t // n_batch // tq
    row = lambda i, j: (i * per_b + j, 0)
    const = lambda i, j: (0, 0)
    if native_mem:
        mem = pl.BlockSpec(memory_space=pl.ANY)
    else:
        mem = pl.BlockSpec((mem_per_step, N_MEM, D_MODEL), lambda i, j: (i, 0, 0))
    wspec = pl.BlockSpec((D_MODEL, D_MODEL), const)
    vec = pl.BlockSpec((1, D_MODEL), const)
    rspec = pl.BlockSpec((D_MODEL, LANES), const)
    blk3 = lambda i, j: (i * per_b + j, 0, 0)
    if sorted_mode:
        out_specs = [pl.BlockSpec((tq, D_MODEL), row), pl.BlockSpec((tq, D_MODEL), row),
                     pl.BlockSpec((tq, 4), row), pl.BlockSpec((n_rb, 8, rb), blk3),
                     pl.BlockSpec((n_rb, 8, LANES), blk3)]
        out_shape = [jax.ShapeDtypeStruct((t, D_MODEL), F32),
                     jax.ShapeDtypeStruct((t, D_MODEL), BF16),
                     jax.ShapeDtypeStruct((t, 4), F32),
                     jax.ShapeDtypeStruct((t // rb, 8, rb), F32),
                     jax.ShapeDtypeStruct((t // rb, 8, LANES), F32)]
    else:
        out_specs = [pl.BlockSpec((tq, D_MODEL), row), pl.BlockSpec((tq, D_MODEL // 2), row),
                     pl.BlockSpec((tq, 2), row), pl.BlockSpec((tq, 4), row)]
        out_shape = [jax.ShapeDtypeStruct((t, D_MODEL), F32),
                     jax.ShapeDtypeStruct((t, D_MODEL // 2), jnp.uint32),
                     jax.ShapeDtypeStruct((t, 2), F32),
                     jax.ShapeDtypeStruct((t, 4), I32)]
    return pl.pallas_call(
        functools.partial(_cross_router_kernel, tq=tq, n_split=n_split, sorted_mode=sorted_mode,
                          native_mem=native_mem),
        grid=(n_batch, per_b),
        in_specs=[pl.BlockSpec((tq, D_MODEL), row), pl.BlockSpec((tq, WIDTH), row),
                  pl.BlockSpec((tq, WIDTH), row), wspec, vec, wspec,
                  pl.BlockSpec((1, MEM_HEAD_DIM), const), mem, mem, wspec, vec,
                  rspec, rspec, pl.BlockSpec((1, LANES), const),
                  pl.BlockSpec((1, LANES), const)],
        out_specs=out_specs + [pl.BlockSpec((1, LANES), const)],
        out_shape=out_shape + [jax.ShapeDtypeStruct((1, LANES), F32)],
        scratch_shapes=scratch,
        compiler_params=_cparams(2),
        name="cross_router",
    )(x2d, r, a, w_o, g_cross, w_mq, g_q_mem, mk, mv, w_mo, g_moe, w_r_hi, w_r_lo, b_r, cnt_in)


def _row_copy(src, src_row, dst, dst_row, sem):
    return pltpu.make_async_copy(src.at[pl.ds(src_row, 1)], dst.at[pl.ds(dst_row, 1)], sem)


def _pair_slot(off_ref, pairs_ref, j, kk):
    return off_ref[pairs_ref[0, 0, 4 * j + kk]] + pairs_ref[0, 0, 4 * j + 2 + kk]


def _dispatch_kernel(off_ref, pairs_ref, hp_ref, xs_in_ref, xs_ref, sem, *, tq):
    del xs_in_ref

    def start(j, carry):
        for kk in range(2):
            _row_copy(hp_ref, j, xs_ref, _pair_slot(off_ref, pairs_ref, j, kk), sem).start()
        return carry

    def wait(j, carry):
        for kk in range(2):
            _row_copy(hp_ref, j, xs_ref, _pair_slot(off_ref, pairs_ref, j, kk), sem).wait()
        return carry

    lax.fori_loop(0, tq, start, 0, unroll=8)
    lax.fori_loop(0, tq, wait, 0, unroll=8)


def _dispatch(offsets, pairs, hp, xs, tq):
    t = hp.shape[0]
    return pl.pallas_call(
        functools.partial(_dispatch_kernel, tq=tq),
        grid_spec=pltpu.PrefetchScalarGridSpec(
            num_scalar_prefetch=1, grid=(t // tq,),
            in_specs=[pl.BlockSpec((1, 1, 4 * tq), lambda i, off: (i, 0, 0),
                                   memory_space=pltpu.SMEM),
                      pl.BlockSpec((tq, D_MODEL // 2), lambda i, off: (i, 0)),
                      pl.BlockSpec(memory_space=pl.ANY)],
            out_specs=pl.BlockSpec(memory_space=pl.ANY),
            scratch_shapes=[pltpu.SemaphoreType.DMA(())]),
        out_shape=jax.ShapeDtypeStruct(xs.shape, xs.dtype),
        input_output_aliases={3: 0},
        compiler_params=pltpu.CompilerParams(dimension_semantics=("arbitrary",),
                                             has_side_effects=True),
        name="moe_dispatch",
    )(offsets, pairs.reshape(t // tq, 1, 4 * tq), hp, xs)


def _unpack_pair(words):
    lo = pltpu.unpack_elementwise(words, index=0, packed_dtype=BF16, unpacked_dtype=F32)
    hi = pltpu.unpack_elementwise(words, index=1, packed_dtype=BF16, unpacked_dtype=F32)
    return lo, hi


def _expert_kernel(te_ref, nt_ref, xs_ref, wg_ref, wu_ref, wd_ref, ys_ref):
    del te_ref
    half = D_MODEL // 2

    @pl.when(pl.program_id(0) < nt_ref[0])
    def _():
        wg = wg_ref[0].astype(BF16)
        wu = wu_ref[0].astype(BF16)
        wd = wd_ref[0].astype(BF16)
        sub = xs_ref.shape[0] // EXPERT_SPLIT
        for i in range(EXPERT_SPLIT):
            rs = pl.ds(i * sub, sub)
            lo, hi = _unpack_pair(xs_ref[rs, :])
            lo = lo.astype(BF16)
            hi = hi.astype(BF16)
            g = _dot(lo, wg[:half]) + _dot(hi, wg[half:])
            u = _dot(lo, wu[:half]) + _dot(hi, wu[half:])
            hid = (g * (1.0 / (1.0 + jnp.exp(-g))) * u).astype(BF16)
            y = _dot(hid, wd)
            ys_ref[rs, :] = pltpu.pack_elementwise([y[:, :half], y[:, half:]], packed_dtype=BF16)

    @pl.when(pl.program_id(0) >= nt_ref[0])
    def _():
        ys_ref[...] = jnp.zeros_like(ys_ref)


def _experts(tile_expert, n_tiles, xs, w_gate, w_up, w_down):
    rows = xs.shape[0]
    n_max = rows // MOE_ROW_TILE

    def xmap(i, te, nt):
        return (jnp.maximum(jnp.minimum(i, nt[0] - 1), 0), 0)

    def wmap(i, te, nt):
        return (te[i], 0, 0)

    return pl.pallas_call(
        _expert_kernel,
        grid_spec=pltpu.PrefetchScalarGridSpec(
            num_scalar_prefetch=2,
            grid=(n_max,),
            in_specs=[pl.BlockSpec((MOE_ROW_TILE, D_MODEL // 2), xmap),
                      pl.BlockSpec((1, D_MODEL, D_EXPERT), wmap),
                      pl.BlockSpec((1, D_MODEL, D_EXPERT), wmap),
                      pl.BlockSpec((1, D_EXPERT, D_MODEL), wmap)],
            out_specs=pl.BlockSpec((MOE_ROW_TILE, D_MODEL // 2), lambda i, te, nt: (i, 0)),
        ),
        out_shape=jax.ShapeDtypeStruct(xs.shape, xs.dtype),
        compiler_params=_cparams(1),
        name="moe_experts",
    )(tile_expert, n_tiles, xs, w_gate, w_up, w_down)


def _combine_kernel(off_ref, pairs_ref, pairs_next_ref, ys_ref, y_ref, gate_ref, o_ref,
                    buf_ref, sem, *, tq):
    step = pl.program_id(0)
    n_steps = pl.num_programs(0)

    def gather(p_ref, slot, wait):
        def body(j, carry):
            for kk in range(2):
                cp = pltpu.make_async_copy(ys_ref.at[pl.ds(_pair_slot(off_ref, p_ref, j, kk), 1)],
                                           buf_ref.at[slot, kk, pl.ds(j, 1)], sem.at[slot])
                cp.wait() if wait else cp.start()
            return carry
        lax.fori_loop(0, tq, body, 0, unroll=8)

    @pl.when(step == 0)
    def _():
        gather(pairs_ref, 0, wait=False)

    @pl.when(step + 1 < n_steps)
    def _():
        gather(pairs_next_ref, (step + 1) % 2, wait=False)

    slot = step % 2
    gather(pairs_ref, slot, wait=True)
    half = D_MODEL // 2
    gates = gate_ref[...]
    w1 = gates[:, 0:1]
    w2 = gates[:, 1:2]
    lo1, hi1 = _unpack_pair(buf_ref[slot, 0])
    lo2, hi2 = _unpack_pair(buf_ref[slot, 1])
    y = y_ref[...]
    o_ref[:, :half] = y[:, :half] + (w1 * lo1 + w2 * lo2)
    o_ref[:, half:] = y[:, half:] + (w1 * hi1 + w2 * hi2)


def _combine(offsets, pairs, ys, y2, gates, tq):
    t = y2.shape[0]
    n = t // tq
    row = lambda i, off: (i, 0)
    smem = lambda f: pl.BlockSpec((1, 1, 4 * tq), f, memory_space=pltpu.SMEM)
    pairs3 = pairs.reshape(n, 1, 4 * tq)
    return pl.pallas_call(
        functools.partial(_combine_kernel, tq=tq),
        grid_spec=pltpu.PrefetchScalarGridSpec(
            num_scalar_prefetch=1, grid=(n,),
            in_specs=[smem(lambda i, off: (i, 0, 0)),
                      smem(lambda i, off: (jnp.minimum(i + 1, n - 1), 0, 0)),
                      pl.BlockSpec(memory_space=pl.ANY),
                      pl.BlockSpec((tq, D_MODEL), row), pl.BlockSpec((tq, 2), row)],
            out_specs=pl.BlockSpec((tq, D_MODEL), row),
            scratch_shapes=[pltpu.VMEM((2, 2, tq, D_MODEL // 2), jnp.uint32),
                            pltpu.SemaphoreType.DMA((2,))]),
        out_shape=jax.ShapeDtypeStruct((t, D_MODEL), F32),
        compiler_params=_cparams(1),
        name="moe_combine",
    )(offsets, pairs3, pairs3, ys, y2, gates)


def _for_each_chunk(tot_ref, rows_ref, blk, fn):
    base = blk * SORT_CHUNKS

    def body(k, carry):
        fn(pl.multiple_of(k * RUN_ALIGN, RUN_ALIGN), pl.multiple_of(rows_ref[base + k], RUN_ALIGN))
        return carry

    lax.fori_loop(0, tot_ref[blk], body, 0)


def _sorted_dispatch_kernel(tot_ref, rows_ref, zt_ref, nt_ref, h_ref, lpos_ref, xs_ref,
                            stage_ref, zero_ref, sem, zsem, *, n_zero, n_tiles_max):
    step = pl.program_id(0)
    n_steps = pl.num_programs(0)

    @pl.when(step == 0)
    def _():
        zero_ref[...] = jnp.zeros_like(zero_ref)

        def fill(granule):
            row = pl.multiple_of(granule * ZERO_ROWS, ZERO_ROWS)
            return pltpu.make_async_copy(zero_ref, xs_ref.at[pl.ds(row, ZERO_ROWS)], zsem)

        def tail(do):
            def body(tile, carry):
                do(fill(tile))
                return carry
            lax.fori_loop(nt_ref[0], n_tiles_max, body, 0)

        for i in range(n_zero):
            pl.when(zt_ref[i] >= 0)(fill(jnp.maximum(zt_ref[i], 0)).start)
        tail(lambda cp: cp.start())
        for i in range(n_zero):
            pl.when(zt_ref[i] >= 0)(fill(jnp.maximum(zt_ref[i], 0)).wait)
        tail(lambda cp: cp.wait())

    half = D_MODEL // 2
    n_slots = lpos_ref.shape[0]
    rb = h_ref.shape[0] // n_slots
    jrow = lax.broadcasted_iota(I32, (SORT_ROWS, rb), 0).astype(F32)

    def copy(s):
        return lambda l, g: pltpu.make_async_copy(stage_ref.at[s, pl.ds(l, RUN_ALIGN)],
                                                  xs_ref.at[pl.ds(g, RUN_ALIGN)], sem.at[s])

    for s in range(n_slots):
        blk = step * n_slots + s

        @pl.when(step > 0)
        def _():
            _for_each_chunk(tot_ref, rows_ref, blk - n_slots, lambda l, g: copy(s)(l, g).wait())

        lpos = lpos_ref[s]
        perm = jnp.where((jrow == lpos[0:1]) | (jrow == lpos[1:2]), 1.0, 0.0).astype(BF16)
        x = _dot(perm, h_ref[s * rb:(s + 1) * rb, :])
        stage_ref[s] = pltpu.pack_elementwise([x[:, :half], x[:, half:]], packed_dtype=BF16)
        _for_each_chunk(tot_ref, rows_ref, blk, lambda l, g: copy(s)(l, g).start())

    @pl.when(step == n_steps - 1)
    def _():
        for s in range(n_slots):
            _for_each_chunk(tot_ref, rows_ref, step * n_slots + s,
                            lambda l, g: copy(s)(l, g).wait())


def _sorted_dispatch(chunk_total, chunk_rows, zero_tiles, n_tiles, h_bf, lpos_t, n_rows):
    n_blocks, _, rb = lpos_t.shape
    per_step = SORT_BLOCKS_PER_STEP
    return pl.pallas_call(
        functools.partial(_sorted_dispatch_kernel, n_zero=zero_tiles.shape[0],
                          n_tiles_max=n_rows // ZERO_ROWS),
        grid_spec=pltpu.PrefetchScalarGridSpec(
            num_scalar_prefetch=4, grid=(n_blocks // per_step,),
            in_specs=[pl.BlockSpec((per_step * rb, D_MODEL), lambda i, *_: (i, 0)),
                      pl.BlockSpec((per_step, 8, rb), lambda i, *_: (i, 0, 0))],
            out_specs=pl.BlockSpec(memory_space=pl.ANY),
            scratch_shapes=[pltpu.VMEM((per_step, SORT_ROWS, D_MODEL // 2), jnp.uint32),
                            pltpu.VMEM((ZERO_ROWS, D_MODEL // 2), jnp.uint32),
                            pltpu.SemaphoreType.DMA((per_step,)), pltpu.SemaphoreType.DMA(())]),
        out_shape=jax.ShapeDtypeStruct((n_rows, D_MODEL // 2), jnp.uint32),
        compiler_params=pltpu.CompilerParams(dimension_semantics=("arbitrary",),
                                             has_side_effects=True, vmem_limit_bytes=VMEM_LIMIT),
        name="moe_sorted_dispatch",
    )(chunk_total, chunk_rows, zero_tiles, n_tiles, h_bf, lpos_t)


def _sorted_combine_kernel(tot_ref, rows_ref, ys_ref, y_ref, gate_ref, o_ref, buf_ref, sem):
    step = pl.program_id(0)
    n_steps = pl.num_programs(0)

    def gather(blk, s, wait):
        def fn(l, g):
            cp = pltpu.make_async_copy(ys_ref.at[pl.ds(g, RUN_ALIGN)],
                                       buf_ref.at[s, pl.ds(l, RUN_ALIGN)], sem.at[s])
            cp.wait() if wait else cp.start()
        _for_each_chunk(tot_ref, rows_ref,blk, fn)

    per_step = SORT_BLOCKS_PER_STEP
    n_slots = 2 * per_step
    first = step * per_step

    @pl.when(step == 0)
    def _():
        buf_ref[...] = jnp.zeros_like(buf_ref)
        for u in range(per_step):
            gather(u, u, wait=False)

    @pl.when(step + 1 < n_steps)
    def _():
        for u in range(per_step):
            gather(first + per_step + u, (first + per_step + u) % n_slots, wait=False)

    half = D_MODEL // 2
    rb = y_ref.shape[0] // per_step
    jcol = lax.broadcasted_iota(I32, (rb, SORT_ROWS), 1).astype(F32)
    for u in range(per_step):
        slot = (first + u) % n_slots
        gather(first + u, slot, wait=True)
        rs = slice(u * rb, (u + 1) * rb)
        gates = gate_ref[rs, :]
        pick1 = jnp.where(jcol == gates[:, 2:3], 1.0, 0.0).astype(BF16)
        pick2 = jnp.where(jcol == gates[:, 3:4], 1.0, 0.0).astype(BF16)
        lo, hi = _unpack_pair(buf_ref[slot])
        lo = lo.astype(BF16)
        hi = hi.astype(BF16)
        w1 = gates[:, 0:1]
        w2 = gates[:, 1:2]
        y = y_ref[rs, :]
        o_ref[rs, :half] = y[:, :half] + (w1 * _dot(pick1, lo) + w2 * _dot(pick2, lo))
        o_ref[rs, half:] = y[:, half:] + (w1 * _dot(pick1, hi) + w2 * _dot(pick2, hi))


def _sorted_combine(chunk_total, chunk_rows, ys, y2, gates, rb):
    t = y2.shape[0]
    row = lambda i, *_: (i, 0)
    tq = SORT_BLOCKS_PER_STEP * rb
    return pl.pallas_call(
        _sorted_combine_kernel,
        grid_spec=pltpu.PrefetchScalarGridSpec(
            num_scalar_prefetch=2, grid=(t // tq,),
            in_specs=[pl.BlockSpec(memory_space=pl.ANY),
                      pl.BlockSpec((tq, D_MODEL), row), pl.BlockSpec((tq, 4), row)],
            out_specs=pl.BlockSpec((tq, D_MODEL), row),
            scratch_shapes=[pltpu.VMEM((2 * SORT_BLOCKS_PER_STEP, SORT_ROWS, D_MODEL // 2),
                                       jnp.uint32),
                            pltpu.SemaphoreType.DMA((2 * SORT_BLOCKS_PER_STEP,))]),
        out_shape=jax.ShapeDtypeStruct((t, D_MODEL), F32),
        compiler_params=_cparams(1),
        name="moe_sorted_combine",
    )(chunk_total, chunk_rows, ys, y2, gates)


def _lane_table(per_head):
    lanes = jnp.repeat(per_head.astype(F32), HEAD_DIM)
    return lanes.reshape(N_PAIRS, 1, LANES), lanes.reshape(1, WIDTH)


def kernel(x_prompt, x_sample, mem_prompt, state_ret, cache_win_k, cache_win_v, cache_mem_k,
           cache_mem_v, g_mix, w_in, g_ret_out, g_q_att, g_k_att, w_o, g_cross, g_mem, w_mq, w_mk,
           w_mv, g_q_mem, g_k_mem, w_mo, g_moe, w_router_group, b_router_group, w_router_expert,
           b_router_expert, w_exp_gate, w_exp_up, w_exp_down):
    depth = g_mix.shape[0]
    assert depth == 1
    b, s, _ = x_prompt.shape
    bs, ns, _ = x_sample.shape
    t_p, t_s = b * s, bs * ns
    l = 0

    heads = jnp.arange(N_HEADS, dtype=F32)
    lg_pairs, lg_lane = _lane_table(jnp.log(1.0 - 2.0 ** (-5.0 - heads)))
    slope_pairs, slope_lane = _lane_table(2.0 ** (-8.0 * (heads + 1.0) / N_HEADS))
    gout_lane = g_ret_out[l].reshape(1, WIDTH)
    gout_pairs = g_ret_out[l].reshape(N_PAIRS, 1, LANES)
    gq_t = jnp.tile(g_q_att[l], N_HEADS).reshape(1, WIDTH)
    gk_t = jnp.tile(g_k_att[l], N_HEADS).reshape(1, WIDTH)
    vec = lambda v: v.reshape(1, -1)
    w_in_bf = w_in[l].astype(BF16)
    w_o_bf, w_mq_bf, w_mk_bf = w_o[l].astype(BF16), w_mq[l].astype(BF16), w_mk[l].astype(BF16)
    w_mv_bf, w_mo_bf = w_mv[l].astype(BF16), w_mo[l].astype(BF16)
    gap = EXPERT_LANE0 - N_GROUPS
    tail = LANES - EXPERT_LANE0 - N_EXPERTS
    w_r = jnp.concatenate(
        [w_router_group[l], jnp.zeros((D_MODEL, gap), F32),
         jnp.moveaxis(w_router_expert[l], 0, 1).reshape(D_MODEL, N_EXPERTS),
         jnp.zeros((D_MODEL, tail), F32)], axis=1)
    b_r = jnp.concatenate([b_router_group[l], jnp.zeros((gap,), F32),
                           b_router_expert[l].reshape(-1), jnp.zeros((tail,), F32)]).reshape(1, LANES)
    w_r_hi = w_r.astype(BF16)
    w_r_lo = (w_r - w_r_hi.astype(F32)).astype(BF16)

    xp = x_prompt.reshape(t_p, D_MODEL)
    qr, kr, vr, gr, qa, ka, va, ka_t, va_t = _mix_proj(xp, vec(g_mix[l]), w_in_bf, gq_t, gk_t,
                                                       tm=MIX_TILE, act_dtype=BF16, seq=s)
    shp = lambda z: z.reshape(b, s, WIDTH)
    r_p, state_p = _retention(shp(qr), shp(kr), shp(vr), shp(gr), lg_pairs, gout_pairs)
    a_p = _dilated_attention(shp(qa), shp(ka), shp(va), slope_pairs)
    mk_p, mv_p, mk_bf, mv_bf = _mem_kv(mem_prompt.reshape(b * N_MEM, D_MODEL), vec(g_mem[l]), w_mk_bf, w_mv_bf,
                          vec(g_k_mem[l]))
    zero_cnt = jnp.zeros((1, LANES), F32)
    y2_p, h_p, gates_p, lpos_p, tabs_p, cnt_p = _cross_router(
        xp, r_p.reshape(t_p, WIDTH), a_p.reshape(t_p, WIDTH), w_o_bf, vec(g_cross[l]), w_mq_bf,
        vec(g_q_mem[l]), mk_bf.reshape(b, N_MEM, D_MODEL), mv_bf.reshape(b, N_MEM, D_MODEL), w_mo_bf,
        vec(g_moe[l]), w_r_hi, w_r_lo, b_r, zero_cnt, n_batch=b, tq=CROSS_TILE, n_split=CROSS_SPLIT,
        sorted_mode=True)

    xs_ = x_sample.reshape(t_s, D_MODEL)
    qr, kr, vr, gr, qa, ka_s, va_s = _mix_proj(xs_, vec(g_mix[l]), w_in_bf, gq_t, gk_t, tm=t_s,
                                               act_dtype=F32)
    pos_minor = lambda c: jnp.transpose(c, (0, 2, 3, 1)).reshape(bs, WIDTH, c.shape[1])
    r_s, a_s, state_s = _sample_mixer(lg_lane, gout_lane, slope_lane, qr, kr, vr, gr, qa, ka_s, va_s,
                                      state_ret[l], pos_minor(cache_win_k[l]),
                                      pos_minor(cache_win_v[l]), n_new=ns)
    y2_s, hp_s, gates_s, pairs_s, cnt_all = _cross_router(
        xs_, r_s, a_s, w_o_bf, vec(g_cross[l]), w_mq_bf, vec(g_q_mem[l]),
        cache_mem_k[l], cache_mem_v[l],
        w_mo_bf, vec(g_moe[l]), w_r_hi, w_r_lo, b_r, cnt_p, n_batch=bs, tq=min(bs, SAMPLE_MEMS_PER_STEP) * ns, n_split=1,
        sorted_mode=False)

    tile = MOE_ROW_TILE
    lanes_e = slice(EXPERT_LANE0, EXPERT_LANE0 + N_EXPERTS)
    counts = cnt_all[0, lanes_e].astype(I32)
    tiles_per = (counts + tile - 1) // tile
    tile_end = jnp.cumsum(tiles_per)
    offsets = (tile_end - tiles_per) * tile
    n_tiles = tile_end[-1:]
    n_blocks = t_p // ROUTE_BLOCK
    n_max = (2 * (t_p + t_s) + n_blocks * N_EXPERTS * (RUN_ALIGN - 1)) // tile + N_EXPERTS
    tile_ids = jnp.minimum(jnp.arange(n_max, dtype=I32), n_tiles[0] - 1)
    tile_expert = jnp.sum((tile_end[None, :] <= tile_ids[:, None]).astype(I32), axis=1)
    seg_end = tile_end * tile
    first_gran = (offsets + cnt_p[0, lanes_e].astype(I32)) // ZERO_ROWS
    n_gran = (t_s + tile) // ZERO_ROWS + 1
    gran = first_gran[None, :] + jnp.arange(n_gran, dtype=I32)[:, None]
    zero_tiles = jnp.where(gran * ZERO_ROWS < seg_end[None, :], gran, -1).reshape(-1)
    first_tail_gran = n_tiles * (tile // ZERO_ROWS)
    run_chunks = (tabs_p[:, 0, lanes_e].astype(I32) + RUN_ALIGN - 1) // RUN_ALIGN
    run_end = jnp.cumsum(run_chunks, axis=1)
    chunk_total = run_end[:, -1]
    chunk_id = jnp.arange(SORT_CHUNKS, dtype=I32)
    owner = (chunk_id[None, :, None] >= run_end[:, None, :]).astype(I32).sum(axis=2)
    owner_hot = owner[:, :, None] == jnp.arange(N_EXPERTS, dtype=I32)[None, None, :]
    run_row0 = offsets[None, :] + tabs_p[:, 2, lanes_e].astype(I32) - (run_end - run_chunks) * RUN_ALIGN
    chunk_rows = (jnp.where(owner_hot, run_row0[:, None, :], 0).sum(axis=2)
                  + chunk_id[None, :] * RUN_ALIGN).reshape(-1)
    xs_sorted = _sorted_dispatch(chunk_total, chunk_rows, zero_tiles, first_tail_gran, h_p, lpos_p,
                                 n_max * tile)
    xs_sorted = _dispatch(offsets, pairs_s, hp_s, xs_sorted, tq=t_s)
    ys_sorted = _experts(tile_expert, n_tiles, xs_sorted, w_exp_gate[l], w_exp_up[l], w_exp_down[l])
    y_p = _sorted_combine(chunk_total, chunk_rows, ys_sorted, y2_p, gates_p, rb=ROUTE_BLOCK)
    y_s = _combine(offsets, pairs_s, ys_sorted, y2_s, gates_s, tq=t_s)

    from_pos_minor = lambda z: jnp.transpose(z.reshape(b, N_HEADS, HEAD_DIM, s), (0, 3, 1, 2))[None]
    return (y_p.reshape(b, s, D_MODEL), y_s.reshape(bs, ns, D_MODEL),
            state_p[None],
            from_pos_minor(ka_t), from_pos_minor(va_t),
            mk_p.reshape(1, b, N_MEM, N_MEM_HEADS, MEM_HEAD_DIM),
            mv_p.reshape(1, b, N_MEM, N_MEM_HEADS, MEM_HEAD_DIM),
            state_s[None],
            ka_s.reshape(1, bs, ns, N_HEADS, HEAD_DIM), va_s.reshape(1, bs, ns, N_HEADS, HEAD_DIM))
```

```python
import functools

import jax
import jax.numpy as jnp
from jax import lax
from jax.experimental import pallas as pl
from jax.experimental.pallas import tpu as pltpu

F32 = jnp.float32
BF16 = jnp.bfloat16
I32 = jnp.int32

D_MODEL = 1024
HEAD_DIM = 64
N_HEADS = 8
WIDTH = N_HEADS * HEAD_DIM
N_PAIRS = N_HEADS // 2
IN_COLS = 7 * WIDTH
RET_CHUNK = 128
RET_UNROLL = 8
ATT_GROUP = 8
ATT_BLOCK = 128
WIN_STEPS = 128
DILATIONS = (1, 4, 16)
N_MEM = 256
N_MEM_HEADS = 4
MEM_HEAD_DIM = 256
N_GROUPS = 4
N_EXP_PER_GROUP = 8
N_EXPERTS = 32
D_EXPERT = 256
EPS = 1e-6
LANES = 128
EXPERT_LANE0 = 32
MIX_TILE = 1024
CROSS_TILE = 1024
CROSS_SPLIT = 4
SAMPLE_MEMS_PER_STEP = 8
MOE_ROW_TILE = 512
EXPERT_SPLIT = 2
ZERO_ROWS = 256
ROUTE_BLOCK = 256
RUN_ALIGN = 8
SORT_ROWS = 2 * ROUTE_BLOCK + N_EXPERTS * RUN_ALIGN
SORT_CHUNKS = SORT_ROWS // RUN_ALIGN
SORT_BLOCKS_PER_STEP = 2
VMEM_LIMIT = 56 * 1024 * 1024

NEG_INF = float("-inf")


def _cparams(n_axes, vmem=VMEM_LIMIT):
    return pltpu.CompilerParams(dimension_semantics=("arbitrary",) * n_axes,
                                vmem_limit_bytes=vmem)


def _dot(a, b):
    return jnp.dot(a, b, preferred_element_type=F32)


def _dot_nt(a, b):
    return lax.dot_general(a, b, (((1,), (1,)), ((), ())), preferred_element_type=F32)


def _dot_tn(a, b):
    return lax.dot_general(a, b, (((0,), (0,)), ((), ())), preferred_element_type=F32)


def _rms(x, g):
    ms = jnp.mean(x * x, axis=-1, keepdims=True)
    return x * lax.rsqrt(ms + EPS) * g


def _mix_proj_kernel(x_ref, g_ref, w_ref, gq_ref, gk_ref,
                     qr_ref, kr_ref, vr_ref, gr_ref, qa_ref, ka_ref, va_ref, *kv_t_refs):
    act = qr_ref.dtype
    h = _rms(x_ref[...], g_ref[...]).astype(BF16)

    def proj(j):
        return _dot(h, w_ref[:, j * WIDTH:(j + 1) * WIDTH])

    def head_norm(z, g):
        is_lo = _pair_masks((z.shape[0], LANES))
        parts = []
        for p in range(N_PAIRS):
            zp = z[:, p * LANES:(p + 1) * LANES]
            parts.append(zp * lax.rsqrt(_segment_mean(zp * zp, is_lo) + EPS))
        return jnp.concatenate(parts, axis=1) * g

    qr_ref[...] = proj(0).astype(act)
    kr_ref[...] = proj(1).astype(act)
    vr_ref[...] = proj(2).astype(act)
    gr_ref[...] = proj(3).astype(act)
    qa_ref[...] = head_norm(proj(4), gq_ref[...]).astype(act)
    ka = head_norm(proj(5), gk_ref[...])
    va = proj(6)
    ka_ref[...] = ka.astype(ka_ref.dtype)
    va_ref[...] = va.astype(va_ref.dtype)
    if kv_t_refs:
        kat_ref, vat_ref = kv_t_refs
        kat_ref[0] = ka.T
        vat_ref[0] = va.T


def _mix_proj(x2d, g_mix, w_in_bf, gq_t, gk_t, tm, act_dtype, seq=None):
    t = x2d.shape[0]
    const = lambda i: (0, 0)
    row = lambda i: (i, 0)
    out_act = jax.ShapeDtypeStruct((t, WIDTH), act_dtype)
    kv_dtype = F32 if seq is None else act_dtype
    out_kv = jax.ShapeDtypeStruct((t, WIDTH), kv_dtype)
    out_specs = [pl.BlockSpec((tm, WIDTH), row)] * 7
    out_shape = [out_act] * 5 + [out_kv] * 2
    if seq is not None:
        per_b = seq // tm
        t_spec = pl.BlockSpec((1, WIDTH, tm), lambda i: (i // per_b, 0, i % per_b))
        out_specs += [t_spec, t_spec]
        out_shape += [jax.ShapeDtypeStruct((t // seq, WIDTH, seq), F32)] * 2
    return pl.pallas_call(
        _mix_proj_kernel,
        grid=(t // tm,),
        in_specs=[
            pl.BlockSpec((tm, D_MODEL), row),
            pl.BlockSpec((1, D_MODEL), const),
            pl.BlockSpec((D_MODEL, IN_COLS), const),
            pl.BlockSpec((1, WIDTH), const),
            pl.BlockSpec((1, WIDTH), const),
        ],
        out_specs=out_specs,
        out_shape=out_shape,
        compiler_params=_cparams(1),
        name="mix_proj",
    )(x2d, g_mix, w_in_bf, gq_t, gk_t)


def _pair_masks(shape):
    lane = lax.broadcasted_iota(I32, shape, len(shape) - 1)
    return lane < HEAD_DIM


def _segment_mean(x, is_lo):
    zero = jnp.zeros_like(x)
    lo = jnp.sum(jnp.where(is_lo, x, zero), axis=-1, keepdims=True)
    hi = jnp.sum(jnp.where(is_lo, zero, x), axis=-1, keepdims=True)
    return jnp.where(is_lo, lo, hi) * (1.0 / HEAD_DIM)


def _group_norm_gate(o, g_r, g_out, is_lo):
    c = o - _segment_mean(o, is_lo)
    y = c * lax.rsqrt(_segment_mean(c * c, is_lo) + EPS) * g_out
    g = g_r.astype(F32)
    return y * (g * (1.0 / (1.0 + jnp.exp(-g))))


def _retention_kernel(lg_ref, gout_ref, q_ref, k_ref, v_ref, g_ref, o_ref, st_ref, kv_ref, s_ref,
                      *, n_chunks):
    c_len = RET_CHUNK
    shape = (c_len, LANES)
    is_lo = _pair_masks(shape)
    row = lax.broadcasted_iota(I32, shape, 0)
    col = lax.broadcasted_iota(I32, shape, 1)
    rel = (row - col).astype(F32)
    lg_lane = lg_ref[0]
    lg0 = lg_lane[:, 0:1]
    lg1 = lg_lane[:, HEAD_DIM:HEAD_DIM + 1]
    scale = HEAD_DIM ** -0.5
    causal = rel >= 0.0
    relp = jnp.maximum(rel, 0.0)
    d0 = jnp.where(causal, jnp.exp(lg0 * relp), 0.0) * scale
    d1 = jnp.where(causal, jnp.exp(lg1 * relp), 0.0) * scale
    rowf = row.astype(F32)
    w_k = jnp.exp(lg_lane * (c_len - 1.0 - rowf)) * scale
    w_q = jnp.exp(lg_lane * (rowf + 1.0))
    lg_row = jnp.where(row < HEAD_DIM, lg0, lg1)
    g_chunk = jnp.exp(lg_row * float(c_len))
    same_head = (row < HEAD_DIM) == (col < HEAD_DIM)
    g_out = gout_ref[0]

    def chunk(c):
        return pl.ds(pl.multiple_of(c * c_len, c_len), c_len)

    def outer(i, carry):
        for u in range(RET_UNROLL):
            c = i * RET_UNROLL + u
            kw = (k_ref[0, chunk(c), :].astype(F32) * w_k).astype(BF16)
            kv_ref[c] = jnp.where(same_head, _dot_tn(kw, v_ref[0, chunk(c), :]), 0.0)
        return carry

    lax.fori_loop(0, n_chunks // RET_UNROLL, outer, 0)

    def recur(c, state):
        kv = kv_ref[c]
        kv_ref[c] = state
        return g_chunk * state + kv

    state = lax.fori_loop(0, n_chunks, recur, jnp.zeros(shape, F32))
    st_ref[0, 0] = state[:HEAD_DIM, :HEAD_DIM]
    st_ref[0, 1] = state[HEAD_DIM:, HEAD_DIM:]

    def score_stage(i, slot):
        for u in range(RET_UNROLL):
            sl = chunk(i * RET_UNROLL + u)
            qc = q_ref[0, sl, :]
            kc = k_ref[0, sl, :]
            zero = jnp.zeros_like(qc)
            s_ref[slot, 2 * u] = (_dot_nt(jnp.where(is_lo, qc, zero), kc) * d0).astype(BF16)
            s_ref[slot, 2 * u + 1] = (_dot_nt(jnp.where(is_lo, zero, qc), kc) * d1).astype(BF16)

    def value_stage(i, slot):
        for u in range(RET_UNROLL):
            c = i * RET_UNROLL + u
            sl = chunk(c)
            qc = q_ref[0, sl, :]
            vc = v_ref[0, sl, :]
            zero = jnp.zeros_like(vc)
            o_in = (_dot(s_ref[slot, 2 * u], jnp.where(is_lo, vc, zero))
                    + _dot(s_ref[slot, 2 * u + 1], jnp.where(is_lo, zero, vc)))
            o_x = _dot(qc, kv_ref[c].astype(BF16)) * w_q
            o_ref[0, sl, :] = _group_norm_gate(o_in + o_x, g_ref[0, sl, :], g_out,
                                               is_lo).astype(BF16)

    n_groups = n_chunks // RET_UNROLL
    score_stage(0, 0)

    def inner(i, carry):
        value_stage(i, i % 2)
        score_stage(i + 1, (i + 1) % 2)
        return carry

    lax.fori_loop(0, n_groups - 1, inner, 0)
    value_stage(n_groups - 1, (n_groups - 1) % 2)


def _retention(q, k, v, g, lg_pairs, gout_pairs):
    b, s, _ = q.shape
    blk = pl.BlockSpec((1, s, LANES), lambda i, p: (i, 0, p))
    tab = pl.BlockSpec((1, 1, LANES), lambda i, p: (p, 0, 0))
    return pl.pallas_call(
        functools.partial(_retention_kernel, n_chunks=s // RET_CHUNK),
        grid=(b, N_PAIRS),
        in_specs=[tab, tab, blk, blk, blk, blk],
        out_specs=[blk, pl.BlockSpec((1, 2, HEAD_DIM, HEAD_DIM), lambda i, p: (i, p, 0, 0))],
        out_shape=[jax.ShapeDtypeStruct((b, s, WIDTH), BF16),
                   jax.ShapeDtypeStruct((b, N_HEADS, HEAD_DIM, HEAD_DIM), F32)],
        scratch_shapes=[pltpu.VMEM((s // RET_CHUNK, LANES, LANES), F32),
                        pltpu.VMEM((2, 2 * RET_UNROLL, RET_CHUNK, RET_CHUNK), BF16)],
        compiler_params=_cparams(2),
        name="retention",
    )(lg_pairs, gout_pairs, q, k, v, g)


def _attn_scores(qb, kb, bias0, bias1, is_lo):
    qb = qb.astype(BF16)
    kb = kb.astype(BF16)
    zq = jnp.zeros_like(qb)
    s0 = _dot_nt(jnp.where(is_lo, qb, zq), kb) + bias0
    s1 = _dot_nt(jnp.where(is_lo, zq, qb), kb) + bias1
    return s0, s1


def _attn_values(s0, s1, vb, is_lo):
    m0 = jnp.max(s0, axis=-1, keepdims=True)
    m1 = jnp.max(s1, axis=-1, keepdims=True)
    p0 = jnp.exp(s0 - m0).astype(BF16)
    p1 = jnp.exp(s1 - m1).astype(BF16)
    is_lo_k = _pair_masks(vb.shape)
    one = jnp.ones_like(vb)
    r0 = _dot(p0, jnp.where(is_lo_k, vb, one).astype(BF16))
    r1 = _dot(p1, jnp.where(is_lo_k, one, vb).astype(BF16))
    acc = jnp.where(is_lo, r0, r1)
    l = pltpu.roll(jnp.where(is_lo, r1, r0), HEAD_DIM, 1)
    m = jnp.where(is_lo, m0, m1)
    return acc, m, l


def _dil_attn_kernel(sl_ref, q_ref, k_ref, v_ref, o_ref,
                     qf_ref, kf_ref, vf_ref, acc_ref, m_ref, l_ref, bias_ref, bias1_ref, s_ref, *,
                     seq):
    blk = ATT_BLOCK
    is_lo = _pair_masks((blk, LANES))
    slope = sl_ref[0]
    slope0 = slope[:, 0:1]
    slope1 = slope[:, HEAD_DIM:HEAD_DIM + 1]

    qf_ref[...] = q_ref[0].astype(F32) * (HEAD_DIM ** -0.5)
    kf_ref[...] = k_ref[0].astype(F32)
    vf_ref[...] = v_ref[0].astype(F32)

    @pl.when(pl.program_id(1) == 0)
    def _():
        i2 = lax.broadcasted_iota(I32, (blk, 2 * blk), 0)
        j2 = lax.broadcasted_iota(I32, (blk, 2 * blk), 1)
        rel2 = blk + i2 - j2
        ok2 = (rel2 >= 0) & (rel2 <= WIN_STEPS)
        i1 = lax.broadcasted_iota(I32, (blk, blk), 0)
        j1 = lax.broadcasted_iota(I32, (blk, blk), 1)
        rel1 = i1 - j1
        ok1 = rel1 >= 0
        for pi, d in enumerate(DILATIONS):
            dist2 = (rel2 * d).astype(F32)
            dist1 = (rel1 * d).astype(F32)
            bias_ref[2 * pi] = jnp.where(ok2, -slope0 * dist2, NEG_INF)
            bias_ref[2 * pi + 1] = jnp.where(ok2, -slope1 * dist2, NEG_INF)
            bias1_ref[2 * pi] = jnp.where(ok1, -slope0 * dist1, NEG_INF)
            bias1_ref[2 * pi + 1] = jnp.where(ok1, -slope1 * dist1, NEG_INF)

    def rows(start, n, d):
        return pl.ds(start, n) if d == 1 else pl.ds(start, n, stride=d)

    def store(pi, sl, res):
        acc, m, l = res
        acc_ref[pi, sl, :] = acc
        m_ref[pi, sl, :] = m
        l_ref[pi, sl, :] = l

    def key_rows(q0, k0, d):
        return rows(q0, blk, d) if k0 is None else rows(k0, 2 * blk, d)

    def score_stage(pi, d, items, slot):
        for u, (q0, k0) in enumerate(items):
            b_ref, nk = (bias1_ref, blk) if k0 is None else (bias_ref, 2 * blk)
            s0, s1 = _attn_scores(qf_ref[rows(q0, blk, d), :], kf_ref[key_rows(q0, k0, d), :],
                                  b_ref[2 * pi], b_ref[2 * pi + 1], is_lo)
            s_ref[slot, 2 * u, :, :nk] = s0
            s_ref[slot, 2 * u + 1, :, :nk] = s1

    def value_stage(pi, d, items, slot):
        for u, (q0, k0) in enumerate(items):
            nk = blk if k0 is None else 2 * blk
            res = _attn_values(s_ref[slot, 2 * u, :, :nk], s_ref[slot, 2 * u + 1, :, :nk],
                               vf_ref[key_rows(q0, k0, d), :], is_lo)
            store(pi, rows(q0, blk, d), res)

    def pipeline(pi, d, n_groups, items_of):
        score_stage(pi, d, items_of(0), 0)

        def body(g, carry):
            value_stage(pi, d, items_of(g), g % 2)
            score_stage(pi, d, items_of(g + 1), (g + 1) % 2)
            return carry

        lax.fori_loop(0, n_groups - 1, body, 0)
        value_stage(pi, d, items_of(n_groups - 1), (n_groups - 1) % 2)

    for pi, d in enumerate(DILATIONS):
        n_blocks = seq // d // blk
        step = blk * d
        firsts = [(r, None) for r in range(d)]
        if n_blocks == 1:
            assert d % ATT_GROUP == 0
            pipeline(pi, d, d // ATT_GROUP,
                     lambda g: [(g * ATT_GROUP + u, None) for u in range(ATT_GROUP)])
            continue
        assert d <= ATT_GROUP
        score_stage(pi, d, firsts, 0)
        value_stage(pi, d, firsts, 0)
        per_group = max(p for p in range(1, ATT_GROUP // d + 1) if (n_blocks - 1) % p == 0)

        def general(g, d=d, step=step, per_group=per_group):
            items = []
            for u in range(per_group):
                base = (1 + g * per_group + u) * step
                if d == 1:
                    base = pl.multiple_of(base, blk)
                items += [(base + r, base + r - step) for r in range(d)]
            return items

        pipeline(pi, d, (n_blocks - 1) // per_group, general)

    def merge(c, carry):
        sl = pl.ds(pl.multiple_of(c * 256, 256), 256)
        m_all = [m_ref[pi, sl, :] for pi in range(3)]
        m_max = jnp.maximum(jnp.maximum(m_all[0], m_all[1]), m_all[2])
        num = jnp.zeros((256, LANES), F32)
        den = jnp.zeros((256, LANES), F32)
        for pi in range(3):
            w = jnp.exp(m_all[pi] - m_max)
            num = num + w * acc_ref[pi, sl, :]
            den = den + w * l_ref[pi, sl, :]
        o_ref[0, sl, :] = (num / den).astype(BF16)
        return carry

    lax.fori_loop(0, seq // 256, merge, 0)


def _dilated_attention(qa, ka, va, slope_pairs):
    b, s, _ = qa.shape
    blk = pl.BlockSpec((1, s, LANES), lambda p, i: (i, 0, p))
    tab = pl.BlockSpec((1, 1, LANES), lambda p, i: (p, 0, 0))
    return pl.pallas_call(
        functools.partial(_dil_attn_kernel, seq=s),
        grid=(N_PAIRS, b),
        in_specs=[tab, blk, blk, blk],
        out_specs=blk,
        out_shape=jax.ShapeDtypeStruct((b, s, WIDTH), BF16),
        scratch_shapes=[
            pltpu.VMEM((s, LANES), F32),
            pltpu.VMEM((s, LANES), F32),
            pltpu.VMEM((s, LANES), F32),
            pltpu.VMEM((3, s, LANES), F32),
            pltpu.VMEM((3, s, LANES), F32),
            pltpu.VMEM((3, s, LANES), F32),
            pltpu.VMEM((6, ATT_BLOCK, 2 * ATT_BLOCK), F32),
            pltpu.VMEM((6, ATT_BLOCK, ATT_BLOCK), F32),
            pltpu.VMEM((2, 2 * ATT_GROUP, ATT_BLOCK, 2 * ATT_BLOCK), F32),
        ],
        compiler_params=_cparams(2),
        name="dilated_attention",
    )(slope_pairs, qa, ka, va)


def _sample_mixer_kernel(lg_ref, gout_ref, slope_ref, qr_ref, kr_ref, vr_ref, gr_ref,
                         qa_ref, ka_ref, va_ref, st_ref, ck_ref, cv_ref,
                         r_ref, a_ref, sto_ref, bias_ref, cnt_ref, *, n_new, w_buf):
    n = n_new
    scale = HEAD_DIM ** -0.5
    lg = lg_ref[...]
    qr = qr_ref[...]
    kr = kr_ref[...]
    vr = vr_ref[...]
    ri = lax.broadcasted_iota(I32, (n, n), 0)
    rj = lax.broadcasted_iota(I32, (n, n), 1)
    rel = (ri - rj).astype(F32)
    rowf = lax.broadcasted_iota(I32, (n, HEAD_DIM), 0).astype(F32)
    outs = []
    for h in range(N_HEADS):
        hs = slice(h * HEAD_DIM, (h + 1) * HEAD_DIM)
        lg_h = lg[:, h * HEAD_DIM:h * HEAD_DIM + 1]
        qh, kh, vh = qr[:, hs], kr[:, hs], vr[:, hs]
        dm = jnp.where(rel >= 0.0, jnp.exp(lg_h * jnp.maximum(rel, 0.0)), 0.0) * scale
        s = _dot_nt(qh, kh) * dm
        s_prev = st_ref[0, h]
        o = _dot(s, vh) + _dot(qh, s_prev) * jnp.exp(lg_h * (rowf + 1.0))
        kw = kh * (jnp.exp(lg_h * (n - 1.0 - rowf)) * scale)
        sto_ref[0, h] = jnp.exp(lg_h * float(n)) * s_prev + _dot_tn(kw, vh)
        outs.append(o)
    o_r = jnp.concatenate(outs, axis=1)
    normed = []
    for p in range(N_PAIRS):
        ps = slice(p * LANES, (p + 1) * LANES)
        is_lo = _pair_masks((n, LANES))
        normed.append(_group_norm_gate(o_r[:, ps], gr_ref[:, ps], gout_ref[:, ps], is_lo))
    r_ref[...] = jnp.concatenate(normed, axis=1)

    n_rows = N_HEADS * n
    n_keys = w_buf + LANES

    @pl.when(pl.program_id(0) == 0)
    def _():
        rr = lax.broadcasted_iota(I32, (n_rows, n_keys), 0)
        cc = lax.broadcasted_iota(I32, (n_rows, n_keys), 1)
        qi = rr % n
        dist = w_buf + qi - cc
        valid = (dist >= 0) & (cc < w_buf + n)
        cnt = jnp.zeros((n_rows, n_keys), F32)
        for d in DILATIONS:
            hit = valid & (dist % d == 0) & (dist <= WIN_STEPS * d)
            cnt = cnt + jnp.where(hit, 1.0, 0.0)
        cnt_ref[...] = cnt
        slope = slope_ref[...]
        srow = jnp.zeros((n_rows, 1), F32)
        r1 = lax.broadcasted_iota(I32, (n_rows, 1), 0)
        for h in range(N_HEADS):
            srow = jnp.where(r1 // n == h, slope[:, h * HEAD_DIM:h * HEAD_DIM + 1], srow)
        bias_ref[...] = jnp.where(cnt > 0.0, -srow * dist.astype(F32), NEG_INF)

    qa = qa_ref[...]
    q_rows = jnp.concatenate([qa] * N_HEADS, axis=0)
    r2 = lax.broadcasted_iota(I32, (n_rows, WIDTH), 0)
    c2 = lax.broadcasted_iota(I32, (n_rows, WIDTH), 1)
    q_blk = jnp.where(r2 // n == c2 // HEAD_DIM, q_rows, 0.0).astype(BF16)
    pad = jnp.zeros((LANES - n, WIDTH), F32)
    k_new = jnp.concatenate([ka_ref[...], pad], axis=0).astype(BF16)
    v_new = jnp.concatenate([va_ref[...], pad], axis=0).astype(BF16)
    s_old = _dot(q_blk, ck_ref[0].astype(BF16))
    s_new = _dot_nt(q_blk, k_new)
    s = jnp.concatenate([s_old, s_new], axis=1) * scale + bias_ref[...]
    m = jnp.max(s, axis=-1, keepdims=True)
    p = cnt_ref[...] * jnp.exp(s - m)
    den = jnp.sum(p, axis=-1, keepdims=True)
    p = p.astype(BF16)
    o_all = _dot_nt(p[:, :w_buf], cv_ref[0].astype(BF16)) + _dot(p[:, w_buf:], v_new)
    o_all = o_all / den
    heads = [o_all[h * n:(h + 1) * n, h * HEAD_DIM:(h + 1) * HEAD_DIM] for h in range(N_HEADS)]
    a_ref[...] = jnp.concatenate(heads, axis=1)


def _sample_mixer(lg_lane, gout, slope_lane, qr, kr, vr, gr, qa, ka, va, state, ck, cv, n_new):
    b = state.shape[0]
    w_buf = ck.shape[2]
    tok = pl.BlockSpec((n_new, WIDTH), lambda i: (i, 0))
    tab = pl.BlockSpec((1, WIDTH), lambda i: (0, 0))
    st = pl.BlockSpec((1, N_HEADS, HEAD_DIM, HEAD_DIM), lambda i: (i, 0, 0, 0))
    cache = pl.BlockSpec((1, WIDTH, w_buf), lambda i: (i, 0, 0))
    n_rows = N_HEADS * n_new
    return pl.pallas_call(
        functools.partial(_sample_mixer_kernel, n_new=n_new, w_buf=w_buf),
        grid=(b,),
        in_specs=[tab, tab, tab, tok, tok, tok, tok, tok, tok, tok, st, cache, cache],
        out_specs=[tok, tok, st],
        out_shape=[jax.ShapeDtypeStruct((b * n_new, WIDTH), F32),
                   jax.ShapeDtypeStruct((b * n_new, WIDTH), F32),
                   jax.ShapeDtypeStruct(state.shape, F32)],
        scratch_shapes=[pltpu.VMEM((n_rows, w_buf + LANES), F32),
                        pltpu.VMEM((n_rows, w_buf + LANES), F32)],
        compiler_params=_cparams(1),
        name="sample_mixer",
    )(lg_lane, gout, slope_lane, qr, kr, vr, gr, qa, ka, va, state, ck, cv)


def _mem_kv_kernel(x_ref, g_ref, wk_ref, wv_ref, gk_ref, k_ref, v_ref, kb_ref, vb_ref):
    h = _rms(x_ref[...], g_ref[...]).astype(BF16)
    k = _dot(h, wk_ref[...])
    gk = gk_ref[...]
    v = _dot(h, wv_ref[...])
    vb_ref[...] = v.astype(BF16)
    for hd in range(N_MEM_HEADS):
        hs = slice(hd * MEM_HEAD_DIM, (hd + 1) * MEM_HEAD_DIM)
        kh = _rms(k[:, hs], gk)
        k_ref[:, hd, :] = kh
        kb_ref[:, hs] = kh.astype(BF16)
        v_ref[:, hd, :] = v[:, hs]


def _mem_kv(mem2d, g_mem, w_mk_bf, w_mv_bf, g_k_mem, tm=256):
    t = mem2d.shape[0]
    const = lambda i: (0, 0)
    row = lambda i: (i, 0)
    out = jax.ShapeDtypeStruct((t, N_MEM_HEADS, MEM_HEAD_DIM), F32)
    return pl.pallas_call(
        _mem_kv_kernel,
        grid=(t // tm,),
        in_specs=[pl.BlockSpec((tm, D_MODEL), row), pl.BlockSpec((1, D_MODEL), const),
                  pl.BlockSpec((D_MODEL, D_MODEL), const), pl.BlockSpec((D_MODEL, D_MODEL), const),
                  pl.BlockSpec((1, MEM_HEAD_DIM), const)],
        out_specs=[pl.BlockSpec((tm, N_MEM_HEADS, MEM_HEAD_DIM), lambda i: (i, 0, 0))] * 2
        + [pl.BlockSpec((tm, D_MODEL), row)] * 2,
        out_shape=[out, out] + [jax.ShapeDtypeStruct((t, D_MODEL), BF16)] * 2,
        compiler_params=_cparams(1),
        name="mem_kv",
    )(mem2d, g_mem, w_mk_bf, w_mv_bf, g_k_mem)


def _cross_router_kernel(x_ref, r_ref, a_ref, wo_ref, gc_ref, wmq_ref, gqm_ref, mk_ref, mv_ref,
                         wmo_ref, gmoe_ref, wrh_ref, wrl_ref, br_ref, cin_ref,
                         y_ref, h_ref, gate_ref, *rest, tq, n_split, sorted_mode, native_mem):
    n_scratch = 3 if native_mem else 1
    if native_mem:
        carry_ref, mem_ref, msem = rest[-n_scratch:]
    else:
        carry_ref, = rest[-n_scratch:]
    if sorted_mode:
        lpos_ref, tab_ref, cnt_ref = rest[:-n_scratch]
    else:
        pair_ref, cnt_ref = rest[:-n_scratch]
    first = (pl.program_id(0) == 0) & (pl.program_id(1) == 0)

    @pl.when(first)
    def _():
        carry_ref[...] = cin_ref[...]

    half = D_MODEL // 2
    gqm = gqm_ref[...]
    head_slices = [slice(hd * MEM_HEAD_DIM, (hd + 1) * MEM_HEAD_DIM) for hd in range(N_MEM_HEADS)]
    if native_mem:
        n_mem = mem_ref.shape[1]
        b0 = pl.program_id(0) * n_mem
        copies = [pltpu.make_async_copy(src.at[b0 + j, :, hd, :], mem_ref.at[kv, j, hd], msem)
                  for kv, src in enumerate((mk_ref, mv_ref))
                  for j in range(n_mem) for hd in range(N_MEM_HEADS)]
        for cp in copies:
            cp.start()

        def memory():
            for cp in copies:
                cp.wait()
            return [[[mem_ref[kv, j, hd].astype(BF16) for hd in range(N_MEM_HEADS)]
                     for j in range(n_mem)] for kv in range(2)]
    else:
        n_mem = mk_ref.shape[0]
        mem_kv = [[[ref[j, :, hs].astype(BF16) for hs in head_slices] for j in range(n_mem)]
                  for ref in (mk_ref, mv_ref)]
        memory = lambda: mem_kv

    def attend(q, mem_k, mem_v):
        heads = []
        for hd, hs in enumerate(head_slices):
            qh = _rms(q[:, hs], gqm).astype(BF16)
            s = _dot_nt(qh, mem_k[hd]) * (MEM_HEAD_DIM ** -0.5)
            m = jnp.max(s, axis=-1, keepdims=True)
            p = jnp.exp(s - m)
            p = p / jnp.sum(p, axis=-1, keepdims=True)
            heads.append(_dot(p.astype(BF16), mem_v[hd]))
        return jnp.concatenate(heads, axis=1)

    def rows_block(rs):
        ra = jnp.concatenate([r_ref[rs, :], a_ref[rs, :]], axis=1).astype(BF16)
        y1 = x_ref[rs, :] + _dot(ra, wo_ref[...])
        q = _dot(_rms(y1, gc_ref[...]).astype(BF16), wmq_ref[...])
        mem_k, mem_v = memory()
        per = q.shape[0] // n_mem
        o = jnp.concatenate([attend(q[j * per:(j + 1) * per], mem_k[j], mem_v[j])
                             for j in range(n_mem)], axis=0)
        y2 = y1 + _dot(o.astype(BF16), wmo_ref[...])
        y_ref[rs, :] = y2
        hm = _rms(y2, gmoe_ref[...])
        if sorted_mode:
            h_ref[rs, :] = hm.astype(BF16)
        else:
            h_ref[rs, :] = pltpu.pack_elementwise([hm[:, :half], hm[:, half:]], packed_dtype=BF16)
        hi = hm.astype(BF16)
        lo = (hm - hi.astype(F32)).astype(BF16)
        return _dot(hi, wrh_ref[...]) + _dot(lo, wrh_ref[...]) + _dot(hi, wrl_ref[...])

    sub = tq // n_split
    logits = jnp.concatenate([rows_block(pl.ds(i * sub, sub)) for i in range(n_split)], axis=0)
    logits = logits + br_ref[...]

    rb = min(tq, ROUTE_BLOCK)
    lane = lax.broadcasted_iota(I32, (rb, LANES), 1)
    lanef = lane.astype(F32)
    big = float(LANES)
    is_grp = lane < N_GROUPS
    ti = lax.broadcasted_iota(I32, (rb, rb), 0)
    tj = lax.broadcasted_iota(I32, (rb, rb), 1)
    tri = jnp.where(tj < ti, 1.0, 0.0)
    ui = lax.broadcasted_iota(I32, (LANES, LANES), 0)
    uj = lax.broadcasted_iota(I32, (LANES, LANES), 1)
    upper = jnp.where(ui < uj, 1.0, 0.0).astype(BF16)

    def pick(hit, table):
        return jnp.sum(jnp.where(hit, table, 0.0), axis=-1, keepdims=True)

    def lanes_of(vals):
        out = jnp.zeros((rb, LANES), F32)
        for idx, val in enumerate(vals):
            out = jnp.where(lane == idx, val, out)
        return out

    for blk_i in range(tq // rb):
        rs = slice(blk_i * rb, (blk_i + 1) * rb)
        lg = logits[rs]
        gl = jnp.where(is_grp, lg, NEG_INF)
        gmax = jnp.max(gl, axis=-1, keepdims=True)
        gidx = jnp.min(jnp.where(gl == gmax, lanef, big), axis=-1, keepdims=True)
        p_top = 1.0 / jnp.sum(jnp.where(is_grp, jnp.exp(lg - gmax), 0.0), axis=-1, keepdims=True)
        lo = EXPERT_LANE0 + N_EXP_PER_GROUP * gidx
        in_grp = (lanef >= lo) & (lanef < lo + N_EXP_PER_GROUP)
        el = jnp.where(in_grp, lg, NEG_INF)
        v1 = jnp.max(el, axis=-1, keepdims=True)
        i1 = jnp.min(jnp.where(el == v1, lanef, big), axis=-1, keepdims=True)
        el2 = jnp.where(lanef == i1, NEG_INF, el)
        v2 = jnp.max(el2, axis=-1, keepdims=True)
        i2 = jnp.min(jnp.where(el2 == v2, lanef, big), axis=-1, keepdims=True)
        e21 = jnp.exp(v2 - v1)
        w1 = p_top / (1.0 + e21)
        w2 = p_top * e21 / (1.0 + e21)

        hit1 = lanef == i1
        hit2 = lanef == i2
        onehot = jnp.where(hit1 | hit2, 1.0, 0.0)
        if rb >= 16:
            within = _dot(tri.astype(BF16), onehot.astype(BF16))
        else:
            within = _dot(tri, onehot)
        carry = carry_ref[...]
        count = jnp.sum(onehot, axis=0, keepdims=True)

        if sorted_mode:
            eighths = jnp.floor((count + (RUN_ALIGN - 1.0)) * (1.0 / RUN_ALIGN))
            carry_ref[...] = carry + eighths * float(RUN_ALIGN)
            lstart = (_dot(jnp.broadcast_to(eighths, (8, LANES)).astype(BF16), upper)[0:1]
                      * float(RUN_ALIGN))
            lpos1 = pick(hit1, within + lstart)
            lpos2 = pick(hit2, within + lstart)
            gate_ref[rs, :] = lanes_of((w1, w2, lpos1, lpos2))[:, :4]
            lpos_ref[blk_i] = lanes_of((lpos1, lpos2)).T[:8]
            tab_ref[blk_i] = jnp.concatenate(
                [count, lstart, carry, jnp.zeros((5, LANES), F32)], axis=0)
        else:
            carry_ref[...] = carry + count
            e1 = i1 - EXPERT_LANE0
            e2 = i2 - EXPERT_LANE0
            rank1 = pick(hit1, within + carry)
            rank2 = pick(hit2, within + carry)
            gate_ref[rs, :] = lanes_of((w1, w2))[:, :2]
            pair_ref[rs, :] = lanes_of((e1, e2, rank1, rank2))[:, :4].astype(I32)
    cnt_ref[...] = carry_ref[...]


def _cross_router(x2d, r, a, w_o, g_cross, w_mq, g_q_mem, mk, mv, w_mo, g_moe, w_r_hi, w_r_lo, b_r,
                  cnt_in, n_batch, tq, n_split, sorted_mode):
    t = x2d.shape[0]
    rb = min(tq, ROUTE_BLOCK)
    n_rb = tq // rb
    mem_per_step = max(1, tq // (t // n_batch))
    n_batch = n_batch // mem_per_step
    assert mem_per_step == 1 or n_split == 1
    native_mem = mk.ndim == 4
    scratch = [pltpu.VMEM((1, LANES), F32)]
    if native_mem:
        scratch += [pltpu.VMEM((2, mem_per_step, N_MEM_HEADS, N_MEM, MEM_HEAD_DIM), F32),
                    pltpu.SemaphoreType.DMA(())]
    per_b = t // n_batch // tq
    row = lambda i, j: (i * per_b + j, 0)
    const = lambda i, j: (0, 0)
    if native_mem:
        mem = pl.BlockSpec(memory_space=pl.ANY)
    else:
        mem = pl.BlockSpec((mem_per_step, N_MEM, D_MODEL), lambda i, j: (i, 0, 0))
    wspec = pl.BlockSpec((D_MODEL, D_MODEL), const)
    vec = pl.BlockSpec((1, D_MODEL), const)
    rspec = pl.BlockSpec((D_MODEL, LANES), const)
    blk3 = lambda i, j: (i * per_b + j, 0, 0)
    if sorted_mode:
        out_specs = [pl.BlockSpec((tq, D_MODEL), row), pl.BlockSpec((tq, D_MODEL), row),
                     pl.BlockSpec((tq, 4), row), pl.BlockSpec((n_rb, 8, rb), blk3),
                     pl.BlockSpec((n_rb, 8, LANES), blk3)]
        out_shape = [jax.ShapeDtypeStruct((t, D_MODEL), F32),
                     jax.ShapeDtypeStruct((t, D_MODEL), BF16),
                     jax.ShapeDtypeStruct((t, 4), F32),
                     jax.ShapeDtypeStruct((t // rb, 8, rb), F32),
                     jax.ShapeDtypeStruct((t // rb, 8, LANES), F32)]
    else:
        out_specs = [pl.BlockSpec((tq, D_MODEL), row), pl.BlockSpec((tq, D_MODEL // 2), row),
                     pl.BlockSpec((tq, 2), row), pl.BlockSpec((tq, 4), row)]
        out_shape = [jax.ShapeDtypeStruct((t, D_MODEL), F32),
                     jax.ShapeDtypeStruct((t, D_MODEL // 2), jnp.uint32),
                     jax.ShapeDtypeStruct((t, 2), F32),
                     jax.ShapeDtypeStruct((t, 4), I32)]
    return pl.pallas_call(
        functools.partial(_cross_router_kernel, tq=tq, n_split=n_split, sorted_mode=sorted_mode,
                          native_mem=native_mem),
        grid=(n_batch, per_b),
        in_specs=[pl.BlockSpec((tq, D_MODEL), row), pl.BlockSpec((tq, WIDTH), row),
                  pl.BlockSpec((tq, WIDTH), row), wspec, vec, wspec,
                  pl.BlockSpec((1, MEM_HEAD_DIM), const), mem, mem, wspec, vec,
                  rspec, rspec, pl.BlockSpec((1, LANES), const),
                  pl.BlockSpec((1, LANES), const)],
        out_specs=out_specs + [pl.BlockSpec((1, LANES), const)],
        out_shape=out_shape + [jax.ShapeDtypeStruct((1, LANES), F32)],
        scratch_shapes=scratch,
        compiler_params=_cparams(2),
        name="cross_router",
    )(x2d, r, a, w_o, g_cross, w_mq, g_q_mem, mk, mv, w_mo, g_moe, w_r_hi, w_r_lo, b_r, cnt_in)


def _row_copy(src, src_row, dst, dst_row, sem):
    return pltpu.make_async_copy(src.at[pl.ds(src_row, 1)], dst.at[pl.ds(dst_row, 1)], sem)


def _pair_slot(off_ref, pairs_ref, j, kk):
    return off_ref[pairs_ref[0, 0, 4 * j + kk]] + pairs_ref[0, 0, 4 * j + 2 + kk]


def _dispatch_kernel(off_ref, pairs_ref, hp_ref, xs_in_ref, xs_ref, sem, *, tq):
    del xs_in_ref

    def start(j, carry):
        for kk in range(2):
            _row_copy(hp_ref, j, xs_ref, _pair_slot(off_ref, pairs_ref, j, kk), sem).start()
        return carry

    def wait(j, carry):
        for kk in range(2):
            _row_copy(hp_ref, j, xs_ref, _pair_slot(off_ref, pairs_ref, j, kk), sem).wait()
        return carry

    lax.fori_loop(0, tq, start, 0, unroll=8)
    lax.fori_loop(0, tq, wait, 0, unroll=8)


def _dispatch(offsets, pairs, hp, xs, tq):
    t = hp.shape[0]
    return pl.pallas_call(
        functools.partial(_dispatch_kernel, tq=tq),
        grid_spec=pltpu.PrefetchScalarGridSpec(
            num_scalar_prefetch=1, grid=(t // tq,),
            in_specs=[pl.BlockSpec((1, 1, 4 * tq), lambda i, off: (i, 0, 0),
                                   memory_space=pltpu.SMEM),
                      pl.BlockSpec((tq, D_MODEL // 2), lambda i, off: (i, 0)),
                      pl.BlockSpec(memory_space=pl.ANY)],
            out_specs=pl.BlockSpec(memory_space=pl.ANY),
            scratch_shapes=[pltpu.SemaphoreType.DMA(())]),
        out_shape=jax.ShapeDtypeStruct(xs.shape, xs.dtype),
        input_output_aliases={3: 0},
        compiler_params=pltpu.CompilerParams(dimension_semantics=("arbitrary",),
                                             has_side_effects=True),
        name="moe_dispatch",
    )(offsets, pairs.reshape(t // tq, 1, 4 * tq), hp, xs)


def _unpack_pair(words):
    lo = pltpu.unpack_elementwise(words, index=0, packed_dtype=BF16, unpacked_dtype=F32)
    hi = pltpu.unpack_elementwise(words, index=1, packed_dtype=BF16, unpacked_dtype=F32)
    return lo, hi


def _expert_kernel(te_ref, nt_ref, xs_ref, wg_ref, wu_ref, wd_ref, ys_ref):
    del te_ref
    half = D_MODEL // 2

    @pl.when(pl.program_id(0) < nt_ref[0])
    def _():
        wg = wg_ref[0].astype(BF16)
        wu = wu_ref[0].astype(BF16)
        wd = wd_ref[0].astype(BF16)
        sub = xs_ref.shape[0] // EXPERT_SPLIT
        for i in range(EXPERT_SPLIT):
            rs = pl.ds(i * sub, sub)
            lo, hi = _unpack_pair(xs_ref[rs, :])
            lo = lo.astype(BF16)
            hi = hi.astype(BF16)
            g = _dot(lo, wg[:half]) + _dot(hi, wg[half:])
            u = _dot(lo, wu[:half]) + _dot(hi, wu[half:])
            hid = (g * (1.0 / (1.0 + jnp.exp(-g))) * u).astype(BF16)
            y = _dot(hid, wd)
            ys_ref[rs, :] = pltpu.pack_elementwise([y[:, :half], y[:, half:]], packed_dtype=BF16)

    @pl.when(pl.program_id(0) >= nt_ref[0])
    def _():
        ys_ref[...] = jnp.zeros_like(ys_ref)


def _experts(tile_expert, n_tiles, xs, w_gate, w_up, w_down):
    rows = xs.shape[0]
    n_max = rows // MOE_ROW_TILE

    def xmap(i, te, nt):
        return (jnp.maximum(jnp.minimum(i, nt[0] - 1), 0), 0)

    def wmap(i, te, nt):
        return (te[i], 0, 0)

    return pl.pallas_call(
        _expert_kernel,
        grid_spec=pltpu.PrefetchScalarGridSpec(
            num_scalar_prefetch=2,
            grid=(n_max,),
            in_specs=[pl.BlockSpec((MOE_ROW_TILE, D_MODEL // 2), xmap),
                      pl.BlockSpec((1, D_MODEL, D_EXPERT), wmap),
                      pl.BlockSpec((1, D_MODEL, D_EXPERT), wmap),
                      pl.BlockSpec((1, D_EXPERT, D_MODEL), wmap)],
            out_specs=pl.BlockSpec((MOE_ROW_TILE, D_MODEL // 2), lambda i, te, nt: (i, 0)),
        ),
        out_shape=jax.ShapeDtypeStruct(xs.shape, xs.dtype),
        compiler_params=_cparams(1),
        name="moe_experts",
    )(tile_expert, n_tiles, xs, w_gate, w_up, w_down)


def _combine_kernel(off_ref, pairs_ref, pairs_next_ref, ys_ref, y_ref, gate_ref, o_ref,
                    buf_ref, sem, *, tq):
    step = pl.program_id(0)
    n_steps = pl.num_programs(0)

    def gather(p_ref, slot, wait):
        def body(j, carry):
            for kk in range(2):
                cp = pltpu.make_async_copy(ys_ref.at[pl.ds(_pair_slot(off_ref, p_ref, j, kk), 1)],
                                           buf_ref.at[slot, kk, pl.ds(j, 1)], sem.at[slot])
                cp.wait() if wait else cp.start()
            return carry
        lax.fori_loop(0, tq, body, 0, unroll=8)

    @pl.when(step == 0)
    def _():
        gather(pairs_ref, 0, wait=False)

    @pl.when(step + 1 < n_steps)
    def _():
        gather(pairs_next_ref, (step + 1) % 2, wait=False)

    slot = step % 2
    gather(pairs_ref, slot, wait=True)
    half = D_MODEL // 2
    gates = gate_ref[...]
    w1 = gates[:, 0:1]
    w2 = gates[:, 1:2]
    lo1, hi1 = _unpack_pair(buf_ref[slot, 0])
    lo2, hi2 = _unpack_pair(buf_ref[slot, 1])
    y = y_ref[...]
    o_ref[:, :half] = y[:, :half] + (w1 * lo1 + w2 * lo2)
    o_ref[:, half:] = y[:, half:] + (w1 * hi1 + w2 * hi2)


def _combine(offsets, pairs, ys, y2, gates, tq):
    t = y2.shape[0]
    n = t // tq
    row = lambda i, off: (i, 0)
    smem = lambda f: pl.BlockSpec((1, 1, 4 * tq), f, memory_space=pltpu.SMEM)
    pairs3 = pairs.reshape(n, 1, 4 * tq)
    return pl.pallas_call(
        functools.partial(_combine_kernel, tq=tq),
        grid_spec=pltpu.PrefetchScalarGridSpec(
            num_scalar_prefetch=1, grid=(n,),
            in_specs=[smem(lambda i, off: (i, 0, 0)),
                      smem(lambda i, off: (jnp.minimum(i + 1, n - 1), 0, 0)),
                      pl.BlockSpec(memory_space=pl.ANY),
                      pl.BlockSpec((tq, D_MODEL), row), pl.BlockSpec((tq, 2), row)],
            out_specs=pl.BlockSpec((tq, D_MODEL), row),
            scratch_shapes=[pltpu.VMEM((2, 2, tq, D_MODEL // 2), jnp.uint32),
                            pltpu.SemaphoreType.DMA((2,))]),
        out_shape=jax.ShapeDtypeStruct((t, D_MODEL), F32),
        compiler_params=_cparams(1),
        name="moe_combine",
    )(offsets, pairs3, pairs3, ys, y2, gates)


def _for_each_chunk(tot_ref, rows_ref, blk, fn):
    base = blk * SORT_CHUNKS

    def body(k, carry):
        fn(pl.multiple_of(k * RUN_ALIGN, RUN_ALIGN), pl.multiple_of(rows_ref[base + k], RUN_ALIGN))
        return carry

    lax.fori_loop(0, tot_ref[blk], body, 0)


def _sorted_dispatch_kernel(tot_ref, rows_ref, zt_ref, nt_ref, h_ref, lpos_ref, xs_ref,
                            stage_ref, zero_ref, sem, zsem, *, n_zero, n_tiles_max):
    step = pl.program_id(0)
    n_steps = pl.num_programs(0)

    @pl.when(step == 0)
    def _():
        zero_ref[...] = jnp.zeros_like(zero_ref)

        def fill(granule):
            row = pl.multiple_of(granule * ZERO_ROWS, ZERO_ROWS)
            return pltpu.make_async_copy(zero_ref, xs_ref.at[pl.ds(row, ZERO_ROWS)], zsem)

        def tail(do):
            def body(tile, carry):
                do(fill(tile))
                return carry
            lax.fori_loop(nt_ref[0], n_tiles_max, body, 0)

        for i in range(n_zero):
            pl.when(zt_ref[i] >= 0)(fill(jnp.maximum(zt_ref[i], 0)).start)
        tail(lambda cp: cp.start())
        for i in range(n_zero):
            pl.when(zt_ref[i] >= 0)(fill(jnp.maximum(zt_ref[i], 0)).wait)
        tail(lambda cp: cp.wait())

    half = D_MODEL // 2
    n_slots = lpos_ref.shape[0]
    rb = h_ref.shape[0] // n_slots
    jrow = lax.broadcasted_iota(I32, (SORT_ROWS, rb), 0).astype(F32)

    def copy(s):
        return lambda l, g: pltpu.make_async_copy(stage_ref.at[s, pl.ds(l, RUN_ALIGN)],
                                                  xs_ref.at[pl.ds(g, RUN_ALIGN)], sem.at[s])

    for s in range(n_slots):
        blk = step * n_slots + s

        @pl.when(step > 0)
        def _():
            _for_each_chunk(tot_ref, rows_ref, blk - n_slots, lambda l, g: copy(s)(l, g).wait())

        lpos = lpos_ref[s]
        perm = jnp.where((jrow == lpos[0:1]) | (jrow == lpos[1:2]), 1.0, 0.0).astype(BF16)
        x = _dot(perm, h_ref[s * rb:(s + 1) * rb, :])
        stage_ref[s] = pltpu.pack_elementwise([x[:, :half], x[:, half:]], packed_dtype=BF16)
        _for_each_chunk(tot_ref, rows_ref, blk, lambda l, g: copy(s)(l, g).start(priority=s % 2))

    @pl.when(step == n_steps - 1)
    def _():
        for s in range(n_slots):
            _for_each_chunk(tot_ref, rows_ref, step * n_slots + s,
                            lambda l, g: copy(s)(l, g).wait())


def _sorted_dispatch(chunk_total, chunk_rows, zero_tiles, n_tiles, h_bf, lpos_t, n_rows):
    n_blocks, _, rb = lpos_t.shape
    per_step = SORT_BLOCKS_PER_STEP
    return pl.pallas_call(
        functools.partial(_sorted_dispatch_kernel, n_zero=zero_tiles.shape[0],
                          n_tiles_max=n_rows // ZERO_ROWS),
        grid_spec=pltpu.PrefetchScalarGridSpec(
            num_scalar_prefetch=4, grid=(n_blocks // per_step,),
            in_specs=[pl.BlockSpec((per_step * rb, D_MODEL), lambda i, *_: (i, 0)),
                      pl.BlockSpec((per_step, 8, rb), lambda i, *_: (i, 0, 0))],
            out_specs=pl.BlockSpec(memory_space=pl.ANY),
            scratch_shapes=[pltpu.VMEM((per_step, SORT_ROWS, D_MODEL // 2), jnp.uint32),
                            pltpu.VMEM((ZERO_ROWS, D_MODEL // 2), jnp.uint32),
                            pltpu.SemaphoreType.DMA((per_step,)), pltpu.SemaphoreType.DMA(())]),
        out_shape=jax.ShapeDtypeStruct((n_rows, D_MODEL // 2), jnp.uint32),
        compiler_params=pltpu.CompilerParams(dimension_semantics=("arbitrary",),
                                             has_side_effects=True, vmem_limit_bytes=VMEM_LIMIT),
        name="moe_sorted_dispatch",
    )(chunk_total, chunk_rows, zero_tiles, n_tiles, h_bf, lpos_t)


def _sorted_combine_kernel(tot_ref, rows_ref, ys_ref, y_ref, gate_ref, o_ref, buf_ref, sem):
    step = pl.program_id(0)
    n_steps = pl.num_programs(0)

    def gather(blk, s, wait, priority=0):
        def fn(l, g):
            cp = pltpu.make_async_copy(ys_ref.at[pl.ds(g, RUN_ALIGN)],
                                       buf_ref.at[s, pl.ds(l, RUN_ALIGN)], sem.at[s])
            cp.wait() if wait else cp.start(priority=priority)
        _for_each_chunk(tot_ref, rows_ref,blk, fn)

    per_step = SORT_BLOCKS_PER_STEP
    n_slots = 2 * per_step
    first = step * per_step

    @pl.when(step == 0)
    def _():
        buf_ref[...] = jnp.zeros_like(buf_ref)
        for u in range(per_step):
            gather(u, u, wait=False, priority=u % 2)

    @pl.when(step + 1 < n_steps)
    def _():
        for u in range(per_step):
            gather(first + per_step + u, (first + per_step + u) % n_slots, wait=False,
                   priority=u % 2)

    half = D_MODEL // 2
    rb = y_ref.shape[0] // per_step
    jcol = lax.broadcasted_iota(I32, (rb, SORT_ROWS), 1).astype(F32)
    for u in range(per_step):
        slot = (first + u) % n_slots
        gather(first + u, slot, wait=True)
        rs = slice(u * rb, (u + 1) * rb)
        gates = gate_ref[rs, :]
        pick1 = jnp.where(jcol == gates[:, 2:3], 1.0, 0.0).astype(BF16)
        pick2 = jnp.where(jcol == gates[:, 3:4], 1.0, 0.0).astype(BF16)
        lo, hi = _unpack_pair(buf_ref[slot])
        lo = lo.astype(BF16)
        hi = hi.astype(BF16)
        w1 = gates[:, 0:1]
        w2 = gates[:, 1:2]
        y = y_ref[rs, :]
        o_ref[rs, :half] = y[:, :half] + (w1 * _dot(pick1, lo) + w2 * _dot(pick2, lo))
        o_ref[rs, half:] = y[:, half:] + (w1 * _dot(pick1, hi) + w2 * _dot(pick2, hi))


def _sorted_combine(chunk_total, chunk_rows, ys, y2, gates, rb):
    t = y2.shape[0]
    row = lambda i, *_: (i, 0)
    tq = SORT_BLOCKS_PER_STEP * rb
    return pl.pallas_call(
        _sorted_combine_kernel,
        grid_spec=pltpu.PrefetchScalarGridSpec(
            num_scalar_prefetch=2, grid=(t // tq,),
            in_specs=[pl.BlockSpec(memory_space=pl.ANY),
                      pl.BlockSpec((tq, D_MODEL), row), pl.BlockSpec((tq, 4), row)],
            out_specs=pl.BlockSpec((tq, D_MODEL), row),
            scratch_shapes=[pltpu.VMEM((2 * SORT_BLOCKS_PER_STEP, SORT_ROWS, D_MODEL // 2),
                                       jnp.uint32),
                            pltpu.SemaphoreType.DMA((2 * SORT_BLOCKS_PER_STEP,))]),
        out_shape=jax.ShapeDtypeStruct((t, D_MODEL), F32),
        compiler_params=_cparams(1),
        name="moe_sorted_combine",
    )(chunk_total, chunk_rows, ys, y2, gates)


def _lane_table(per_head):
    lanes = jnp.repeat(per_head.astype(F32), HEAD_DIM)
    return lanes.reshape(N_PAIRS, 1, LANES), lanes.reshape(1, WIDTH)


def kernel(x_prompt, x_sample, mem_prompt, state_ret, cache_win_k, cache_win_v, cache_mem_k,
           cache_mem_v, g_mix, w_in, g_ret_out, g_q_att, g_k_att, w_o, g_cross, g_mem, w_mq, w_mk,
           w_mv, g_q_mem, g_k_mem, w_mo, g_moe, w_router_group, b_router_group, w_router_expert,
           b_router_expert, w_exp_gate, w_exp_up, w_exp_down):
    depth = g_mix.shape[0]
    assert depth == 1
    b, s, _ = x_prompt.shape
    bs, ns, _ = x_sample.shape
    t_p, t_s = b * s, bs * ns
    l = 0

    heads = jnp.arange(N_HEADS, dtype=F32)
    lg_pairs, lg_lane = _lane_table(jnp.log(1.0 - 2.0 ** (-5.0 - heads)))
    slope_pairs, slope_lane = _lane_table(2.0 ** (-8.0 * (heads + 1.0) / N_HEADS))
    gout_lane = g_ret_out[l].reshape(1, WIDTH)
    gout_pairs = g_ret_out[l].reshape(N_PAIRS, 1, LANES)
    gq_t = jnp.tile(g_q_att[l], N_HEADS).reshape(1, WIDTH)
    gk_t = jnp.tile(g_k_att[l], N_HEADS).reshape(1, WIDTH)
    vec = lambda v: v.reshape(1, -1)
    w_in_bf = w_in[l].astype(BF16)
    w_o_bf, w_mq_bf, w_mk_bf = w_o[l].astype(BF16), w_mq[l].astype(BF16), w_mk[l].astype(BF16)
    w_mv_bf, w_mo_bf = w_mv[l].astype(BF16), w_mo[l].astype(BF16)
    gap = EXPERT_LANE0 - N_GROUPS
    tail = LANES - EXPERT_LANE0 - N_EXPERTS
    w_r = jnp.concatenate(
        [w_router_group[l], jnp.zeros((D_MODEL, gap), F32),
         jnp.moveaxis(w_router_expert[l], 0, 1).reshape(D_MODEL, N_EXPERTS),
         jnp.zeros((D_MODEL, tail), F32)], axis=1)
    b_r = jnp.concatenate([b_router_group[l], jnp.zeros((gap,), F32),
                           b_router_expert[l].reshape(-1), jnp.zeros((tail,), F32)]).reshape(1, LANES)
    w_r_hi = w_r.astype(BF16)
    w_r_lo = (w_r - w_r_hi.astype(F32)).astype(BF16)

    xp = x_prompt.reshape(t_p, D_MODEL)
    qr, kr, vr, gr, qa, ka, va, ka_t, va_t = _mix_proj(xp, vec(g_mix[l]), w_in_bf, gq_t, gk_t,
                                                       tm=MIX_TILE, act_dtype=BF16, seq=s)
    shp = lambda z: z.reshape(b, s, WIDTH)
    r_p, state_p = _retention(shp(qr), shp(kr), shp(vr), shp(gr), lg_pairs, gout_pairs)
    a_p = _dilated_attention(shp(qa), shp(ka), shp(va), slope_pairs)
    mk_p, mv_p, mk_bf, mv_bf = _mem_kv(mem_prompt.reshape(b * N_MEM, D_MODEL), vec(g_mem[l]), w_mk_bf, w_mv_bf,
                          vec(g_k_mem[l]))
    zero_cnt = jnp.zeros((1, LANES), F32)
    y2_p, h_p, gates_p, lpos_p, tabs_p, cnt_p = _cross_router(
        xp, r_p.reshape(t_p, WIDTH), a_p.reshape(t_p, WIDTH), w_o_bf, vec(g_cross[l]), w_mq_bf,
        vec(g_q_mem[l]), mk_bf.reshape(b, N_MEM, D_MODEL), mv_bf.reshape(b, N_MEM, D_MODEL), w_mo_bf,
        vec(g_moe[l]), w_r_hi, w_r_lo, b_r, zero_cnt, n_batch=b, tq=CROSS_TILE, n_split=CROSS_SPLIT,
        sorted_mode=True)

    xs_ = x_sample.reshape(t_s, D_MODEL)
    qr, kr, vr, gr, qa, ka_s, va_s = _mix_proj(xs_, vec(g_mix[l]), w_in_bf, gq_t, gk_t, tm=t_s,
                                               act_dtype=F32)
    pos_minor = lambda c: jnp.transpose(c, (0, 2, 3, 1)).reshape(bs, WIDTH, c.shape[1])
    r_s, a_s, state_s = _sample_mixer(lg_lane, gout_lane, slope_lane, qr, kr, vr, gr, qa, ka_s, va_s,
                                      state_ret[l], pos_minor(cache_win_k[l]),
                                      pos_minor(cache_win_v[l]), n_new=ns)
    y2_s, hp_s, gates_s, pairs_s, cnt_all = _cross_router(
        xs_, r_s, a_s, w_o_bf, vec(g_cross[l]), w_mq_bf, vec(g_q_mem[l]),
        cache_mem_k[l], cache_mem_v[l],
        w_mo_bf, vec(g_moe[l]), w_r_hi, w_r_lo, b_r, cnt_p, n_batch=bs, tq=min(bs, SAMPLE_MEMS_PER_STEP) * ns, n_split=1,
        sorted_mode=False)

    tile = MOE_ROW_TILE
    lanes_e = slice(EXPERT_LANE0, EXPERT_LANE0 + N_EXPERTS)
    counts = cnt_all[0, lanes_e].astype(I32)
    tiles_per = (counts + tile - 1) // tile
    tile_end = jnp.cumsum(tiles_per)
    offsets = (tile_end - tiles_per) * tile
    n_tiles = tile_end[-1:]
    n_blocks = t_p // ROUTE_BLOCK
    n_max = (2 * (t_p + t_s) + n_blocks * N_EXPERTS * (RUN_ALIGN - 1)) // tile + N_EXPERTS
    tile_ids = jnp.minimum(jnp.arange(n_max, dtype=I32), n_tiles[0] - 1)
    tile_expert = jnp.sum((tile_end[None, :] <= tile_ids[:, None]).astype(I32), axis=1)
    seg_end = tile_end * tile
    first_gran = (offsets + cnt_p[0, lanes_e].astype(I32)) // ZERO_ROWS
    n_gran = (t_s + tile) // ZERO_ROWS + 1
    gran = first_gran[None, :] + jnp.arange(n_gran, dtype=I32)[:, None]
    zero_tiles = jnp.where(gran * ZERO_ROWS < seg_end[None, :], gran, -1).reshape(-1)
    first_tail_gran = n_tiles * (tile // ZERO_ROWS)
    run_chunks = (tabs_p[:, 0, lanes_e].astype(I32) + RUN_ALIGN - 1) // RUN_ALIGN
    run_end = jnp.cumsum(run_chunks, axis=1)
    chunk_total = run_end[:, -1]
    chunk_id = jnp.arange(SORT_CHUNKS, dtype=I32)
    owner = (chunk_id[None, :, None] >= run_end[:, None, :]).astype(I32).sum(axis=2)
    owner_hot = owner[:, :, None] == jnp.arange(N_EXPERTS, dtype=I32)[None, None, :]
    run_row0 = offsets[None, :] + tabs_p[:, 2, lanes_e].astype(I32) - (run_end - run_chunks) * RUN_ALIGN
    chunk_rows = (jnp.where(owner_hot, run_row0[:, None, :], 0).sum(axis=2)
                  + chunk_id[None, :] * RUN_ALIGN).reshape(-1)
    xs_sorted = _sorted_dispatch(chunk_total, chunk_rows, zero_tiles, first_tail_gran, h_p, lpos_p,
                                 n_max * tile)
    xs_sorted = _dispatch(offsets, pairs_s, hp_s, xs_sorted, tq=t_s)
    ys_sorted = _experts(tile_expert, n_tiles, xs_sorted, w_exp_gate[l], w_exp_up[l], w_exp_down[l])
    y_p = _sorted_combine(chunk_total, chunk_rows, ys_sorted, y2_p, gates_p, rb=ROUTE_BLOCK)
    y_s = _combine(offsets, pairs_s, ys_sorted, y2_s, gates_s, tq=t_s)

    from_pos_minor = lambda z: jnp.transpose(z.reshape(b, N_HEADS, HEAD_DIM, s), (0, 3, 1, 2))[None]
    return (y_p.reshape(b, s, D_MODEL), y_s.reshape(bs, ns, D_MODEL),
            state_p[None],
            from_pos_minor(ka_t), from_pos_minor(va_t),
            mk_p.reshape(1, b, N_MEM, N_MEM_HEADS, MEM_HEAD_DIM),
            mv_p.reshape(1, b, N_MEM, N_MEM_HEADS, MEM_HEAD_DIM),
            state_s[None],
            ka_s.reshape(1, bs, ns, N_HEADS, HEAD_DIM), va_s.reshape(1, bs, ns, N_HEADS, HEAD_DIM))
```
